```python
import math
import jax, jax.numpy as jnp
from jax import lax
import numpy as np

D_MODEL = 1024
BATCH = 16
SEQ = 4096
DEPTH = 2

HEAD_DIM = 64
MIX_WIDTH = D_MODEL
POOL_WIDTH = MIX_WIDTH // 4
POOL_GROUPS = 4
POOL_GROUP_DIM = POOL_WIDTH // POOL_GROUPS
POOL_SIZES = (2, 4, 8, 16)
GDN_HEADS = (MIX_WIDTH - POOL_WIDTH) // (2 * HEAD_DIM)
GDN_WIDTH = GDN_HEADS * HEAD_DIM
SWA_HEADS = (MIX_WIDTH - POOL_WIDTH - GDN_WIDTH) // HEAD_DIM
SWA_WIDTH = SWA_HEADS * HEAD_DIM
IN_WIDTH = 4 * GDN_WIDTH + 2 * GDN_HEADS + 3 * SWA_WIDTH + POOL_WIDTH
CONV_K = 4
GDN_CHUNK = 64
DILATED_PAIRS = ((128, 1), (512, 4), (2048, 16))
ROPE_THETA = 10000.0
RMS_EPS = 1e-6
D_FF = 11 * D_MODEL // 4
N_EXPERTS = 8
TOP_K = 2
D_EXPERT = 7 * D_MODEL // 2
MOE_BLOCK = 512

kernel_name = "hybrid_gdn_dilated_pool_moe_block"


def rms_norm(x, g):
    xf = x.astype(jnp.float32)
    y = xf * lax.rsqrt(jnp.mean(xf * xf, axis=-1, keepdims=True) + RMS_EPS)
    return (y * g.astype(jnp.float32)).astype(x.dtype)


def l2_normalize(x):
    return x * lax.rsqrt(jnp.sum(x * x, axis=-1, keepdims=True) + RMS_EPS)


def split_columns(t, widths):
    offsets = np.cumsum(widths)[:-1].tolist()
    return jnp.split(t, offsets, axis=-1)


def swiglu(h, w_gate, w_up, w_down):
    return (jax.nn.silu(h @ w_gate) * (h @ w_up)) @ w_down


def causal_depthwise_conv(u, w):
    k_len, chans = w.shape
    return lax.conv_general_dilated(
        u, w[:, None, :].astype(u.dtype), window_strides=(1,),
        padding=[(k_len - 1, 0)], dimension_numbers=('NWC', 'WIO', 'NWC'),
        feature_group_count=chans)


def rotary(x):
    s_len, hd = x.shape[1], x.shape[-1]
    inv_freq = ROPE_THETA ** (-jnp.arange(0, hd, 2, dtype=jnp.float32) / hd)
    ang = jnp.arange(s_len, dtype=jnp.float32)[:, None] * inv_freq[None, :]
    cos = jnp.cos(ang)[None, :, None, :]
    sin = jnp.sin(ang)[None, :, None, :]
    x1, x2 = jnp.split(x, 2, axis=-1)
    return jnp.concatenate([x1 * cos - x2 * sin, x2 * cos + x1 * sin], axis=-1)


def gated_delta_rule(q, k, v, log_decay, beta):
    bsz, s_len, n_h, dk = q.shape
    dv = v.shape[-1]
    c = GDN_CHUNK
    n_chunks = s_len // c

    def chunks(t):
        t = jnp.moveaxis(t, 2, 1)
        return t.reshape((bsz, n_h, n_chunks, c) + t.shape[3:])

    q, k, v, g, beta = (chunks(t) for t in (q, k, v, log_decay, beta))
    g = jnp.cumsum(g, axis=-1)
    causal = jnp.tril(jnp.ones((c, c), bool))
    strict = jnp.tril(jnp.ones((c, c), bool), -1)
    diff = g[..., :, None] - g[..., None, :]
    decay = jnp.where(causal, jnp.exp(jnp.where(causal, diff, 0.0)), 0.0)
    k_beta = k * beta[..., None]
    a = jnp.where(strict, jnp.einsum('bhnik,bhnjk->bhnij', k_beta, k) * decay, 0.0)
    system = a + jnp.eye(c, dtype=a.dtype)
    u = lax.linalg.triangular_solve(system, v * beta[..., None], left_side=True,
                                    lower=True, unit_diagonal=True)
    w = lax.linalg.triangular_solve(system, k_beta * jnp.exp(g)[..., None], left_side=True,
                                    lower=True, unit_diagonal=True)
    attn = jnp.einsum('bhnik,bhnjk->bhnij', q, k) * decay
    q_dec = q * jnp.exp(g)[..., None]
    g_last = g[..., -1]
    k_dec = k * jnp.exp(g_last[..., None] - g)[..., None]

    def step(state, inp):
        qd, kd, uc, wc, ac, gl = inp
        v_new = uc - jnp.einsum('bhck,bhkv->bhcv', wc, state)
        o = jnp.einsum('bhck,bhkv->bhcv', qd, state) + jnp.einsum('bhij,bhjv->bhiv', ac, v_new)
        state = state * jnp.exp(gl)[..., None, None] + jnp.einsum('bhck,bhcv->bhkv', kd, v_new)
        return state, o

    xs = tuple(jnp.moveaxis(t, 2, 0) for t in (q_dec, k_dec, u, w, attn, g_last))
    state0 = jnp.zeros((bsz, n_h, dk, dv), q.dtype)
    _, o = lax.scan(step, state0, xs)
    o = jnp.moveaxis(o, 0, 2).reshape(bsz, n_h, s_len, dv)
    return jnp.moveaxis(o, 1, 2)


def dilated_window_attention(q, k, v, window, dilation):
    bsz, s_len, n_h, hd = q.shape
    span = window // dilation
    blk = span
    sub_len = s_len // dilation
    n_blk = -(-sub_len // blk)
    sub_pad = n_blk * blk

    def to_blocks(t):
        t = t.reshape(bsz, sub_len, dilation, n_h, hd).transpose(0, 2, 3, 1, 4)
        t = jnp.pad(t, ((0, 0), (0, 0), (0, 0), (0, sub_pad - sub_len), (0, 0)))
        return t.reshape(bsz, dilation, n_h, n_blk, blk, hd)

    def with_previous(t):
        prev = jnp.pad(t, ((0, 0), (0, 0), (0, 0), (1, 0), (0, 0), (0, 0)))[:, :, :, :n_blk]
        return jnp.concatenate([prev, t], axis=4)

    qb = to_blocks(q)
    kk = with_previous(to_blocks(k))
    vv = with_previous(to_blocks(v))
    s = jnp.einsum('bdhnqe,bdhnke->bdhnqk', qb, kk) * (hd ** -0.5)
    q_pos = blk + jnp.arange(blk)
    k_pos = jnp.arange(2 * blk)
    dist = q_pos[:, None] - k_pos[None, :]
    band = (dist >= 0) & (dist <= span)
    exists = (jnp.arange(n_blk)[:, None] > 0) | (k_pos[None, :] >= blk)
    valid = band[None, :, :] & exists[:, None, :]
    s = jnp.where(valid, s, -jnp.inf)
    m = jnp.max(s, axis=-1, keepdims=True)
    p = jnp.exp(s - m)
    den = jnp.sum(p, axis=-1, keepdims=True)
    o = jnp.einsum('bdhnqk,bdhnke->bdhnqe', p, vv) / den
    lse = (m + jnp.log(den))[..., 0]
    o = o.reshape(bsz, dilation, n_h, sub_pad, hd)[:, :, :, :sub_len]
    o = o.transpose(0, 3, 1, 2, 4).reshape(bsz, s_len, n_h, hd)
    lse = lse.reshape(bsz, dilation, n_h, sub_pad)[..., :sub_len]
    lse = lse.transpose(0, 3, 1, 2).reshape(bsz, s_len, n_h)
    return o, lse


def dilated_mixture_attention(q, k, v):
    outs, lses = [], []
    for window, dilation in DILATED_PAIRS:
        o, lse = dilated_window_attention(q, k, v, window, dilation)
        outs.append(o)
        lses.append(lse)
    weights = jax.nn.softmax(jnp.stack(lses, axis=0), axis=0)
    return jnp.einsum('pbsh,pbshe->bshe', weights, jnp.stack(outs, axis=0))


def multiscale_pool(u, w_grp, scale):
    bsz, s_len, _ = u.shape
    uf = u.astype(jnp.float32).reshape(bsz, s_len, POOL_GROUPS, POOL_GROUP_DIM)
    cs = jnp.pad(jnp.cumsum(uf, axis=1), ((0, 0), (1, 0), (0, 0), (0, 0)))
    t = jnp.arange(s_len)
    pooled = []
    for gi, size in enumerate(POOL_SIZES):
        csg = cs[:, :, gi]
        lag = jnp.pad(csg, ((0, 0), (size, 0), (0, 0)))[:, 1:s_len + 1]
        count = jnp.minimum(t + 1, size).astype(jnp.float32)[None, :, None]
        pooled.append((csg[:, 1:] - lag) / count - uf[:, :, gi])
    pooled = jnp.stack(pooled, axis=2)
    y = jnp.einsum('bsgc,gcd->bsgd', pooled, w_grp.astype(jnp.float32))
    y = y.reshape(bsz, s_len, POOL_WIDTH) * scale.astype(jnp.float32)
    return y.astype(u.dtype)


def moe_swiglu(h, router_w, router_b, w_gate, w_up, w_down):
    bsz, s_len, d = h.shape
    n_tok = bsz * s_len
    n_assign = n_tok * TOP_K
    tokens = h.reshape(n_tok, d)
    logits = (tokens @ router_w).astype(jnp.float32) + router_b.astype(jnp.float32)
    top_logits, top_expert = lax.top_k(logits, TOP_K)
    gates = jax.nn.softmax(top_logits, axis=-1)
    flat_expert = top_expert.reshape(-1)
    flat_token = jnp.arange(n_assign, dtype=jnp.int32) // TOP_K
    order = jnp.argsort(flat_expert, stable=True)
    s_expert = flat_expert[order]
    s_token = flat_token[order]
    s_gate = gates.reshape(-1)[order]
    counts = jnp.bincount(flat_expert, length=N_EXPERTS)
    padded = (counts + MOE_BLOCK - 1) // MOE_BLOCK * MOE_BLOCK
    padded_end = jnp.cumsum(padded)
    padded_start = padded_end - padded
    start = jnp.cumsum(counts) - counts
    slot = padded_start[s_expert] + jnp.arange(n_assign) - start[s_expert]
    n_slots = (-(-n_assign // MOE_BLOCK) + N_EXPERTS) * MOE_BLOCK
    n_blocks = n_slots // MOE_BLOCK
    slot_token = jnp.full((n_slots,), n_tok, jnp.int32).at[slot].set(s_token)
    block_start = jnp.arange(n_blocks) * MOE_BLOCK
    block_expert = jnp.minimum(
        jnp.sum(block_start[:, None] >= padded_end[None, :], axis=1), N_EXPERTS - 1)
    tokens_pad = jnp.concatenate([tokens, jnp.zeros((1, d), tokens.dtype)], axis=0)

    def expert_block(args):
        idx, e = args
        return swiglu(tokens_pad[idx], w_gate[e], w_up[e], w_down[e])

    y_slots = lax.map(expert_block, (slot_token.reshape(n_blocks, MOE_BLOCK), block_expert))
    y_slots = y_slots.reshape(n_slots, d)
    y = y_slots[slot] * s_gate[:, None].astype(y_slots.dtype)
    out = jax.ops.segment_sum(y, s_token, num_segments=n_tok)
    return out.reshape(bsz, s_len, d).astype(h.dtype)


def setup_inputs(seed: int = 0) -> dict:
    key = jax.random.key(seed)
    ks = jax.random.split(key, 21)
    f32 = jnp.float32
    n_dense = (DEPTH + 1) // 2
    n_moe = DEPTH // 2

    def normal(k, shape, scale):
        return jax.random.normal(k, shape, f32) * scale

    def gain(k, shape):
        return 1.0 + normal(k, shape, 0.02)

    dt = jnp.exp(jax.random.uniform(ks[5], (DEPTH, GDN_HEADS), f32,
                                    math.log(1e-3), math.log(1e-1)))
    return {
        'x': normal(ks[0], (BATCH, SEQ, D_MODEL), 1.0),
        'norm_mix': gain(ks[1], (DEPTH, D_MODEL)),
        'w_in': normal(ks[2], (DEPTH, D_MODEL, IN_WIDTH), D_MODEL ** -0.5),
        'conv_w': normal(ks[3], (DEPTH, CONV_K, 3 * GDN_WIDTH), CONV_K ** -0.5),
        'a_log': jnp.log(jax.random.uniform(ks[4], (DEPTH, GDN_HEADS), f32, 1.0, 16.0)),
        'dt_bias': dt + jnp.log(-jnp.expm1(-dt)),
        'gdn_norm': gain(ks[6], (DEPTH, HEAD_DIM)),
        'q_norm': gain(ks[7], (DEPTH, HEAD_DIM)),
        'k_norm': gain(ks[8], (DEPTH, HEAD_DIM)),
        'pool_w': normal(ks[9], (DEPTH, POOL_GROUPS, POOL_GROUP_DIM, POOL_GROUP_DIM),
                         POOL_GROUP_DIM ** -0.5),
        'pool_scale': gain(ks[10], (DEPTH, POOL_WIDTH)),
        'w_out': normal(ks[11], (DEPTH, MIX_WIDTH, D_MODEL), MIX_WIDTH ** -0.5),
        'norm_ffn': gain(ks[12], (DEPTH, D_MODEL)),
        'ffn_gate': normal(ks[13], (n_dense, D_MODEL, D_FF), D_MODEL ** -0.5),
        'ffn_up': normal(ks[14], (n_dense, D_MODEL, D_FF), D_MODEL ** -0.5),
        'ffn_down': normal(ks[15], (n_dense, D_FF, D_MODEL), D_FF ** -0.5),
        'router_w': normal(ks[16], (n_moe, D_MODEL, N_EXPERTS), D_MODEL ** -0.5),
        'router_b': normal(ks[17], (n_moe, N_EXPERTS), 0.01),
        'exp_gate': normal(ks[18], (n_moe, N_EXPERTS, D_MODEL, D_EXPERT), D_MODEL ** -0.5),
        'exp_up': normal(ks[19], (n_moe, N_EXPERTS, D_MODEL, D_EXPERT), D_MODEL ** -0.5),
        'exp_down': normal(ks[20], (n_moe, N_EXPERTS, D_EXPERT, D_MODEL), D_EXPERT ** -0.5),
    }


def reference(x, norm_mix, w_in, conv_w, a_log, dt_bias, gdn_norm, q_norm, k_norm,
              pool_w, pool_scale, w_out, norm_ffn, ffn_gate, ffn_up, ffn_down,
              router_w, router_b, exp_gate, exp_up, exp_down):
    bsz, s_len, _ = x.shape

    def heads(t):
        return t.reshape(bsz, s_len, -1, HEAD_DIM)

    for layer in range(DEPTH):
        h = rms_norm(x, norm_mix[layer])
        proj = h @ w_in[layer]
        aq, ak, av, az, a_dt, a_beta, bq, bk, bv, cu = split_columns(
            proj, (GDN_WIDTH,) * 4 + (GDN_HEADS,) * 2 + (SWA_WIDTH,) * 3 + (POOL_WIDTH,))

        qkv = jax.nn.silu(causal_depthwise_conv(jnp.concatenate([aq, ak, av], axis=-1),
                                                conv_w[layer]))
        aq, ak, av = jnp.split(qkv.astype(jnp.float32), 3, axis=-1)
        qa = l2_normalize(heads(aq)) * (HEAD_DIM ** -0.5)
        ka = l2_normalize(heads(ak))
        log_decay = -jnp.exp(a_log[layer].astype(jnp.float32)) * jax.nn.softplus(
            a_dt.astype(jnp.float32) + dt_bias[layer].astype(jnp.float32))
        beta = jax.nn.sigmoid(a_beta.astype(jnp.float32))
        ya = gated_delta_rule(qa, ka, heads(av), log_decay, beta)
        ya = rms_norm(ya, gdn_norm[layer]) * jax.nn.silu(heads(az).astype(jnp.float32))
        ya = ya.reshape(bsz, s_len, GDN_WIDTH).astype(x.dtype)

        qb = rotary(rms_norm(heads(bq), q_norm[layer]).astype(jnp.float32))
        kb = rotary(rms_norm(heads(bk), k_norm[layer]).astype(jnp.float32))
        yb = dilated_mixture_attention(qb, kb, heads(bv).astype(jnp.float32))
        yb = yb.reshape(bsz, s_len, SWA_WIDTH).astype(x.dtype)

        yc = multiscale_pool(cu, pool_w[layer], pool_scale[layer])

        x = x + jnp.concatenate([ya, yb, yc], axis=-1) @ w_out[layer]

        h = rms_norm(x, norm_ffn[layer])
        i = layer // 2
        if layer % 2 == 0:
            x = x + swiglu(h, ffn_gate[i], ffn_up[i], ffn_down[i])
        else:
            x = x + moe_swiglu(h, router_w[i], router_b[i], exp_gate[i], exp_up[i], exp_down[i])
    return x
```

```python
import functools
import math

import jax
import jax.numpy as jnp
import numpy as np
from jax import lax
from jax.experimental import pallas as pl
from jax.experimental.pallas import tpu as pltpu

D_MODEL = 1024
HEAD_DIM = 64
MIX_WIDTH = D_MODEL
POOL_WIDTH = MIX_WIDTH // 4
POOL_GROUPS = 4
POOL_GROUP_DIM = POOL_WIDTH // POOL_GROUPS
POOL_SIZES = (2, 4, 8, 16)
GDN_HEADS = (MIX_WIDTH - POOL_WIDTH) // (2 * HEAD_DIM)
GDN_WIDTH = GDN_HEADS * HEAD_DIM
SWA_HEADS = (MIX_WIDTH - POOL_WIDTH - GDN_WIDTH) // HEAD_DIM
SWA_WIDTH = SWA_HEADS * HEAD_DIM
CONV_K = 4
GDN_CHUNK = 64
DILATED_PAIRS = ((128, 1), (512, 4), (2048, 16))
ROPE_THETA = 10000.0
RMS_EPS = 1e-6
N_EXPERTS = 8
TOP_K = 2

VMEM_LIMIT_BYTES = 56 * 1024 * 1024
LANES = 128

SMALL_OFF = 7 * GDN_WIDTH
POOL_OFF = SMALL_OFF + LANES
PROJ_WIDTH = POOL_OFF + POOL_WIDTH


def _row_tile(n_rows, want):
    t = min(want, n_rows)
    while n_rows % t:
        t //= 2
    return t


def _params(*sem):
    return pltpu.CompilerParams(dimension_semantics=sem, vmem_limit_bytes=VMEM_LIMIT_BYTES)


def _resident(shape):
    zeros = (0,) * len(shape)
    return pl.BlockSpec(shape, lambda *_: zeros, pipeline_mode=pl.Buffered(1))


def _norm_inproj_kernel(x_ref, g_ref, w_ref, o_ref):
    x = x_ref[...]
    ms = jnp.mean(x * x, axis=-1, keepdims=True)
    h = (x * lax.rsqrt(ms + RMS_EPS) * g_ref[...]).astype(jnp.bfloat16)
    o_ref[...] = jnp.dot(h, w_ref[...], preferred_element_type=jnp.float32)


def norm_inproj(x2, g, w):
    n, d = x2.shape
    width = w.shape[1]
    tm = _row_tile(n, 512)
    return pl.pallas_call(
        _norm_inproj_kernel,
        grid=(n // tm,),
        in_specs=[pl.BlockSpec((tm, d), lambda i: (i, 0)),
                  _resident((1, d)),
                  _resident((d, width))],
        out_specs=pl.BlockSpec((tm, width), lambda i: (i, 0)),
        out_shape=jax.ShapeDtypeStruct((n, width), jnp.float32),
        compiler_params=_params("parallel"),
        name="norm_inproj",
    )(x2, g.reshape(1, d), w)


def _arrange_w_in(w):
    gw, gh, sw = GDN_WIDTH, GDN_HEADS, SWA_WIDTH
    a_main = w[:, :4 * gw]
    small = w[:, 4 * gw:4 * gw + 2 * gh]
    b_main = w[:, 4 * gw + 2 * gh:4 * gw + 2 * gh + 3 * sw]
    cu = w[:, 4 * gw + 2 * gh + 3 * sw:]
    small = jnp.pad(small, ((0, 0), (0, LANES - 2 * gh)))
    return jnp.concatenate([a_main, b_main, small, cu], axis=1).astype(jnp.bfloat16)


def _outproj_kernel(x_ref, ya_ref, yb_ref, yc_ref, wa_ref, wb_ref, wc_ref, o_ref):
    acc = x_ref[...]
    acc += jnp.dot(ya_ref[...].astype(jnp.bfloat16), wa_ref[...], preferred_element_type=jnp.float32)
    acc += jnp.dot(yb_ref[...].astype(jnp.bfloat16), wb_ref[...], preferred_element_type=jnp.float32)
    acc += jnp.dot(yc_ref[...].astype(jnp.bfloat16), wc_ref[...], preferred_element_type=jnp.float32)
    o_ref[...] = acc


def outproj_residual(x2, ya, yb, yc, w_out):
    n, d = x2.shape
    tm = _row_tile(n, 1024)
    wa = w_out[:GDN_WIDTH].astype(jnp.bfloat16)
    wb = w_out[GDN_WIDTH:GDN_WIDTH + SWA_WIDTH].astype(jnp.bfloat16)
    wc = w_out[GDN_WIDTH + SWA_WIDTH:].astype(jnp.bfloat16)
    row = lambda width: pl.BlockSpec((tm, width), lambda i: (i, 0))
    return pl.pallas_call(
        _outproj_kernel,
        grid=(n // tm,),
        in_specs=[row(d), row(GDN_WIDTH), row(SWA_WIDTH), row(POOL_WIDTH),
                  _resident(wa.shape), _resident(wb.shape), _resident(wc.shape)],
        out_specs=row(d),
        out_shape=jax.ShapeDtypeStruct((n, d), jnp.float32),
        compiler_params=_params("parallel"),
        name="outproj_residual",
    )(x2, ya, yb, yc, wa, wb, wc)


def _ffn_kernel(n_chunks, x_ref, g_ref, wg_ref, wu_ref, wd_ref, o_ref):
    x = x_ref[...]
    ms = jnp.mean(x * x, axis=-1, keepdims=True)
    h = (x * lax.rsqrt(ms + RMS_EPS) * g_ref[...]).astype(jnp.bfloat16)
    tc = wg_ref.shape[1] // n_chunks
    acc = x
    for c in range(n_chunks):
        gate = jnp.dot(h, wg_ref[:, c * tc:(c + 1) * tc], preferred_element_type=jnp.float32)
        up = jnp.dot(h, wu_ref[:, c * tc:(c + 1) * tc], preferred_element_type=jnp.float32)
        act = (gate * jax.nn.sigmoid(gate) * up).astype(jnp.bfloat16)
        acc = acc + jnp.dot(act, wd_ref[c * tc:(c + 1) * tc, :], preferred_element_type=jnp.float32)
    o_ref[...] = acc


def ffn_dense(x2, g, w_gate, w_up, w_down):
    n, d = x2.shape
    d_ff = w_gate.shape[1]
    tm = _row_tile(n, 512)
    n_chunks = 2 if d_ff % (2 * LANES) == 0 else 1
    row = pl.BlockSpec((tm, d), lambda i: (i, 0))
    return pl.pallas_call(
        functools.partial(_ffn_kernel, n_chunks),
        grid=(n // tm,),
        in_specs=[row, _resident((1, d)), _resident((d, d_ff)), _resident((d, d_ff)),
                  _resident((d_ff, d))],
        out_specs=row,
        out_shape=jax.ShapeDtypeStruct((n, d), jnp.float32),
        compiler_params=_params("parallel"),
        name="ffn_dense",
    )(x2, g.reshape(1, d), w_gate.astype(jnp.bfloat16), w_up.astype(jnp.bfloat16),
      w_down.astype(jnp.bfloat16))


def _router_kernel(x_ref, g_ref, wr_ref, br_ref, h_ref, logit_ref):
    x = x_ref[...]
    ms = jnp.mean(x * x, axis=-1, keepdims=True)
    h = x * lax.rsqrt(ms + RMS_EPS) * g_ref[...]
    h_ref[...] = h.astype(jnp.bfloat16)
    logit_ref[...] = jnp.dot(h, wr_ref[...], preferred_element_type=jnp.float32,
                             precision=lax.Precision.HIGHEST) + br_ref[...]


def norm_router(x2, g, router_w, router_b):
    n, d = x2.shape
    tm = _row_tile(n, 1024)
    wr = jnp.pad(router_w, ((0, 0), (0, LANES - N_EXPERTS)))
    br = jnp.pad(router_b, (0, LANES - N_EXPERTS)).reshape(1, LANES)
    return pl.pallas_call(
        _router_kernel,
        grid=(n // tm,),
        in_specs=[pl.BlockSpec((tm, d), lambda i: (i, 0)), _resident((1, d)),
                  _resident((d, LANES)), _resident((1, LANES))],
        out_specs=[pl.BlockSpec((tm, d), lambda i: (i, 0)),
                   pl.BlockSpec((tm, LANES), lambda i: (i, 0))],
        out_shape=[jax.ShapeDtypeStruct((n, d), jnp.bfloat16),
                   jax.ShapeDtypeStruct((n, LANES), jnp.float32)],
        compiler_params=_params("parallel"),
        name="norm_router",
    )(x2, g.reshape(1, d), wr, br)


def _moe_kernel(be_ref, nb_ref, xs_ref, wg_ref, wu_ref, wd_ref, o_ref):
    j = pl.program_id(0)
    c = pl.program_id(1)

    @pl.when(j < nb_ref[0])
    def _():
        h = xs_ref[...]
        gate = jnp.dot(h, wg_ref[0], preferred_element_type=jnp.float32)
        up = jnp.dot(h, wu_ref[0], preferred_element_type=jnp.float32)
        act = (gate * jax.nn.sigmoid(gate) * up).astype(jnp.bfloat16)
        part = jnp.dot(act, wd_ref[0], preferred_element_type=jnp.float32)

        @pl.when(c == 0)
        def _():
            o_ref[...] = part

        @pl.when(c > 0)
        def _():
            o_ref[...] += part

    @pl.when(jnp.logical_and(j >= nb_ref[0], c == 0))
    def _():
        o_ref[...] = jnp.zeros_like(o_ref)


def moe_experts(xs, block_expert, n_used, w_gate, w_up, w_down, tm, tc):
    n_slots, d = xs.shape
    d_e = w_gate.shape[2]
    n_blocks = n_slots // tm
    n_c = d_e // tc

    def live(j, nb):
        return jnp.minimum(j, nb[0] - 1)

    return pl.pallas_call(
        _moe_kernel,
        grid_spec=pltpu.PrefetchScalarGridSpec(
            num_scalar_prefetch=2,
            grid=(n_blocks, n_c),
            in_specs=[
                pl.BlockSpec((tm, d), lambda j, c, be, nb: (live(j, nb), 0)),
                pl.BlockSpec((1, d, tc), lambda j, c, be, nb: (be[live(j, nb)], 0, jnp.where(j < nb[0], c, n_c - 1))),
                pl.BlockSpec((1, d, tc), lambda j, c, be, nb: (be[live(j, nb)], 0, jnp.where(j < nb[0], c, n_c - 1))),
                pl.BlockSpec((1, tc, d), lambda j, c, be, nb: (be[live(j, nb)], jnp.where(j < nb[0], c, n_c - 1), 0)),
            ],
            out_specs=pl.BlockSpec((tm, d), lambda j, c, be, nb: (j, 0)),
        ),
        out_shape=jax.ShapeDtypeStruct((n_slots, d), jnp.float32),
        compiler_params=_params("arbitrary", "arbitrary"),
        name="moe_experts",
    )(block_expert, n_used, xs, w_gate, w_up, w_down)


def moe_block(x2, g, router_w, router_b, w_gate, w_up, w_down):
    n, d = x2.shape
    tm = _row_tile(n, 1024)
    h, logits = norm_router(x2, g, router_w, router_b)
    logits = logits[:, :N_EXPERTS]
    top_logits, top_expert = lax.top_k(logits, TOP_K)
    gates = jax.nn.softmax(top_logits, axis=-1)
    flat_expert = top_expert.reshape(-1)
    n_assign = n * TOP_K
    onehot = (flat_expert[:, None] == jnp.arange(N_EXPERTS)[None, :]).astype(jnp.int32)
    rank = jnp.cumsum(onehot, axis=0) - onehot
    counts = jnp.sum(onehot, axis=0)
    padded = (counts + tm - 1) // tm * tm
    padded_end = jnp.cumsum(padded)
    padded_start = padded_end - padded
    slot = jnp.sum(onehot * (padded_start[None, :] + rank), axis=1)
    n_blocks = -(-n_assign // tm) + N_EXPERTS
    n_slots = n_blocks * tm
    flat_token = jnp.arange(n_assign, dtype=jnp.int32) // TOP_K
    slot_token = jnp.zeros((n_slots,), jnp.int32).at[slot].set(flat_token)
    block_start = jnp.arange(n_blocks) * tm
    block_expert = jnp.minimum(
        jnp.sum(block_start[:, None] >= padded_end[None, :], axis=1), N_EXPERTS - 1).astype(jnp.int32)
    n_used = (padded_end[-1] // tm).astype(jnp.int32).reshape(1)
    xs = jnp.take(h, slot_token, axis=0)
    d_e = w_gate.shape[2]
    tc = d_e // 4 if d_e % (4 * LANES) == 0 else d_e
    y_slots = moe_experts(xs, block_expert, n_used, w_gate.astype(jnp.bfloat16),
                          w_up.astype(jnp.bfloat16), w_down.astype(jnp.bfloat16), tm, tc)
    y = jnp.take(y_slots, slot, axis=0).reshape(n, TOP_K, d) * gates[:, :, None]
    return x2 + jnp.sum(y, axis=1)


def _rms_norm(x, g):
    return x * lax.rsqrt(jnp.mean(x * x, axis=-1, keepdims=True) + RMS_EPS) * g


def _l2_normalize(x):
    return x * lax.rsqrt(jnp.sum(x * x, axis=-1, keepdims=True) + RMS_EPS)


def _causal_depthwise_conv(u, w):
    k_len, chans = w.shape
    return lax.conv_general_dilated(
        u, w[:, None, :].astype(u.dtype), window_strides=(1,),
        padding=[(k_len - 1, 0)], dimension_numbers=('NWC', 'WIO', 'NWC'),
        feature_group_count=chans)


def _rotary(x):
    s_len, hd = x.shape[1], x.shape[-1]
    inv_freq = ROPE_THETA ** (-jnp.arange(0, hd, 2, dtype=jnp.float32) / hd)
    ang = jnp.arange(s_len, dtype=jnp.float32)[:, None] * inv_freq[None, :]
    cos = jnp.cos(ang)[None, :, None, :]
    sin = jnp.sin(ang)[None, :, None, :]
    x1, x2 = jnp.split(x, 2, axis=-1)
    return jnp.concatenate([x1 * cos - x2 * sin, x2 * cos + x1 * sin], axis=-1)


def _gated_delta_rule(q, k, v, log_decay, beta):
    bsz, s_len, n_h, dk = q.shape
    dv = v.shape[-1]
    c = GDN_CHUNK
    n_chunks = s_len // c

    def chunks(t):
        t = jnp.moveaxis(t, 2, 1)
        return t.reshape((bsz, n_h, n_chunks, c) + t.shape[3:])

    q, k, v, g, beta = (chunks(t) for t in (q, k, v, log_decay, beta))
    g = jnp.cumsum(g, axis=-1)
    causal = jnp.tril(jnp.ones((c, c), bool))
    strict = jnp.tril(jnp.ones((c, c), bool), -1)
    diff = g[..., :, None] - g[..., None, :]
    decay = jnp.where(causal, jnp.exp(jnp.where(causal, diff, 0.0)), 0.0)
    k_beta = k * beta[..., None]
    a = jnp.where(strict, jnp.einsum('bhnik,bhnjk->bhnij', k_beta, k) * decay, 0.0)
    system = a + jnp.eye(c, dtype=a.dtype)
    u = lax.linalg.triangular_solve(system, v * beta[..., None], left_side=True,
                                    lower=True, unit_diagonal=True)
    w = lax.linalg.triangular_solve(system, k_beta * jnp.exp(g)[..., None], left_side=True,
                                    lower=True, unit_diagonal=True)
    attn = jnp.einsum('bhnik,bhnjk->bhnij', q, k) * decay
    q_dec = q * jnp.exp(g)[..., None]
    g_last = g[..., -1]
    k_dec = k * jnp.exp(g_last[..., None] - g)[..., None]

    def step(state, inp):
        qd, kd, uc, wc, ac, gl = inp
        v_new = uc - jnp.einsum('bhck,bhkv->bhcv', wc, state)
        o = jnp.einsum('bhck,bhkv->bhcv', qd, state) + jnp.einsum('bhij,bhjv->bhiv', ac, v_new)
        state = state * jnp.exp(gl)[..., None, None] + jnp.einsum('bhck,bhcv->bhkv', kd, v_new)
        return state, o

    xs = tuple(jnp.moveaxis(t, 2, 0) for t in (q_dec, k_dec, u, w, attn, g_last))
    state0 = jnp.zeros((bsz, n_h, dk, dv), q.dtype)
    _, o = lax.scan(step, state0, xs)
    o = jnp.moveaxis(o, 0, 2).reshape(bsz, n_h, s_len, dv)
    return jnp.moveaxis(o, 1, 2)


def _dilated_window_attention(q, k, v, window, dilation):
    bsz, s_len, n_h, hd = q.shape
    span = window // dilation
    blk = span
    sub_len = s_len // dilation
    n_blk = -(-sub_len // blk)
    sub_pad = n_blk * blk

    def to_blocks(t):
        t = t.reshape(bsz, sub_len, dilation, n_h, hd).transpose(0, 2, 3, 1, 4)
        t = jnp.pad(t, ((0, 0), (0, 0), (0, 0), (0, sub_pad - sub_len), (0, 0)))
        return t.reshape(bsz, dilation, n_h, n_blk, blk, hd)

    def with_previous(t):
        prev = jnp.pad(t, ((0, 0), (0, 0), (0, 0), (1, 0), (0, 0), (0, 0)))[:, :, :, :n_blk]
        return jnp.concatenate([prev, t], axis=4)

    qb = to_blocks(q)
    kk = with_previous(to_blocks(k))
    vv = with_previous(to_blocks(v))
    s = jnp.einsum('bdhnqe,bdhnke->bdhnqk', qb, kk) * (hd ** -0.5)
    q_pos = blk + jnp.arange(blk)
    k_pos = jnp.arange(2 * blk)
    dist = q_pos[:, None] - k_pos[None, :]
    band = (dist >= 0) & (dist <= span)
    exists = (jnp.arange(n_blk)[:, None] > 0) | (k_pos[None, :] >= blk)
    valid = band[None, :, :] & exists[:, None, :]
    s = jnp.where(valid, s, -jnp.inf)
    m = jnp.max(s, axis=-1, keepdims=True)
    p = jnp.exp(s - m)
    den = jnp.sum(p, axis=-1, keepdims=True)
    o = jnp.einsum('bdhnqk,bdhnke->bdhnqe', p, vv) / den
    lse = (m + jnp.log(den))[..., 0]
    o = o.reshape(bsz, dilation, n_h, sub_pad, hd)[:, :, :, :sub_len]
    o = o.transpose(0, 3, 1, 2, 4).reshape(bsz, s_len, n_h, hd)
    lse = lse.reshape(bsz, dilation, n_h, sub_pad)[..., :sub_len]
    lse = lse.transpose(0, 3, 1, 2).reshape(bsz, s_len, n_h)
    return o, lse


def _dilated_mixture_attention(q, k, v):
    outs, lses = [], []
    for window, dilation in DILATED_PAIRS:
        o, lse = _dilated_window_attention(q, k, v, window, dilation)
        outs.append(o)
        lses.append(lse)
    weights = jax.nn.softmax(jnp.stack(lses, axis=0), axis=0)
    return jnp.einsum('pbsh,pbshe->bshe', weights, jnp.stack(outs, axis=0))


def _multiscale_pool(u, w_grp, scale):
    bsz, s_len, _ = u.shape
    uf = u.reshape(bsz, s_len, POOL_GROUPS, POOL_GROUP_DIM)
    cs = jnp.pad(jnp.cumsum(uf, axis=1), ((0, 0), (1, 0), (0, 0), (0, 0)))
    t = jnp.arange(s_len)
    pooled = []
    for gi, size in enumerate(POOL_SIZES):
        csg = cs[:, :, gi]
        lag = jnp.pad(csg, ((0, 0), (size, 0), (0, 0)))[:, 1:s_len + 1]
        count = jnp.minimum(t + 1, size).astype(jnp.float32)[None, :, None]
        pooled.append((csg[:, 1:] - lag) / count - uf[:, :, gi])
    pooled = jnp.stack(pooled, axis=2)
    y = jnp.einsum('bsgc,gcd->bsgd', pooled, w_grp)
    return y.reshape(bsz, s_len, POOL_WIDTH) * scale


def kernel(x, norm_mix, w_in, conv_w, a_log, dt_bias, gdn_norm, q_norm, k_norm, pool_w,
           pool_scale, w_out, norm_ffn, ffn_gate, ffn_up, ffn_down, router_w, router_b,
           exp_gate, exp_up, exp_down):
    bsz, s_len, d = x.shape
    n = bsz * s_len
    depth = w_in.shape[0]
    x2 = x.reshape(n, d)

    def heads(t):
        return t.reshape(bsz, s_len, -1, HEAD_DIM)

    gw = GDN_WIDTH
    for layer in range(depth):
        proj = norm_inproj(x2, norm_mix[layer], _arrange_w_in(w_in[layer]))
        proj3 = proj.reshape(bsz, s_len, PROJ_WIDTH)
        aqkv = proj3[..., :3 * gw]
        az = proj3[..., 3 * gw:4 * gw]
        bq, bk, bv = (proj3[..., (4 + i) * gw:(5 + i) * gw] for i in range(3))
        a_dt = proj3[..., SMALL_OFF:SMALL_OFF + GDN_HEADS]
        a_beta = proj3[..., SMALL_OFF + GDN_HEADS:SMALL_OFF + 2 * GDN_HEADS]
        cu = proj3[..., POOL_OFF:]

        qkv = jax.nn.silu(_causal_depthwise_conv(aqkv, conv_w[layer]))
        aq, ak, av = jnp.split(qkv, 3, axis=-1)
        qa = _l2_normalize(heads(aq)) * (HEAD_DIM ** -0.5)
        ka = _l2_normalize(heads(ak))
        log_decay = -jnp.exp(a_log[layer]) * jax.nn.softplus(a_dt + dt_bias[layer])
        beta = jax.nn.sigmoid(a_beta)
        ya = _gated_delta_rule(qa, ka, heads(av), log_decay, beta)
        ya = _rms_norm(ya, gdn_norm[layer]) * jax.nn.silu(heads(az))
        ya = ya.reshape(n, gw)

        qb = _rotary(_rms_norm(heads(bq), q_norm[layer]))
        kb = _rotary(_rms_norm(heads(bk), k_norm[layer]))
        yb = _dilated_mixture_attention(qb, kb, heads(bv)).reshape(n, SWA_WIDTH)

        yc = _multiscale_pool(cu, pool_w[layer], pool_scale[layer]).reshape(n, POOL_WIDTH)

        x2 = outproj_residual(x2, ya, yb, yc, w_out[layer])

        i = layer // 2
        if layer % 2 == 0:
            x2 = ffn_dense(x2, norm_ffn[layer], ffn_gate[i], ffn_up[i], ffn_down[i])
        else:
            x2 = moe_block(x2, norm_ffn[layer], router_w[i], router_b[i], exp_gate[i],
                           exp_up[i], exp_down[i])
    return x2.reshape(bsz, s_len, d)
```

```python
import functools
import math

import jax
import jax.numpy as jnp
import numpy as np
from jax import lax
from jax.experimental import pallas as pl
from jax.experimental.pallas import tpu as pltpu

D_MODEL = 1024
HEAD_DIM = 64
MIX_WIDTH = D_MODEL
POOL_WIDTH = MIX_WIDTH // 4
POOL_GROUPS = 4
POOL_GROUP_DIM = POOL_WIDTH // POOL_GROUPS
POOL_SIZES = (2, 4, 8, 16)
GDN_HEADS = (MIX_WIDTH - POOL_WIDTH) // (2 * HEAD_DIM)
GDN_WIDTH = GDN_HEADS * HEAD_DIM
SWA_HEADS = (MIX_WIDTH - POOL_WIDTH - GDN_WIDTH) // HEAD_DIM
SWA_WIDTH = SWA_HEADS * HEAD_DIM
CONV_K = 4
GDN_CHUNK = 64
DILATED_PAIRS = ((128, 1), (512, 4), (2048, 16))
ROPE_THETA = 10000.0
RMS_EPS = 1e-6
N_EXPERTS = 8
TOP_K = 2

VMEM_LIMIT_BYTES = 56 * 1024 * 1024
LANES = 128

SMALL_OFF = 7 * GDN_WIDTH
POOL_OFF = SMALL_OFF + LANES
PROJ_WIDTH = POOL_OFF + POOL_WIDTH


def _row_tile(n_rows, want):
    t = min(want, n_rows)
    while n_rows % t:
        t //= 2
    return t


def _params(*sem):
    return pltpu.CompilerParams(dimension_semantics=sem, vmem_limit_bytes=VMEM_LIMIT_BYTES)


def _resident(shape):
    zeros = (0,) * len(shape)
    return pl.BlockSpec(shape, lambda *_: zeros, pipeline_mode=pl.Buffered(1))


def _norm_inproj_kernel(x_ref, g_ref, w_ref, o_ref):
    x = x_ref[...]
    ms = jnp.mean(x * x, axis=-1, keepdims=True)
    h = (x * lax.rsqrt(ms + RMS_EPS) * g_ref[...]).astype(jnp.bfloat16)
    o_ref[...] = jnp.dot(h, w_ref[...], preferred_element_type=jnp.float32)


def norm_inproj(x2, g, w):
    n, d = x2.shape
    width = w.shape[1]
    tm = _row_tile(n, 512)
    return pl.pallas_call(
        _norm_inproj_kernel,
        grid=(n // tm,),
        in_specs=[pl.BlockSpec((tm, d), lambda i: (i, 0)),
                  _resident((1, d)),
                  _resident((d, width))],
        out_specs=pl.BlockSpec((tm, width), lambda i: (i, 0)),
        out_shape=jax.ShapeDtypeStruct((n, width), jnp.float32),
        compiler_params=_params("parallel"),
        name="norm_inproj",
    )(x2, g.reshape(1, d), w)


def _arrange_w_in(w):
    gw, gh, sw = GDN_WIDTH, GDN_HEADS, SWA_WIDTH
    a_main = w[:, :4 * gw]
    small = w[:, 4 * gw:4 * gw + 2 * gh]
    b_main = w[:, 4 * gw + 2 * gh:4 * gw + 2 * gh + 3 * sw]
    cu = w[:, 4 * gw + 2 * gh + 3 * sw:]
    small = jnp.pad(small, ((0, 0), (0, LANES - 2 * gh)))
    return jnp.concatenate([a_main, b_main, small, cu], axis=1).astype(jnp.bfloat16)


POOL_HIST = max(POOL_SIZES)


def _pool_outproj_kernel(x_ref, ya_ref, yb_ref, cu_ref, size_ref, pw_ref, ps_ref,
                         wa_ref, wb_ref, wc_ref, o_ref, ext_s):
    f32, bf16 = jnp.float32, jnp.bfloat16
    tm = x_ref.shape[0]
    t_blk = pl.program_id(1)

    @pl.when(t_blk == 0)
    def _():
        ext_s[0:POOL_HIST, :] = jnp.zeros((POOL_HIST, POOL_WIDTH), f32)

    cu = cu_ref[...]
    ext_s[POOL_HIST:POOL_HIST + tm, :] = cu
    ext = ext_s[...]
    ext_s[0:POOL_HIST, :] = ext_s[tm:tm + POOL_HIST, :]
    size = size_ref[...]
    win = ext
    total = jnp.zeros_like(ext)
    shift = 1
    while shift < POOL_HIST:
        win = win + pltpu.roll(win, shift, 0)
        shift *= 2
        total = jnp.where(size == shift, win, total)
    total = total[POOL_HIST:, :]
    pos = t_blk * tm + lax.broadcasted_iota(jnp.int32, (tm, 1), 0)
    count = jnp.minimum(pos + 1, size).astype(f32)
    pooled = total / count - cu
    yc = _dot(pooled.astype(bf16), pw_ref[...]) * ps_ref[...]

    acc = x_ref[...]
    acc += _dot(ya_ref[...].astype(bf16), wa_ref[...])
    acc += _dot(yb_ref[...].astype(bf16), wb_ref[...])
    acc += _dot(yc.astype(bf16), wc_ref[...])
    o_ref[...] = acc


def pool_outproj_residual(x2, ya, yb, proj, bsz, s_len, pool_w, pool_scale, w_out):
    n, d = x2.shape
    tm = _row_tile(s_len, 1024)
    nt = s_len // tm
    bf16 = jnp.bfloat16
    wa = w_out[:GDN_WIDTH].astype(bf16)
    wb = w_out[GDN_WIDTH:GDN_WIDTH + SWA_WIDTH].astype(bf16)
    wc = w_out[GDN_WIDTH + SWA_WIDTH:].astype(bf16)
    size = jnp.asarray(np.repeat(np.asarray(POOL_SIZES, np.int32), POOL_GROUP_DIM).reshape(1, POOL_WIDTH))
    pw = jax.scipy.linalg.block_diag(*[pool_w[g] for g in range(POOL_GROUPS)]).astype(bf16)
    row = lambda width, cb=0: pl.BlockSpec((tm, width), lambda b, t: (b * nt + t, cb))
    return pl.pallas_call(
        _pool_outproj_kernel,
        grid=(bsz, nt),
        in_specs=[row(d), row(GDN_WIDTH), row(SWA_WIDTH), row(POOL_WIDTH, POOL_OFF // POOL_WIDTH),
                  _resident((1, POOL_WIDTH)), _resident(pw.shape), _resident((1, POOL_WIDTH)),
                  _resident(wa.shape), _resident(wb.shape), _resident(wc.shape)],
        out_specs=row(d),
        out_shape=jax.ShapeDtypeStruct((n, d), jnp.float32),
        scratch_shapes=[pltpu.VMEM((tm + POOL_HIST, POOL_WIDTH), jnp.float32)],
        compiler_params=_params("parallel", "arbitrary"),
        name="pool_outproj_residual",
    )(x2, ya, yb, proj, size, pw, pool_scale.reshape(1, POOL_WIDTH), wa, wb, wc)


def _ffn_kernel(n_chunks, x_ref, g_ref, wg_ref, wu_ref, wd_ref, o_ref):
    x = x_ref[...]
    ms = jnp.mean(x * x, axis=-1, keepdims=True)
    h = (x * lax.rsqrt(ms + RMS_EPS) * g_ref[...]).astype(jnp.bfloat16)
    tc = wg_ref.shape[1] // n_chunks
    acc = x
    for c in range(n_chunks):
        gate = jnp.dot(h, wg_ref[:, c * tc:(c + 1) * tc], preferred_element_type=jnp.float32)
        up = jnp.dot(h, wu_ref[:, c * tc:(c + 1) * tc], preferred_element_type=jnp.float32)
        act = (gate * jax.nn.sigmoid(gate) * up).astype(jnp.bfloat16)
        acc = acc + jnp.dot(act, wd_ref[c * tc:(c + 1) * tc, :], preferred_element_type=jnp.float32)
    o_ref[...] = acc


def ffn_dense(x2, g, w_gate, w_up, w_down):
    n, d = x2.shape
    d_ff = w_gate.shape[1]
    tm = _row_tile(n, 512)
    n_chunks = 2 if d_ff % (2 * LANES) == 0 else 1
    row = pl.BlockSpec((tm, d), lambda i: (i, 0))
    return pl.pallas_call(
        functools.partial(_ffn_kernel, n_chunks),
        grid=(n // tm,),
        in_specs=[row, _resident((1, d)), _resident((d, d_ff)), _resident((d, d_ff)),
                  _resident((d_ff, d))],
        out_specs=row,
        out_shape=jax.ShapeDtypeStruct((n, d), jnp.float32),
        compiler_params=_params("parallel"),
        name="ffn_dense",
    )(x2, g.reshape(1, d), w_gate.astype(jnp.bfloat16), w_up.astype(jnp.bfloat16),
      w_down.astype(jnp.bfloat16))


def _router_kernel(x_ref, g_ref, wr_ref, br_ref, h_ref, logit_ref):
    x = x_ref[...]
    ms = jnp.mean(x * x, axis=-1, keepdims=True)
    h = x * lax.rsqrt(ms + RMS_EPS) * g_ref[...]
    h_ref[...] = h.astype(jnp.bfloat16)
    logit_ref[...] = jnp.dot(h, wr_ref[...], preferred_element_type=jnp.float32,
                             precision=lax.Precision.HIGHEST) + br_ref[...]


def norm_router(x2, g, router_w, router_b):
    n, d = x2.shape
    tm = _row_tile(n, 1024)
    wr = jnp.pad(router_w, ((0, 0), (0, LANES - N_EXPERTS)))
    br = jnp.pad(router_b, (0, LANES - N_EXPERTS)).reshape(1, LANES)
    return pl.pallas_call(
        _router_kernel,
        grid=(n // tm,),
        in_specs=[pl.BlockSpec((tm, d), lambda i: (i, 0)), _resident((1, d)),
                  _resident((d, LANES)), _resident((1, LANES))],
        out_specs=[pl.BlockSpec((tm, d), lambda i: (i, 0)),
                   pl.BlockSpec((tm, LANES), lambda i: (i, 0))],
        out_shape=[jax.ShapeDtypeStruct((n, d), jnp.bfloat16),
                   jax.ShapeDtypeStruct((n, LANES), jnp.float32)],
        compiler_params=_params("parallel"),
        name="norm_router",
    )(x2, g.reshape(1, d), wr, br)


def _moe_kernel(be_ref, nb_ref, xs_ref, wg_ref, wu_ref, wd_ref, o_ref):
    j = pl.program_id(0)
    c = pl.program_id(1)

    @pl.when(j < nb_ref[0])
    def _():
        h = xs_ref[...]
        gate = jnp.dot(h, wg_ref[0], preferred_element_type=jnp.float32)
        up = jnp.dot(h, wu_ref[0], preferred_element_type=jnp.float32)
        act = (gate * jax.nn.sigmoid(gate) * up).astype(jnp.bfloat16)
        part = jnp.dot(act, wd_ref[0], preferred_element_type=jnp.float32)

        @pl.when(c == 0)
        def _():
            o_ref[...] = part

        @pl.when(c > 0)
        def _():
            o_ref[...] += part

    @pl.when(jnp.logical_and(j >= nb_ref[0], c == 0))
    def _():
        o_ref[...] = jnp.zeros_like(o_ref)


def moe_experts(xs, block_expert, n_used, w_gate, w_up, w_down, tm, tc):
    n_slots, d = xs.shape
    d_e = w_gate.shape[2]
    n_blocks = n_slots // tm
    n_c = d_e // tc

    def live(j, nb):
        return jnp.minimum(j, nb[0] - 1)

    return pl.pallas_call(
        _moe_kernel,
        grid_spec=pltpu.PrefetchScalarGridSpec(
            num_scalar_prefetch=2,
            grid=(n_blocks, n_c),
            in_specs=[
                pl.BlockSpec((tm, d), lambda j, c, be, nb: (live(j, nb), 0)),
                pl.BlockSpec((1, d, tc), lambda j, c, be, nb: (be[live(j, nb)], 0, jnp.where(j < nb[0], c, n_c - 1))),
                pl.BlockSpec((1, d, tc), lambda j, c, be, nb: (be[live(j, nb)], 0, jnp.where(j < nb[0], c, n_c - 1))),
                pl.BlockSpec((1, tc, d), lambda j, c, be, nb: (be[live(j, nb)], jnp.where(j < nb[0], c, n_c - 1), 0)),
            ],
            out_specs=pl.BlockSpec((tm, d), lambda j, c, be, nb: (j, 0)),
        ),
        out_shape=jax.ShapeDtypeStruct((n_slots, d), jnp.float32),
        compiler_params=_params("arbitrary", "arbitrary"),
        name="moe_experts",
    )(block_expert, n_used, xs, w_gate, w_up, w_down)


def moe_block(x2, g, router_w, router_b, w_gate, w_up, w_down):
    n, d = x2.shape
    tm = _row_tile(n, 1024)
    h, logits = norm_router(x2, g, router_w, router_b)
    logits = logits[:, :N_EXPERTS]
    top_logits, top_expert = lax.top_k(logits, TOP_K)
    gates = jax.nn.softmax(top_logits, axis=-1)
    flat_expert = top_expert.reshape(-1)
    n_assign = n * TOP_K
    onehot = (flat_expert[:, None] == jnp.arange(N_EXPERTS)[None, :]).astype(jnp.int32)
    rank = jnp.cumsum(onehot, axis=0) - onehot
    counts = jnp.sum(onehot, axis=0)
    padded = (counts + tm - 1) // tm * tm
    padded_end = jnp.cumsum(padded)
    padded_start = padded_end - padded
    slot = jnp.sum(onehot * (padded_start[None, :] + rank), axis=1)
    n_blocks = -(-n_assign // tm) + N_EXPERTS
    n_slots = n_blocks * tm
    flat_token = jnp.arange(n_assign, dtype=jnp.int32) // TOP_K
    slot_token = jnp.zeros((n_slots,), jnp.int32).at[slot].set(flat_token)
    block_start = jnp.arange(n_blocks) * tm
    block_expert = jnp.minimum(
        jnp.sum(block_start[:, None] >= padded_end[None, :], axis=1), N_EXPERTS - 1).astype(jnp.int32)
    n_used = (padded_end[-1] // tm).astype(jnp.int32).reshape(1)
    xs = jnp.take(h, slot_token, axis=0)
    d_e = w_gate.shape[2]
    tc = d_e // 4 if d_e % (4 * LANES) == 0 else d_e
    y_slots = moe_experts(xs, block_expert, n_used, w_gate.astype(jnp.bfloat16),
                          w_up.astype(jnp.bfloat16), w_down.astype(jnp.bfloat16), tm, tc)
    y = jnp.take(y_slots, slot, axis=0).reshape(n, TOP_K, d) * gates[:, :, None]
    return x2 + jnp.sum(y, axis=1)


GDN_BLOCK = 512
CONV_PAD = 8

_HI = lax.Precision.HIGHEST


def _dot(a, b, precision=None):
    return jnp.dot(a, b, preferred_element_type=jnp.float32, precision=precision)


def _dot_nt(a, b):
    return lax.dot_general(a, b, (((1,), (1,)), ((), ())), preferred_element_type=jnp.float32)


def _dot_tn(a, b):
    return lax.dot_general(a, b, (((0,), (0,)), ((), ())), preferred_element_type=jnp.float32)


def _gdn_kernel(aqkv_ref, az_ref, small_ref, convw_ref, aneg_ref, bias_ref, gnorm_ref,
                edt_ref, ebeta_ref, bd_ref, ltri_ref, o_ref,
                ext_s, state_s, q_s, k_s, kb_s, vb_s, kbe_s, qd_s, kd_s, gb_s, oh_s):
    f32, bf16 = jnp.float32, jnp.bfloat16
    c = GDN_CHUNK
    tb = aqkv_ref.shape[0]
    gw = GDN_WIDTH
    nc = tb // c

    @pl.when(pl.program_id(1) == 0)
    def _():
        ext_s[0:CONV_PAD, :] = jnp.zeros((CONV_PAD, 3 * gw), f32)
        state_s[...] = jnp.zeros_like(state_s)

    ext_s[CONV_PAD:CONV_PAD + tb, :] = aqkv_ref[...]
    conv = convw_ref[0:1, :] * ext_s[pl.ds(CONV_PAD - CONV_K + 1, tb), :]
    for j in range(1, CONV_K):
        conv += convw_ref[j:j + 1, :] * ext_s[pl.ds(CONV_PAD - CONV_K + 1 + j, tb), :]
    ext_s[0:CONV_PAD, :] = ext_s[tb:tb + CONV_PAD, :]
    qkv = conv * jax.nn.sigmoid(conv)
    q, k, v = qkv[:, :gw], qkv[:, gw:2 * gw], qkv[:, 2 * gw:]

    bd = bd_ref[...]
    q = q * lax.rsqrt(_dot(q * q, bd, _HI) + RMS_EPS) * (HEAD_DIM ** -0.5)
    k = k * lax.rsqrt(_dot(k * k, bd, _HI) + RMS_EPS)

    sm = small_ref[...]
    pre = sm + bias_ref[...]
    softplus = jnp.maximum(pre, 0.0) + jnp.log(1.0 + jnp.exp(-jnp.abs(pre)))
    g = _dot(ltri_ref[...], aneg_ref[...] * softplus, _HI)
    gb = _dot(g, edt_ref[...], _HI)
    bb = _dot(jax.nn.sigmoid(sm), ebeta_ref[...], _HI)
    gb3 = gb.reshape(nc, c, gw)
    k_dec = (k.reshape(nc, c, gw) * jnp.exp(gb3[:, c - 1:c, :] - gb3)).reshape(tb, gw)
    eg = jnp.exp(gb)
    kb = k * bb
    q_s[...] = q.astype(bf16)
    k_s[...] = k.astype(bf16)
    kb_s[...] = kb.astype(bf16)
    vb_s[...] = (v * bb).astype(bf16)
    kbe_s[...] = (kb * eg).astype(bf16)
    qd_s[...] = (q * eg).astype(bf16)
    kd_s[...] = k_dec.astype(bf16)
    gb_s[...] = gb

    row = lax.broadcasted_iota(jnp.int32, (c, c), 0)
    col = lax.broadcasted_iota(jnp.int32, (c, c), 1)
    causal = row >= col
    strict = row > col
    diag = row == col
    eye = diag.astype(f32)
    ones = jnp.ones((c, c), f32)

    def chunk(n, carry):
        rows = pl.ds(pl.multiple_of(n * c, c), c)
        for h in range(GDN_HEADS):
            hs = slice(h * HEAD_DIM, (h + 1) * HEAD_DIM)
            qh, kh, kbh = q_s[rows, hs], k_s[rows, hs], kb_s[rows, hs]
            gc = gb_s[rows, hs]
            grow = _dot(ones, jnp.where(diag, gc, 0.0), _HI)
            dmat = jnp.where(causal, jnp.exp(jnp.minimum(gc - grow, 0.0)), 0.0)
            a = jnp.where(strict, _dot_nt(kbh, kh) * dmat, 0.0)
            attn = _dot_nt(qh, kh) * dmat
            ab = a.astype(bf16)
            x = _dot(ab, ab)
            p = eye - a
            for _ in range(4):
                xb = x.astype(bf16)
                p = p + _dot(p.astype(bf16), xb)
                x = _dot(xb, xb)
            p = p + _dot(p.astype(bf16), x.astype(bf16))
            tinv = p.astype(bf16)
            u = _dot(tinv, vb_s[rows, hs])
            w = _dot(tinv, kbe_s[rows, hs])
            s = state_s[h]
            sb = s.astype(bf16)
            vn = u - _dot(w.astype(bf16), sb)
            vnb = vn.astype(bf16)
            oh_s[rows, hs] = _dot(qd_s[rows, hs], sb) + _dot(attn.astype(bf16), vnb)
            state_s[h] = s * jnp.exp(gc[c - 1:c, :]) + _dot_tn(kd_s[rows, hs], vnb)
        return carry

    lax.fori_loop(0, nc, chunk, 0)

    o = oh_s[...]
    ms = _dot(o * o, bd, _HI) * (1.0 / HEAD_DIM)
    z = az_ref[...]
    o_ref[...] = o * lax.rsqrt(ms + RMS_EPS) * gnorm_ref[...] * (z * jax.nn.sigmoid(z))


def gdn_mixer(proj, bsz, s_len, conv_w, a_log, dt_bias, gdn_norm):
    n = bsz * s_len
    gw, nh, c = GDN_WIDTH, GDN_HEADS, GDN_CHUNK
    tb = _row_tile(s_len, GDN_BLOCK)
    nt = s_len // tb
    f32 = jnp.float32
    aneg = jnp.zeros((1, LANES), f32).at[0, :nh].set(-jnp.exp(a_log))
    bias = jnp.zeros((1, LANES), f32).at[0, :nh].set(dt_bias)
    gnorm = jnp.tile(gdn_norm, nh).reshape(1, gw)
    lane_head = np.arange(gw) // HEAD_DIM
    edt = (np.arange(LANES)[:, None] == lane_head[None, :]).astype(np.float32)
    ebeta = (np.arange(LANES)[:, None] == nh + lane_head[None, :]).astype(np.float32)
    bd = (lane_head[:, None] == lane_head[None, :]).astype(np.float32)
    ti = np.arange(tb)
    ltri = ((ti[:, None] // c == ti[None, :] // c) & (ti[None, :] <= ti[:, None])).astype(np.float32)
    rows = lambda width, cb: pl.BlockSpec((tb, width), lambda b, t: (b * nt + t, cb))
    wide = lambda dt: pltpu.VMEM((tb, gw), dt)
    return pl.pallas_call(
        _gdn_kernel,
        grid=(bsz, nt),
        in_specs=[rows(3 * gw, 0), rows(gw, 3), rows(LANES, SMALL_OFF // LANES),
                  _resident((CONV_K, 3 * gw)), _resident((1, LANES)), _resident((1, LANES)),
                  _resident((1, gw)), _resident((LANES, gw)), _resident((LANES, gw)),
                  _resident((gw, gw)), _resident((tb, tb))],
        out_specs=pl.BlockSpec((tb, gw), lambda b, t: (b * nt + t, 0)),
        out_shape=jax.ShapeDtypeStruct((n, gw), f32),
        scratch_shapes=[pltpu.VMEM((tb + CONV_PAD, 3 * gw), f32),
                        pltpu.VMEM((nh, HEAD_DIM, HEAD_DIM), f32)]
        + [wide(jnp.bfloat16)] * 7 + [wide(f32)] * 2,
        compiler_params=_params("parallel", "arbitrary"),
        name="gdn_mixer",
    )(proj, proj, proj, conv_w, aneg, bias, gnorm, jnp.asarray(edt), jnp.asarray(ebeta),
      jnp.asarray(bd), jnp.asarray(ltri))


SWA_SPAN = 128
SWA_PREP_ROWS = 512
MASKED_SCORE = -1e30


def _swa_kernel(q_ref, k_ref, v_ref, cos_ref, sin_ref, qn_ref, kn_ref, bd_ref, o_ref,
                q_s, k_s, v_s, op_s, lse_s):
    f32, bf16 = jnp.float32, jnp.bfloat16
    s_len = q_ref.shape[0]
    blk = SWA_SPAN
    lane = lax.broadcasted_iota(jnp.int32, (1, LANES), 1)
    first_half = (lane % HEAD_DIM) < (HEAD_DIM // 2)
    head0 = lane < HEAD_DIM
    bd = bd_ref[...]
    prep_rows = min(SWA_PREP_ROWS, s_len)

    def normed_rotary(x, gain, cos, sin, scale):
        ms = _dot(x * x, bd, _HI) * (1.0 / HEAD_DIM)
        y = x * lax.rsqrt(ms + RMS_EPS) * gain
        partner = jnp.where(first_half, pltpu.roll(y, LANES - HEAD_DIM // 2, 1),
                            pltpu.roll(y, HEAD_DIM // 2, 1))
        return (y * cos + partner * sin) * scale

    def prologue(i, carry):
        rows = pl.ds(pl.multiple_of(i * prep_rows, prep_rows), prep_rows)
        cos, sin = cos_ref[rows, :], sin_ref[rows, :]
        q_s[rows, :] = normed_rotary(q_ref[rows, :], qn_ref[...], cos, sin, HEAD_DIM ** -0.5)
        k_s[rows, :] = normed_rotary(k_ref[rows, :], kn_ref[...], cos, sin, 1.0)
        v_s[rows, :] = v_ref[rows, :]
        return carry

    lax.fori_loop(0, s_len // prep_rows, prologue, 0)

    qi = lax.broadcasted_iota(jnp.int32, (blk, 2 * blk), 0)
    kj = lax.broadcasted_iota(jnp.int32, (blk, 2 * blk), 1)
    dist = blk + qi - kj
    band = (dist >= 0) & (dist <= SWA_SPAN)

    for pidx, (window, dil) in enumerate(DILATED_PAIRS):
        n_blk = s_len // (dil * blk)

        def rows_of(r, i):
            start = r + i * (blk * dil)
            if dil == 1:
                return pl.ds(start, blk)
            return pl.ds(start, blk, stride=dil)

        def block(r, i):
            cur = rows_of(r, i)
            prev = rows_of(r, jnp.maximum(i - 1, 0))
            qb = q_s[cur, :]
            kw = jnp.concatenate([k_s[prev, :], k_s[cur, :]], axis=0).astype(bf16)
            vw = jnp.concatenate([v_s[prev, :], v_s[cur, :]], axis=0).astype(bf16)
            valid = band & (kj >= jnp.where(i > 0, 0, blk))
            outs, lses = [], []
            for h in range(2):
                qh = jnp.where(head0 if h == 0 else ~head0, qb, 0.0).astype(bf16)
                s = jnp.where(valid, _dot_nt(qh, kw), MASKED_SCORE)
                m = jnp.max(s, axis=-1, keepdims=True)
                p = jnp.exp(s - m)
                den = jnp.sum(p, axis=-1, keepdims=True)
                outs.append(_dot(p.astype(bf16), vw) / den)
                lses.append(m + jnp.log(den))
            op_s[pidx, cur, :] = jnp.where(head0, outs[0], outs[1])
            lse_s[pidx, cur, :] = jnp.where(head0, lses[0], lses[1])

        def residue(r, carry):
            def q_block(i, c2):
                block(r, i)
                return c2
            return lax.fori_loop(0, n_blk, q_block, carry)

        lax.fori_loop(0, dil, residue, 0)

    def epilogue(i, carry):
        rows = pl.ds(pl.multiple_of(i * prep_rows, prep_rows), prep_rows)
        l0, l1, l2 = lse_s[0, rows, :], lse_s[1, rows, :], lse_s[2, rows, :]
        m = jnp.maximum(jnp.maximum(l0, l1), l2)
        w0, w1, w2 = jnp.exp(l0 - m), jnp.exp(l1 - m), jnp.exp(l2 - m)
        mix = w0 * op_s[0, rows, :] + w1 * op_s[1, rows, :] + w2 * op_s[2, rows, :]
        o_ref[rows, :] = mix / (w0 + w1 + w2)
        return carry

    lax.fori_loop(0, s_len // prep_rows, epilogue, 0)


def swa_mixer(proj, bsz, s_len, q_norm, k_norm):
    assert all(w // d == SWA_SPAN for w, d in DILATED_PAIRS) and len(DILATED_PAIRS) == 3
    assert s_len % (SWA_SPAN * max(d for _, d in DILATED_PAIRS)) == 0
    n = bsz * s_len
    f32 = jnp.float32
    n_pairs = SWA_WIDTH // LANES
    half = HEAD_DIM // 2
    inv_freq = ROPE_THETA ** (-jnp.arange(0, HEAD_DIM, 2, dtype=f32) / HEAD_DIM)
    ang = jnp.arange(s_len, dtype=f32)[:, None] * inv_freq[None, :]
    cos = jnp.tile(jnp.cos(ang), (1, LANES // half))
    sin = jnp.tile(jnp.concatenate([-jnp.sin(ang), jnp.sin(ang)], axis=1), (1, LANES // HEAD_DIM))
    gain = lambda g: jnp.tile(g, LANES // HEAD_DIM).reshape(1, LANES)
    lane_head = np.arange(LANES) // HEAD_DIM
    bd = jnp.asarray((lane_head[:, None] == lane_head[None, :]).astype(np.float32))
    q_col0 = 4 * GDN_WIDTH // LANES
    col = lambda which: pl.BlockSpec(
        (s_len, LANES), lambda b, p: (b, q_col0 + which * n_pairs + p))
    seq = pltpu.VMEM((s_len, LANES), f32)
    per_pattern = pltpu.VMEM((len(DILATED_PAIRS), s_len, LANES), f32)
    return pl.pallas_call(
        _swa_kernel,
        grid=(bsz, n_pairs),
        in_specs=[col(0), col(1), col(2), _resident((s_len, LANES)), _resident((s_len, LANES)),
                  _resident((1, LANES)), _resident((1, LANES)), _resident((LANES, LANES))],
        out_specs=pl.BlockSpec((s_len, LANES), lambda b, p: (b, p)),
        out_shape=jax.ShapeDtypeStruct((n, SWA_WIDTH), f32),
        scratch_shapes=[seq, seq, seq, per_pattern, per_pattern],
        compiler_params=_params("parallel", "parallel"),
        name="swa_mixer",
    )(proj, proj, proj, cos, sin, gain(q_norm), gain(k_norm), bd)


def _rms_norm(x, g):
    return x * lax.rsqrt(jnp.mean(x * x, axis=-1, keepdims=True) + RMS_EPS) * g


def _l2_normalize(x):
    return x * lax.rsqrt(jnp.sum(x * x, axis=-1, keepdims=True) + RMS_EPS)


def _causal_depthwise_conv(u, w):
    k_len, chans = w.shape
    return lax.conv_general_dilated(
        u, w[:, None, :].astype(u.dtype), window_strides=(1,),
        padding=[(k_len - 1, 0)], dimension_numbers=('NWC', 'WIO', 'NWC'),
        feature_group_count=chans)


def _rotary(x):
    s_len, hd = x.shape[1], x.shape[-1]
    inv_freq = ROPE_THETA ** (-jnp.arange(0, hd, 2, dtype=jnp.float32) / hd)
    ang = jnp.arange(s_len, dtype=jnp.float32)[:, None] * inv_freq[None, :]
    cos = jnp.cos(ang)[None, :, None, :]
    sin = jnp.sin(ang)[None, :, None, :]
    x1, x2 = jnp.split(x, 2, axis=-1)
    return jnp.concatenate([x1 * cos - x2 * sin, x2 * cos + x1 * sin], axis=-1)


def _gated_delta_rule(q, k, v, log_decay, beta):
    bsz, s_len, n_h, dk = q.shape
    dv = v.shape[-1]
    c = GDN_CHUNK
    n_chunks = s_len // c

    def chunks(t):
        t = jnp.moveaxis(t, 2, 1)
        return t.reshape((bsz, n_h, n_chunks, c) + t.shape[3:])

    q, k, v, g, beta = (chunks(t) for t in (q, k, v, log_decay, beta))
    g = jnp.cumsum(g, axis=-1)
    causal = jnp.tril(jnp.ones((c, c), bool))
    strict = jnp.tril(jnp.ones((c, c), bool), -1)
    diff = g[..., :, None] - g[..., None, :]
    decay = jnp.where(causal, jnp.exp(jnp.where(causal, diff, 0.0)), 0.0)
    k_beta = k * beta[..., None]
    a = jnp.where(strict, jnp.einsum('bhnik,bhnjk->bhnij', k_beta, k) * decay, 0.0)
    system = a + jnp.eye(c, dtype=a.dtype)
    u = lax.linalg.triangular_solve(system, v * beta[..., None], left_side=True,
                                    lower=True, unit_diagonal=True)
    w = lax.linalg.triangular_solve(system, k_beta * jnp.exp(g)[..., None], left_side=True,
                                    lower=True, unit_diagonal=True)
    attn = jnp.einsum('bhnik,bhnjk->bhnij', q, k) * decay
    q_dec = q * jnp.exp(g)[..., None]
    g_last = g[..., -1]
    k_dec = k * jnp.exp(g_last[..., None] - g)[..., None]

    def step(state, inp):
        qd, kd, uc, wc, ac, gl = inp
        v_new = uc - jnp.einsum('bhck,bhkv->bhcv', wc, state)
        o = jnp.einsum('bhck,bhkv->bhcv', qd, state) + jnp.einsum('bhij,bhjv->bhiv', ac, v_new)
        state = state * jnp.exp(gl)[..., None, None] + jnp.einsum('bhck,bhcv->bhkv', kd, v_new)
        return state, o

    xs = tuple(jnp.moveaxis(t, 2, 0) for t in (q_dec, k_dec, u, w, attn, g_last))
    state0 = jnp.zeros((bsz, n_h, dk, dv), q.dtype)
    _, o = lax.scan(step, state0, xs)
    o = jnp.moveaxis(o, 0, 2).reshape(bsz, n_h, s_len, dv)
    return jnp.moveaxis(o, 1, 2)


def _dilated_window_attention(q, k, v, window, dilation):
    bsz, s_len, n_h, hd = q.shape
    span = window // dilation
    blk = span
    sub_len = s_len // dilation
    n_blk = -(-sub_len // blk)
    sub_pad = n_blk * blk

    def to_blocks(t):
        t = t.reshape(bsz, sub_len, dilation, n_h, hd).transpose(0, 2, 3, 1, 4)
        t = jnp.pad(t, ((0, 0), (0, 0), (0, 0), (0, sub_pad - sub_len), (0, 0)))
        return t.reshape(bsz, dilation, n_h, n_blk, blk, hd)

    def with_previous(t):
        prev = jnp.pad(t, ((0, 0), (0, 0), (0, 0), (1, 0), (0, 0), (0, 0)))[:, :, :, :n_blk]
        return jnp.concatenate([prev, t], axis=4)

    qb = to_blocks(q)
    kk = with_previous(to_blocks(k))
    vv = with_previous(to_blocks(v))
    s = jnp.einsum('bdhnqe,bdhnke->bdhnqk', qb, kk) * (hd ** -0.5)
    q_pos = blk + jnp.arange(blk)
    k_pos = jnp.arange(2 * blk)
    dist = q_pos[:, None] - k_pos[None, :]
    band = (dist >= 0) & (dist <= span)
    exists = (jnp.arange(n_blk)[:, None] > 0) | (k_pos[None, :] >= blk)
    valid = band[None, :, :] & exists[:, None, :]
    s = jnp.where(valid, s, -jnp.inf)
    m = jnp.max(s, axis=-1, keepdims=True)
    p = jnp.exp(s - m)
    den = jnp.sum(p, axis=-1, keepdims=True)
    o = jnp.einsum('bdhnqk,bdhnke->bdhnqe', p, vv) / den
    lse = (m + jnp.log(den))[..., 0]
    o = o.reshape(bsz, dilation, n_h, sub_pad, hd)[:, :, :, :sub_len]
    o = o.transpose(0, 3, 1, 2, 4).reshape(bsz, s_len, n_h, hd)
    lse = lse.reshape(bsz, dilation, n_h, sub_pad)[..., :sub_len]
    lse = lse.transpose(0, 3, 1, 2).reshape(bsz, s_len, n_h)
    return o, lse


def _dilated_mixture_attention(q, k, v):
    outs, lses = [], []
    for window, dilation in DILATED_PAIRS:
        o, lse = _dilated_window_attention(q, k, v, window, dilation)
        outs.append(o)
        lses.append(lse)
    weights = jax.nn.softmax(jnp.stack(lses, axis=0), axis=0)
    return jnp.einsum('pbsh,pbshe->bshe', weights, jnp.stack(outs, axis=0))


def _multiscale_pool(u, w_grp, scale):
    bsz, s_len, _ = u.shape
    uf = u.reshape(bsz, s_len, POOL_GROUPS, POOL_GROUP_DIM)
    cs = jnp.pad(jnp.cumsum(uf, axis=1), ((0, 0), (1, 0), (0, 0), (0, 0)))
    t = jnp.arange(s_len)
    pooled = []
    for gi, size in enumerate(POOL_SIZES):
        csg = cs[:, :, gi]
        lag = jnp.pad(csg, ((0, 0), (size, 0), (0, 0)))[:, 1:s_len + 1]
        count = jnp.minimum(t + 1, size).astype(jnp.float32)[None, :, None]
        pooled.append((csg[:, 1:] - lag) / count - uf[:, :, gi])
    pooled = jnp.stack(pooled, axis=2)
    y = jnp.einsum('bsgc,gcd->bsgd', pooled, w_grp)
    return y.reshape(bsz, s_len, POOL_WIDTH) * scale


def kernel(x, norm_mix, w_in, conv_w, a_log, dt_bias, gdn_norm, q_norm, k_norm, pool_w,
           pool_scale, w_out, norm_ffn, ffn_gate, ffn_up, ffn_down, router_w, router_b,
           exp_gate, exp_up, exp_down):
    bsz, s_len, d = x.shape
    n = bsz * s_len
    depth = w_in.shape[0]
    x2 = x.reshape(n, d)

    for layer in range(depth):
        proj = norm_inproj(x2, norm_mix[layer], _arrange_w_in(w_in[layer]))
        ya = gdn_mixer(proj, bsz, s_len, conv_w[layer], a_log[layer], dt_bias[layer],
                       gdn_norm[layer])
        yb = swa_mixer(proj, bsz, s_len, q_norm[layer], k_norm[layer])
        x2 = pool_outproj_residual(x2, ya, yb, proj, bsz, s_len, pool_w[layer],
                                   pool_scale[layer], w_out[layer])

        i = layer // 2
        if layer % 2 == 0:
            x2 = ffn_dense(x2, norm_ffn[layer], ffn_gate[i], ffn_up[i], ffn_down[i])
        else:
            x2 = moe_block(x2, norm_ffn[layer], router_w[i], router_b[i], exp_gate[i],
                           exp_up[i], exp_down[i])
    return x2.reshape(bsz, s_len, d)
```

```python
import functools

import jax
import jax.numpy as jnp
import numpy as np
from jax import lax
from jax.experimental import pallas as pl
from jax.experimental.pallas import tpu as pltpu

D_MODEL = 1024
HEAD_DIM = 64
MIX_WIDTH = D_MODEL
POOL_WIDTH = MIX_WIDTH // 4
POOL_GROUPS = 4
POOL_GROUP_DIM = POOL_WIDTH // POOL_GROUPS
POOL_SIZES = (2, 4, 8, 16)
GDN_HEADS = (MIX_WIDTH - POOL_WIDTH) // (2 * HEAD_DIM)
GDN_WIDTH = GDN_HEADS * HEAD_DIM
SWA_HEADS = (MIX_WIDTH - POOL_WIDTH - GDN_WIDTH) // HEAD_DIM
SWA_WIDTH = SWA_HEADS * HEAD_DIM
CONV_K = 4
GDN_CHUNK = 64
DILATED_PAIRS = ((128, 1), (512, 4), (2048, 16))
ROPE_THETA = 10000.0
RMS_EPS = 1e-6
N_EXPERTS = 8
TOP_K = 2

VMEM_LIMIT_BYTES = 56 * 1024 * 1024
LANES = 128

SMALL_OFF = 7 * GDN_WIDTH
POOL_OFF = SMALL_OFF + LANES
PROJ_WIDTH = POOL_OFF + POOL_WIDTH

_HI = lax.Precision.HIGHEST


def _row_tile(n_rows, want):
    t = min(want, n_rows)
    while n_rows % t:
        t //= 2
    return t


def _params(*sem):
    return pltpu.CompilerParams(dimension_semantics=sem, vmem_limit_bytes=VMEM_LIMIT_BYTES)


def _resident(shape):
    zeros = (0,) * len(shape)
    return pl.BlockSpec(shape, lambda *_: zeros, pipeline_mode=pl.Buffered(1))


def _dot(a, b, precision=None):
    return jnp.dot(a, b, preferred_element_type=jnp.float32, precision=precision)


def _dot_nt(a, b):
    return lax.dot_general(a, b, (((1,), (1,)), ((), ())), preferred_element_type=jnp.float32)


def _split_dot(x, sel, terms, sel_on_left=False):
    acc, rest = None, x
    for t in range(terms):
        piece = rest.astype(jnp.bfloat16)
        part = _dot(sel, piece) if sel_on_left else _dot(piece, sel)
        acc = part if acc is None else acc + part
        if t + 1 < terms:
            rest = rest - piece.astype(jnp.float32)
    return acc


def _norm_inproj_kernel(x_ref, g_ref, w_ref, o_ref):
    x = x_ref[...]
    ms = jnp.mean(x * x, axis=-1, keepdims=True)
    h = (x * lax.rsqrt(ms + RMS_EPS) * g_ref[...]).astype(jnp.bfloat16)
    o_ref[...] = _dot(h, w_ref[...])


def norm_inproj(x2, g, w):
    n, d = x2.shape
    width = w.shape[1]
    tm = _row_tile(n, 512)
    return pl.pallas_call(
        _norm_inproj_kernel,
        grid=(n // tm,),
        in_specs=[pl.BlockSpec((tm, d), lambda i: (i, 0)),
                  _resident((1, d)),
                  _resident((d, width))],
        out_specs=pl.BlockSpec((tm, width), lambda i: (i, 0)),
        out_shape=jax.ShapeDtypeStruct((n, width), jnp.float32),
        compiler_params=_params("parallel"),
        name="norm_inproj",
    )(x2, g.reshape(1, d), w)


def _arrange_w_in(w):
    gw, gh, sw = GDN_WIDTH, GDN_HEADS, SWA_WIDTH
    a_main = w[:, :4 * gw]
    small = w[:, 4 * gw:4 * gw + 2 * gh]
    b_main = w[:, 4 * gw + 2 * gh:4 * gw + 2 * gh + 3 * sw]
    cu = w[:, 4 * gw + 2 * gh + 3 * sw:]
    small = jnp.pad(small, ((0, 0), (0, LANES - 2 * gh)))
    return jnp.concatenate([a_main, b_main, small, cu], axis=1).astype(jnp.bfloat16)


POOL_HIST = max(POOL_SIZES)


def _pool_outproj_kernel(x_ref, ya_ref, yb_ref, cu_ref, size_ref, pw_ref, ps_ref,
                         wa_ref, wb_ref, wc_ref, o_ref, ext_s):
    f32, bf16 = jnp.float32, jnp.bfloat16
    tm = x_ref.shape[0]
    t_blk = pl.program_id(1)

    @pl.when(t_blk == 0)
    def _():
        ext_s[0:POOL_HIST, :] = jnp.zeros((POOL_HIST, POOL_WIDTH), f32)

    cu = cu_ref[...]
    ext_s[POOL_HIST:POOL_HIST + tm, :] = cu
    ext = ext_s[...]
    ext_s[0:POOL_HIST, :] = ext_s[tm:tm + POOL_HIST, :]
    size = size_ref[...]
    win = ext
    total = jnp.zeros_like(ext)
    shift = 1
    while shift < POOL_HIST:
        win = win + pltpu.roll(win, shift, 0)
        shift *= 2
        total = jnp.where(size == shift, win, total)
    total = total[POOL_HIST:, :]
    pos = t_blk * tm + lax.broadcasted_iota(jnp.int32, (tm, 1), 0)
    count = jnp.minimum(pos + 1, size).astype(f32)
    pooled = total / count - cu
    yc = _dot(pooled.astype(bf16), pw_ref[...]) * ps_ref[...]

    acc = x_ref[...]
    acc += _dot(ya_ref[...].astype(bf16), wa_ref[...])
    acc += _dot(yb_ref[...].astype(bf16), wb_ref[...])
    acc += _dot(yc.astype(bf16), wc_ref[...])
    o_ref[...] = acc


def pool_outproj_residual(x2, ya, yb, proj, bsz, s_len, pool_w, pool_scale, w_out):
    n, d = x2.shape
    tm = _row_tile(s_len, 1024)
    nt = s_len // tm
    bf16 = jnp.bfloat16
    wa = w_out[:GDN_WIDTH].astype(bf16)
    wb = w_out[GDN_WIDTH:GDN_WIDTH + SWA_WIDTH].astype(bf16)
    wc = w_out[GDN_WIDTH + SWA_WIDTH:].astype(bf16)
    size = jnp.asarray(np.repeat(np.asarray(POOL_SIZES, np.int32), POOL_GROUP_DIM).reshape(1, POOL_WIDTH))
    pw = jax.scipy.linalg.block_diag(*[pool_w[g] for g in range(POOL_GROUPS)]).astype(bf16)
    row = lambda width, cb=0: pl.BlockSpec((tm, width), lambda b, t: (b * nt + t, cb))
    return pl.pallas_call(
        _pool_outproj_kernel,
        grid=(bsz, nt),
        in_specs=[row(d), row(GDN_WIDTH), row(SWA_WIDTH), row(POOL_WIDTH, POOL_OFF // POOL_WIDTH),
                  _resident((1, POOL_WIDTH)), _resident(pw.shape), _resident((1, POOL_WIDTH)),
                  _resident(wa.shape), _resident(wb.shape), _resident(wc.shape)],
        out_specs=row(d),
        out_shape=jax.ShapeDtypeStruct((n, d), jnp.float32),
        scratch_shapes=[pltpu.VMEM((tm + POOL_HIST, POOL_WIDTH), jnp.float32)],
        compiler_params=_params("parallel", "arbitrary"),
        name="pool_outproj_residual",
    )(x2, ya, yb, proj, size, pw, pool_scale.reshape(1, POOL_WIDTH), wa, wb, wc)


def _ffn_kernel(n_chunks, x_ref, g_ref, wg_ref, wu_ref, wd_ref, o_ref):
    x = x_ref[...]
    ms = jnp.mean(x * x, axis=-1, keepdims=True)
    h = (x * lax.rsqrt(ms + RMS_EPS) * g_ref[...]).astype(jnp.bfloat16)
    tc = wg_ref.shape[1] // n_chunks
    acc = x
    for c in range(n_chunks):
        gate = _dot(h, wg_ref[:, c * tc:(c + 1) * tc])
        up = _dot(h, wu_ref[:, c * tc:(c + 1) * tc])
        act = (gate * jax.nn.sigmoid(gate) * up).astype(jnp.bfloat16)
        acc = acc + _dot(act, wd_ref[c * tc:(c + 1) * tc, :])
    o_ref[...] = acc


def ffn_dense(x2, g, w_gate, w_up, w_down):
    n, d = x2.shape
    d_ff = w_gate.shape[1]
    tm = _row_tile(n, 512)
    n_chunks = 2 if d_ff % (2 * LANES) == 0 else 1
    row = pl.BlockSpec((tm, d), lambda i: (i, 0))
    return pl.pallas_call(
        functools.partial(_ffn_kernel, n_chunks),
        grid=(n // tm,),
        in_specs=[row, _resident((1, d)), _resident((d, d_ff)), _resident((d, d_ff)),
                  _resident((d_ff, d))],
        out_specs=row,
        out_shape=jax.ShapeDtypeStruct((n, d), jnp.float32),
        compiler_params=_params("parallel"),
        name="ffn_dense",
    )(x2, g.reshape(1, d), w_gate.astype(jnp.bfloat16), w_up.astype(jnp.bfloat16),
      w_down.astype(jnp.bfloat16))


def _router_kernel(x_ref, g_ref, wr_ref, br_ref, h_ref, logit_ref):
    x = x_ref[...]
    ms = jnp.mean(x * x, axis=-1, keepdims=True)
    h = x * lax.rsqrt(ms + RMS_EPS) * g_ref[...]
    h_ref[...] = h.astype(jnp.bfloat16)
    logit_ref[...] = _dot(h, wr_ref[...], _HI) + br_ref[...]


def norm_router(x2, g, router_w, router_b):
    n, d = x2.shape
    tm = _row_tile(n, 1024)
    wr = jnp.pad(router_w, ((0, 0), (0, LANES - N_EXPERTS)))
    br = jnp.pad(router_b, (0, LANES - N_EXPERTS)).reshape(1, LANES)
    return pl.pallas_call(
        _router_kernel,
        grid=(n // tm,),
        in_specs=[pl.BlockSpec((tm, d), lambda i: (i, 0)), _resident((1, d)),
                  _resident((d, LANES)), _resident((1, LANES))],
        out_specs=[pl.BlockSpec((tm, d), lambda i: (i, 0)),
                   pl.BlockSpec((tm, LANES), lambda i: (i, 0))],
        out_shape=[jax.ShapeDtypeStruct((n, d), jnp.bfloat16),
                   jax.ShapeDtypeStruct((n, LANES), jnp.float32)],
        compiler_params=_params("parallel"),
        name="norm_router",
    )(x2, g.reshape(1, d), wr, br)


def _moe_kernel(be_ref, nb_ref, xs_ref, wg_ref, wu_ref, wd_ref, o_ref):
    j = pl.program_id(0)
    c = pl.program_id(1)

    @pl.when(j < nb_ref[0])
    def _():
        h = xs_ref[...]
        gate = _dot(h, wg_ref[0])
        up = _dot(h, wu_ref[0])
        act = (gate * jax.nn.sigmoid(gate) * up).astype(jnp.bfloat16)
        part = _dot(act, wd_ref[0])

        @pl.when(c == 0)
        def _():
            o_ref[...] = part

        @pl.when(c > 0)
        def _():
            o_ref[...] += part

    @pl.when(jnp.logical_and(j >= nb_ref[0], c == 0))
    def _():
        o_ref[...] = jnp.zeros_like(o_ref)


def moe_experts(xs, block_expert, n_used, w_gate, w_up, w_down, tm, tc):
    n_slots, d = xs.shape
    d_e = w_gate.shape[2]
    n_blocks = n_slots // tm
    n_c = d_e // tc

    def live(j, nb):
        return jnp.minimum(j, nb[0] - 1)

    def chunk(j, c, nb):
        return jnp.where(j < nb[0], c, n_c - 1)

    return pl.pallas_call(
        _moe_kernel,
        grid_spec=pltpu.PrefetchScalarGridSpec(
            num_scalar_prefetch=2,
            grid=(n_blocks, n_c),
            in_specs=[
                pl.BlockSpec((tm, d), lambda j, c, be, nb: (live(j, nb), 0)),
                pl.BlockSpec((1, d, tc), lambda j, c, be, nb: (be[live(j, nb)], 0, chunk(j, c, nb))),
                pl.BlockSpec((1, d, tc), lambda j, c, be, nb: (be[live(j, nb)], 0, chunk(j, c, nb))),
                pl.BlockSpec((1, tc, d), lambda j, c, be, nb: (be[live(j, nb)], chunk(j, c, nb), 0)),
            ],
            out_specs=pl.BlockSpec((tm, d), lambda j, c, be, nb: (j, 0)),
        ),
        out_shape=jax.ShapeDtypeStruct((n_slots, d), jnp.float32),
        compiler_params=_params("arbitrary", "arbitrary"),
        name="moe_experts",
    )(block_expert, n_used, xs, w_gate, w_up, w_down)


def moe_block(x2, g, router_w, router_b, w_gate, w_up, w_down):
    n, d = x2.shape
    tm = _row_tile(n, 1024)
    h, logits = norm_router(x2, g, router_w, router_b)
    logits = logits[:, :N_EXPERTS]
    top_logits, top_expert = lax.top_k(logits, TOP_K)
    gates = jax.nn.softmax(top_logits, axis=-1)
    flat_expert = top_expert.reshape(-1)
    n_assign = n * TOP_K
    onehot = (flat_expert[:, None] == jnp.arange(N_EXPERTS)[None, :]).astype(jnp.int32)
    rank = jnp.cumsum(onehot, axis=0) - onehot
    counts = jnp.sum(onehot, axis=0)
    padded = (counts + tm - 1) // tm * tm
    padded_end = jnp.cumsum(padded)
    padded_start = padded_end - padded
    slot = jnp.sum(onehot * (padded_start[None, :] + rank), axis=1)
    n_blocks = -(-n_assign // tm) + N_EXPERTS
    n_slots = n_blocks * tm
    flat_token = jnp.arange(n_assign, dtype=jnp.int32) // TOP_K
    slot_token = jnp.zeros((n_slots,), jnp.int32).at[slot].set(flat_token)
    block_start = jnp.arange(n_blocks) * tm
    block_expert = jnp.minimum(
        jnp.sum(block_start[:, None] >= padded_end[None, :], axis=1), N_EXPERTS - 1).astype(jnp.int32)
    n_used = (padded_end[-1] // tm).astype(jnp.int32).reshape(1)
    xs = jnp.take(h, slot_token, axis=0)
    d_e = w_gate.shape[2]
    tc = d_e // 4 if d_e % (4 * LANES) == 0 else d_e
    y_slots = moe_experts(xs, block_expert, n_used, w_gate.astype(jnp.bfloat16),
                          w_up.astype(jnp.bfloat16), w_down.astype(jnp.bfloat16), tm, tc)
    y = jnp.take(y_slots, slot, axis=0).reshape(n, TOP_K, d) * gates[:, :, None]
    return x2 + jnp.sum(y, axis=1)


GDN_BLOCK = 512
GDN_GROUP = 2
CONV_PAD = 8
GDN_PAIRS = GDN_HEADS // 2
PAIR_ROWS = 2 * GDN_CHUNK


def _gdn_kernel(aqkv_ref, az_ref, small_ref, convw_ref, aneg_ref, bias_ref, gnorm_ref,
                eeven_ref, eodd_ref, ebeta_ref, bd_ref, ltri_ref, o_ref,
                ext_s, state_s, q_s, k_s, kb_s, qd_s, rhs_s, attn_s, w_s, kdt_s,
                gc_s, kd_s, u_s, oh_s):
    f32, bf16 = jnp.float32, jnp.bfloat16
    c = GDN_CHUNK
    tb = aqkv_ref.shape[0]
    gw = GDN_WIDTH
    nc = tb // c

    @pl.when(pl.program_id(1) == 0)
    def _():
        ext_s[0:CONV_PAD, :] = jnp.zeros((CONV_PAD, 3 * gw), f32)
        state_s[...] = jnp.zeros_like(state_s)

    ext_s[CONV_PAD:CONV_PAD + tb, :] = aqkv_ref[...]
    conv = convw_ref[0:1, :] * ext_s[pl.ds(CONV_PAD - CONV_K + 1, tb), :]
    for j in range(1, CONV_K):
        conv += convw_ref[j:j + 1, :] * ext_s[pl.ds(CONV_PAD - CONV_K + 1 + j, tb), :]
    ext_s[0:CONV_PAD, :] = ext_s[tb:tb + CONV_PAD, :]
    qkv = conv * jax.nn.sigmoid(conv)
    q, k, v = qkv[:, :gw], qkv[:, gw:2 * gw], qkv[:, 2 * gw:]

    bd = bd_ref[...]
    q = q * lax.rsqrt(_split_dot(q * q, bd, 2) + RMS_EPS) * (HEAD_DIM ** -0.5)
    k = k * lax.rsqrt(_split_dot(k * k, bd, 2) + RMS_EPS)

    sm = small_ref[...]
    pre = sm + bias_ref[...]
    softplus = jnp.maximum(pre, 0.0) + jnp.log(1.0 + jnp.exp(-jnp.abs(pre)))
    g = _split_dot(aneg_ref[...] * softplus, ltri_ref[...], 3, sel_on_left=True)
    g_even = _split_dot(g, eeven_ref[...], 3)
    g_odd = _split_dot(g, eodd_ref[...], 3)
    bb = _split_dot(jax.nn.sigmoid(sm), ebeta_ref[...], 3)
    lane = lax.broadcasted_iota(jnp.int32, (1, gw), 1)
    even_lane = (lane // HEAD_DIM) % 2 == 0
    gb = jnp.where(even_lane, g_even, g_odd)
    gb3 = gb.reshape(nc, c, gw)
    k_dec = (k.reshape(nc, c, gw) * jnp.exp(gb3[:, c - 1:c, :] - gb3)).reshape(tb, gw)
    eg = jnp.exp(gb)
    kb = k * bb

    def stack(x):
        x3 = x.reshape(nc, c, gw)
        zero = jnp.zeros_like(x3)
        return jnp.concatenate([jnp.where(even_lane, x3, zero), jnp.where(even_lane, zero, x3)], axis=1)

    q_s[...] = stack(q.astype(bf16))
    k_s[...] = stack(k.astype(bf16))
    kb_s[...] = stack(kb.astype(bf16))
    qd_s[...] = stack((q * eg).astype(bf16))
    kd_s[...] = stack(k_dec)
    vb_st = stack((v * bb).astype(bf16))
    kbe_st = stack((kb * eg).astype(bf16))
    for p in range(GDN_PAIRS):
        ps = slice(p * LANES, (p + 1) * LANES)
        rhs_s[:, :, 2 * p * LANES:(2 * p + 1) * LANES] = vb_st[:, :, ps]
        rhs_s[:, :, (2 * p + 1) * LANES:(2 * p + 2) * LANES] = kbe_st[:, :, ps]
    gc_s[...] = jnp.concatenate([g_even.reshape(nc, c, gw), g_odd.reshape(nc, c, gw)], axis=1)

    row = lax.broadcasted_iota(jnp.int32, (PAIR_ROWS, PAIR_ROWS), 0)
    col = lax.broadcasted_iota(jnp.int32, (PAIR_ROWS, PAIR_ROWS), 1)
    causal = (row // c == col // c) & (row >= col)
    strict = causal & (row > col)
    eye = (row == col).astype(f32)

    def phase_a(gi, carry):
        units = [(gi * GDN_GROUP + ci, p) for ci in range(GDN_GROUP) for p in range(GDN_PAIRS)]
        at = lambda ref, n, p: ref[n, :, p * LANES:(p + 1) * LANES]
        k_u = [at(k_s, n, p) for n, p in units]
        kk = [_dot_nt(at(kb_s, n, p), kx) for (n, p), kx in zip(units, k_u)]
        qk = [_dot_nt(at(q_s, n, p), kx) for (n, p), kx in zip(units, k_u)]
        gc = [at(gc_s, n, p) for n, p in units]
        dm = [jnp.where(causal, jnp.exp(jnp.minimum(gx - gx.T, 0.0)), 0.0) for gx in gc]
        a = [jnp.where(strict, x * d, 0.0) for x, d in zip(kk, dm)]
        for (n, p), x, d in zip(units, qk, dm):
            attn_s[n, :, p * LANES:(p + 1) * LANES] = (x * d).astype(bf16)
        for (n, p) in units:
            kdt_s[n, :, p * LANES:(p + 1) * LANES] = at(kd_s, n, p).T.astype(bf16)
        ab = [x.astype(bf16) for x in a]
        sq = [_dot(x, x) for x in ab]
        inv = [eye - x for x in a]
        for _ in range(4):
            sqb = [x.astype(bf16) for x in sq]
            inv = [t + _dot(t.astype(bf16), x) for t, x in zip(inv, sqb)]
            sq = [_dot(x, x) for x in sqb]
        inv = [t + _dot(t.astype(bf16), x.astype(bf16)) for t, x in zip(inv, sq)]
        for (n, p), t in zip(units, inv):
            uw = _dot(t.astype(bf16), rhs_s[n, :, 2 * p * LANES:(2 * p + 2) * LANES])
            u_s[n, :, p * LANES:(p + 1) * LANES] = uw[:, :LANES]
            w_s[n, :, p * LANES:(p + 1) * LANES] = uw[:, LANES:].astype(bf16)
        return carry

    lax.fori_loop(0, nc // GDN_GROUP, phase_a, 0)

    def phase_b(n, carry):
        pairs = range(GDN_PAIRS)
        at = lambda ref, p: ref[n, :, p * LANES:(p + 1) * LANES]
        s = [state_s[p] for p in pairs]
        sb = [x.astype(bf16) for x in s]
        both = [_dot(jnp.concatenate([at(w_s, p), at(qd_s, p)], axis=0), sb[p]) for p in pairs]
        vn = [(at(u_s, p) - both[p][:PAIR_ROWS]).astype(bf16) for p in pairs]
        o_st = [both[p][PAIR_ROWS:] + _dot(at(attn_s, p), vn[p]) for p in pairs]
        upd = [_dot(at(kdt_s, p), vn[p]) for p in pairs]
        for p in pairs:
            ps = slice(p * LANES, (p + 1) * LANES)
            g_last = jnp.where(even_lane[:, :LANES], gc_s[n, c - 1:c, ps], gc_s[n, 2 * c - 1:2 * c, ps])
            state_s[p] = s[p] * jnp.exp(g_last) + upd[p]
            oh_s[pl.ds(pl.multiple_of(n * c, c), c), p * LANES:(p + 1) * LANES] = (
                o_st[p][:c] + o_st[p][c:])
        return carry

    lax.fori_loop(0, nc, phase_b, 0)

    o = oh_s[...]
    ms = _split_dot(o * o, bd, 2) * (1.0 / HEAD_DIM)
    z = az_ref[...]
    o_ref[...] = o * lax.rsqrt(ms + RMS_EPS) * gnorm_ref[...] * (z * jax.nn.sigmoid(z))


def gdn_mixer(proj, bsz, s_len, conv_w, a_log, dt_bias, gdn_norm):
    n = bsz * s_len
    gw, nh, c = GDN_WIDTH, GDN_HEADS, GDN_CHUNK
    tb = _row_tile(s_len, GDN_BLOCK)
    nt = s_len // tb
    nc = tb // c
    assert nc % GDN_GROUP == 0 and nh % 2 == 0
    f32, bf16 = jnp.float32, jnp.bfloat16
    aneg = jnp.zeros((1, LANES), f32).at[0, :nh].set(-jnp.exp(a_log))
    bias = jnp.zeros((1, LANES), f32).at[0, :nh].set(dt_bias)
    gnorm = jnp.tile(gdn_norm, nh).reshape(1, gw)
    lane_head = np.arange(gw) // HEAD_DIM
    src = np.arange(LANES)[:, None]
    eeven = jnp.asarray(src == (lane_head // 2 * 2)[None, :], bf16)
    eodd = jnp.asarray(src == (lane_head // 2 * 2 + 1)[None, :], bf16)
    ebeta = jnp.asarray(src == nh + lane_head[None, :], bf16)
    bd = jnp.asarray(lane_head[:, None] == lane_head[None, :], bf16)
    ti = np.arange(tb)
    ltri = jnp.asarray((ti[:, None] // c == ti[None, :] // c) & (ti[None, :] <= ti[:, None]), bf16)
    rows = lambda width, cb: pl.BlockSpec((tb, width), lambda b, t: (b * nt + t, cb))
    stacked = lambda dt, width=gw: pltpu.VMEM((nc, PAIR_ROWS, width), dt)
    return pl.pallas_call(
        _gdn_kernel,
        grid=(bsz, nt),
        in_specs=[rows(3 * gw, 0), rows(gw, 3), rows(LANES, SMALL_OFF // LANES),
                  _resident((CONV_K, 3 * gw)), _resident((1, LANES)), _resident((1, LANES)),
                  _resident((1, gw)), _resident((LANES, gw)), _resident((LANES, gw)),
                  _resident((LANES, gw)), _resident((gw, gw)), _resident((tb, tb))],
        out_specs=pl.BlockSpec((tb, gw), lambda b, t: (b * nt + t, 0)),
        out_shape=jax.ShapeDtypeStruct((n, gw), f32),
        scratch_shapes=[
            pltpu.VMEM((tb + CONV_PAD, 3 * gw), f32),
            pltpu.VMEM((GDN_PAIRS, LANES, LANES), f32),
            stacked(bf16), stacked(bf16), stacked(bf16), stacked(bf16),
            stacked(bf16, 2 * gw),
            stacked(bf16), stacked(bf16), stacked(bf16),
            stacked(f32), stacked(f32), stacked(f32),
            pltpu.VMEM((tb, gw), f32)],
        compiler_params=_params("parallel", "arbitrary"),
        name="gdn_mixer",
    )(proj, proj, proj, conv_w, aneg, bias, gnorm, eeven, eodd, ebeta, bd, ltri)


SWA_SPAN = 128
SWA_PREP_ROWS = 512
SWA_GROUP = 4
MASKED_SCORE = -1e30


def _swa_kernel(q_ref, k_ref, v_ref, cos_ref, sin_ref, qn_ref, kn_ref, bd_ref, o_ref,
                q_s, k_s, v_s, op_s, lse_s):
    f32, bf16 = jnp.float32, jnp.bfloat16
    s_len = q_ref.shape[0]
    blk = SWA_SPAN
    lane = lax.broadcasted_iota(jnp.int32, (1, LANES), 1)
    first_half = (lane % HEAD_DIM) < (HEAD_DIM // 2)
    head0 = lane < HEAD_DIM
    bd = bd_ref[...]
    prep_rows = min(SWA_PREP_ROWS, s_len)

    def normed_rotary(x, gain, cos, sin, scale):
        ms = _split_dot(x * x, bd, 2) * (1.0 / HEAD_DIM)
        y = x * lax.rsqrt(ms + RMS_EPS) * gain
        partner = jnp.where(first_half, pltpu.roll(y, LANES - HEAD_DIM // 2, 1),
                            pltpu.roll(y, HEAD_DIM // 2, 1))
        return (y * cos + partner * sin) * scale

    def prologue(i, carry):
        rows = pl.ds(pl.multiple_of(i * prep_rows, prep_rows), prep_rows)
        cos, sin = cos_ref[rows, :], sin_ref[rows, :]
        q_s[rows, :] = normed_rotary(q_ref[rows, :], qn_ref[...], cos, sin, HEAD_DIM ** -0.5)
        k_s[rows, :] = normed_rotary(k_ref[rows, :], kn_ref[...], cos, sin, 1.0)
        v_s[rows, :] = v_ref[rows, :]
        return carry

    lax.fori_loop(0, s_len // prep_rows, prologue, 0)

    qi = lax.broadcasted_iota(jnp.int32, (blk, 2 * blk), 0)
    kj = lax.broadcasted_iota(jnp.int32, (blk, 2 * blk), 1)
    dist = blk + qi - kj
    band = (dist >= 0) & (dist <= SWA_SPAN)

    for pidx, (window, dil) in enumerate(DILATED_PAIRS):
        n_blk = s_len // (dil * blk)

        def rows_of(r, i):
            start = r + i * (blk * dil)
            if dil == 1:
                return pl.ds(start, blk)
            return pl.ds(start, blk, stride=dil)

        def block_group(gi, carry):
            units = [gi * SWA_GROUP + j for j in range(SWA_GROUP)]
            ri = [(u // n_blk, u % n_blk) for u in units]
            cur = [rows_of(r, i) for r, i in ri]
            prev = [rows_of(r, jnp.maximum(i - 1, 0)) for r, i in ri]
            qb = [q_s[c, :] for c in cur]
            kw = [jnp.concatenate([k_s[pv, :], k_s[c, :]], axis=0).astype(bf16)
                  for pv, c in zip(prev, cur)]
            vw = [jnp.concatenate([v_s[pv, :], v_s[c, :]], axis=0).astype(bf16)
                  for pv, c in zip(prev, cur)]
            valid = [band & (kj >= jnp.where(i > 0, 0, blk)) for _, i in ri]
            chains = [(j, h) for j in range(SWA_GROUP) for h in range(2)]
            qh = [jnp.where(head0 if h == 0 else ~head0, qb[j], 0.0).astype(bf16) for j, h in chains]
            s = [jnp.where(valid[j], _dot_nt(x, kw[j]), MASKED_SCORE) for (j, h), x in zip(chains, qh)]
            m = [jnp.max(x, axis=-1, keepdims=True) for x in s]
            p = [jnp.exp(x - mx) for x, mx in zip(s, m)]
            den = [jnp.sum(x, axis=-1, keepdims=True) for x in p]
            o = [_dot(x.astype(bf16), vw[j]) / dx for (j, h), x, dx in zip(chains, p, den)]
            lse = [mx + jnp.log(dx) for mx, dx in zip(m, den)]
            for j in range(SWA_GROUP):
                op_s[pidx, cur[j], :] = jnp.where(head0, o[2 * j], o[2 * j + 1])
                lse_s[pidx, cur[j], :] = jnp.where(head0, lse[2 * j], lse[2 * j + 1])
            return carry

        lax.fori_loop(0, s_len // (blk * SWA_GROUP), block_group, 0)

    def epilogue(i, carry):
        rows = pl.ds(pl.multiple_of(i * prep_rows, prep_rows), prep_rows)
        l0, l1, l2 = lse_s[0, rows, :], lse_s[1, rows, :], lse_s[2, rows, :]
        m = jnp.maximum(jnp.maximum(l0, l1), l2)
        w0, w1, w2 = jnp.exp(l0 - m), jnp.exp(l1 - m), jnp.exp(l2 - m)
        mix = w0 * op_s[0, rows, :] + w1 * op_s[1, rows, :] + w2 * op_s[2, rows, :]
        o_ref[rows, :] = mix / (w0 + w1 + w2)
        return carry

    lax.fori_loop(0, s_len // prep_rows, epilogue, 0)


def swa_mixer(proj, bsz, s_len, q_norm, k_norm):
    assert all(w // d == SWA_SPAN for w, d in DILATED_PAIRS) and len(DILATED_PAIRS) == 3
    assert s_len % (SWA_SPAN * max(d for _, d in DILATED_PAIRS)) == 0
    n = bsz * s_len
    f32 = jnp.float32
    n_pairs = SWA_WIDTH // LANES
    half = HEAD_DIM // 2
    inv_freq = ROPE_THETA ** (-jnp.arange(0, HEAD_DIM, 2, dtype=f32) / HEAD_DIM)
    ang = jnp.arange(s_len, dtype=f32)[:, None] * inv_freq[None, :]
    cos = jnp.tile(jnp.cos(ang), (1, LANES // half))
    sin = jnp.tile(jnp.concatenate([-jnp.sin(ang), jnp.sin(ang)], axis=1), (1, LANES // HEAD_DIM))
    gain = lambda g: jnp.tile(g, LANES // HEAD_DIM).reshape(1, LANES)
    lane_head = np.arange(LANES) // HEAD_DIM
    bd = jnp.asarray(lane_head[:, None] == lane_head[None, :], jnp.bfloat16)
    q_col0 = 4 * GDN_WIDTH // LANES
    col = lambda which: pl.BlockSpec(
        (s_len, LANES), lambda b, p: (b, q_col0 + which * n_pairs + p))
    seq = pltpu.VMEM((s_len, LANES), f32)
    per_pattern = pltpu.VMEM((len(DILATED_PAIRS), s_len, LANES), f32)
    return pl.pallas_call(
        _swa_kernel,
        grid=(bsz, n_pairs),
        in_specs=[col(0), col(1), col(2), _resident((s_len, LANES)), _resident((s_len, LANES)),
                  _resident((1, LANES)), _resident((1, LANES)), _resident((LANES, LANES))],
        out_specs=pl.BlockSpec((s_len, LANES), lambda b, p: (b, p)),
        out_shape=jax.ShapeDtypeStruct((n, SWA_WIDTH), f32),
        scratch_shapes=[seq, seq, seq, per_pattern, per_pattern],
        compiler_params=_params("parallel", "parallel"),
        name="swa_mixer",
    )(proj, proj, proj, cos, sin, gain(q_norm), gain(k_norm), bd)


def kernel(x, norm_mix, w_in, conv_w, a_log, dt_bias, gdn_norm, q_norm, k_norm, pool_w,
           pool_scale, w_out, norm_ffn, ffn_gate, ffn_up, ffn_down, router_w, router_b,
           exp_gate, exp_up, exp_down):
    bsz, s_len, d = x.shape
    n = bsz * s_len
    depth = w_in.shape[0]
    x2 = x.reshape(n, d)
    for layer in range(depth):
        proj = norm_inproj(x2, norm_mix[layer], _arrange_w_in(w_in[layer]))
        ya = gdn_mixer(proj, bsz, s_len, conv_w[layer], a_log[layer], dt_bias[layer],
                       gdn_norm[layer])
        yb = swa_mixer(proj, bsz, s_len, q_norm[layer], k_norm[layer])
        x2 = pool_outproj_residual(x2, ya, yb, proj, bsz, s_len, pool_w[layer],
                                   pool_scale[layer], w_out[layer])
        i = layer // 2
        if layer % 2 == 0:
            x2 = ffn_dense(x2, norm_ffn[layer], ffn_gate[i], ffn_up[i], ffn_down[i])
        else:
            x2 = moe_block(x2, norm_ffn[layer], router_w[i], router_b[i], exp_gate[i],
                           exp_up[i], exp_down[i])
    return x2.reshape(bsz, s_len, d)
```

```python
import functools

import jax
import jax.numpy as jnp
import numpy as np
from jax import lax
from jax.experimental import pallas as pl
from jax.experimental.pallas import tpu as pltpu

D_MODEL = 1024
HEAD_DIM = 64
MIX_WIDTH = D_MODEL
POOL_WIDTH = MIX_WIDTH // 4
POOL_GROUPS = 4
POOL_GROUP_DIM = POOL_WIDTH // POOL_GROUPS
POOL_SIZES = (2, 4, 8, 16)
GDN_HEADS = (MIX_WIDTH - POOL_WIDTH) // (2 * HEAD_DIM)
GDN_WIDTH = GDN_HEADS * HEAD_DIM
SWA_HEADS = (MIX_WIDTH - POOL_WIDTH - GDN_WIDTH) // HEAD_DIM
SWA_WIDTH = SWA_HEADS * HEAD_DIM
CONV_K = 4
GDN_CHUNK = 64
DILATED_PAIRS = ((128, 1), (512, 4), (2048, 16))
ROPE_THETA = 10000.0
RMS_EPS = 1e-6
N_EXPERTS = 8
TOP_K = 2

VMEM_LIMIT_BYTES = 56 * 1024 * 1024
LANES = 128

SMALL_OFF = 7 * GDN_WIDTH
POOL_OFF = SMALL_OFF + LANES
PROJ_WIDTH = POOL_OFF + POOL_WIDTH

_HI = lax.Precision.HIGHEST


def _row_tile(n_rows, want):
    t = min(want, n_rows)
    while n_rows % t:
        t //= 2
    return t


def _params(*sem):
    return pltpu.CompilerParams(dimension_semantics=sem, vmem_limit_bytes=VMEM_LIMIT_BYTES)


def _resident(shape):
    zeros = (0,) * len(shape)
    return pl.BlockSpec(shape, lambda *_: zeros, pipeline_mode=pl.Buffered(1))


def _dot(a, b, precision=None):
    return jnp.dot(a, b, preferred_element_type=jnp.float32, precision=precision)


def _dot_nt(a, b):
    return lax.dot_general(a, b, (((1,), (1,)), ((), ())), preferred_element_type=jnp.float32)


def _bf16_pieces(x, terms):
    pieces, rest = [], x
    for t in range(terms):
        pieces.append(rest.astype(jnp.bfloat16))
        if t + 1 < terms:
            rest = rest - pieces[-1].astype(jnp.float32)
    return pieces


def _select_dot(x, sel_stacked):
    terms = sel_stacked.shape[0] // x.shape[1]
    return _dot(jnp.concatenate(_bf16_pieces(x, terms), axis=1), sel_stacked)


def _prefix_dot(sel, x, terms):
    width = x.shape[1]
    wide = _dot(sel, jnp.concatenate(_bf16_pieces(x, terms), axis=1))
    return sum(wide[:, t * width:(t + 1) * width] for t in range(terms))


def _group_sum_sq(x, group_ones):
    return _dot((x * x).astype(jnp.bfloat16), group_ones)


def _norm_inproj_kernel(x_ref, g_ref, w_ref, o_ref):
    x = x_ref[...]
    ms = jnp.mean(x * x, axis=-1, keepdims=True)
    h = (x * lax.rsqrt(ms + RMS_EPS) * g_ref[...]).astype(jnp.bfloat16)
    o_ref[...] = _dot(h, w_ref[...])


def norm_inproj(x2, g, w):
    n, d = x2.shape
    width = w.shape[1]
    tm = _row_tile(n, 512)
    return pl.pallas_call(
        _norm_inproj_kernel,
        grid=(n // tm,),
        in_specs=[pl.BlockSpec((tm, d), lambda i: (i, 0)),
                  _resident((1, d)),
                  _resident((d, width))],
        out_specs=pl.BlockSpec((tm, width), lambda i: (i, 0)),
        out_shape=jax.ShapeDtypeStruct((n, width), jnp.float32),
        compiler_params=_params("parallel"),
        name="norm_inproj",
    )(x2, g.reshape(1, d), w)


def _arrange_w_in(w):
    gw, gh, sw = GDN_WIDTH, GDN_HEADS, SWA_WIDTH
    a_main = w[:, :4 * gw]
    small = w[:, 4 * gw:4 * gw + 2 * gh]
    b_main = w[:, 4 * gw + 2 * gh:4 * gw + 2 * gh + 3 * sw]
    cu = w[:, 4 * gw + 2 * gh + 3 * sw:]
    small = jnp.pad(small, ((0, 0), (0, LANES - 2 * gh)))
    return jnp.concatenate([a_main, b_main, small, cu], axis=1).astype(jnp.bfloat16)


POOL_HIST = max(POOL_SIZES)


def _pool_outproj_kernel(x_ref, ya_ref, yb_ref, cu_ref, size_ref, pw_ref, ps_ref,
                         wa_ref, wb_ref, wc_ref, o_ref, ext_s):
    f32, bf16 = jnp.float32, jnp.bfloat16
    tm = x_ref.shape[0]
    t_blk = pl.program_id(1)

    @pl.when(t_blk == 0)
    def _():
        ext_s[0:POOL_HIST, :] = jnp.zeros((POOL_HIST, POOL_WIDTH), f32)

    cu = cu_ref[...]
    ext_s[POOL_HIST:POOL_HIST + tm, :] = cu
    ext = ext_s[...]
    ext_s[0:POOL_HIST, :] = ext_s[tm:tm + POOL_HIST, :]
    size = size_ref[...]
    win = ext
    total = jnp.zeros_like(ext)
    shift = 1
    while shift < POOL_HIST:
        win = win + pltpu.roll(win, shift, 0)
        shift *= 2
        total = jnp.where(size == shift, win, total)
    total = total[POOL_HIST:, :]
    pos = t_blk * tm + lax.broadcasted_iota(jnp.int32, (tm, 1), 0)
    count = jnp.minimum(pos + 1, size).astype(f32)
    pooled = total / count - cu
    yc = _dot(pooled.astype(bf16), pw_ref[...]) * ps_ref[...]

    acc = x_ref[...]
    acc += _dot(ya_ref[...].astype(bf16), wa_ref[...])
    acc += _dot(yb_ref[...].astype(bf16), wb_ref[...])
    acc += _dot(yc.astype(bf16), wc_ref[...])
    o_ref[...] = acc


def pool_outproj_residual(x2, ya, yb, proj, bsz, s_len, pool_w, pool_scale, w_out):
    n, d = x2.shape
    tm = _row_tile(s_len, 1024)
    nt = s_len // tm
    bf16 = jnp.bfloat16
    wa = w_out[:GDN_WIDTH].astype(bf16)
    wb = w_out[GDN_WIDTH:GDN_WIDTH + SWA_WIDTH].astype(bf16)
    wc = w_out[GDN_WIDTH + SWA_WIDTH:].astype(bf16)
    size = jnp.asarray(np.repeat(np.asarray(POOL_SIZES, np.int32), POOL_GROUP_DIM).reshape(1, POOL_WIDTH))
    pw = jax.scipy.linalg.block_diag(*[pool_w[g] for g in range(POOL_GROUPS)]).astype(bf16)
    row = lambda width, cb=0: pl.BlockSpec((tm, width), lambda b, t: (b * nt + t, cb))
    return pl.pallas_call(
        _pool_outproj_kernel,
        grid=(bsz, nt),
        in_specs=[row(d), row(GDN_WIDTH), row(SWA_WIDTH), row(POOL_WIDTH, POOL_OFF // POOL_WIDTH),
                  _resident((1, POOL_WIDTH)), _resident(pw.shape), _resident((1, POOL_WIDTH)),
                  _resident(wa.shape), _resident(wb.shape), _resident(wc.shape)],
        out_specs=row(d),
        out_shape=jax.ShapeDtypeStruct((n, d), jnp.float32),
        scratch_shapes=[pltpu.VMEM((tm + POOL_HIST, POOL_WIDTH), jnp.float32)],
        compiler_params=_params("parallel", "arbitrary"),
        name="pool_outproj_residual",
    )(x2, ya, yb, proj, size, pw, pool_scale.reshape(1, POOL_WIDTH), wa, wb, wc)


def _ffn_kernel(n_chunks, x_ref, g_ref, wg_ref, wu_ref, wd_ref, o_ref):
    x = x_ref[...]
    ms = jnp.mean(x * x, axis=-1, keepdims=True)
    h = (x * lax.rsqrt(ms + RMS_EPS) * g_ref[...]).astype(jnp.bfloat16)
    tc = wg_ref.shape[1] // n_chunks
    acc = x
    for c in range(n_chunks):
        gate = _dot(h, wg_ref[:, c * tc:(c + 1) * tc])
        up = _dot(h, wu_ref[:, c * tc:(c + 1) * tc])
        act = (gate * jax.nn.sigmoid(gate) * up).astype(jnp.bfloat16)
        acc = acc + _dot(act, wd_ref[c * tc:(c + 1) * tc, :])
    o_ref[...] = acc


def ffn_dense(x2, g, w_gate, w_up, w_down):
    n, d = x2.shape
    d_ff = w_gate.shape[1]
    tm = _row_tile(n, 512)
    n_chunks = 2 if d_ff % (2 * LANES) == 0 else 1
    row = pl.BlockSpec((tm, d), lambda i: (i, 0))
    return pl.pallas_call(
        functools.partial(_ffn_kernel, n_chunks),
        grid=(n // tm,),
        in_specs=[row, _resident((1, d)), _resident((d, d_ff)), _resident((d, d_ff)),
                  _resident((d_ff, d))],
        out_specs=row,
        out_shape=jax.ShapeDtypeStruct((n, d), jnp.float32),
        compiler_params=_params("parallel"),
        name="ffn_dense",
    )(x2, g.reshape(1, d), w_gate.astype(jnp.bfloat16), w_up.astype(jnp.bfloat16),
      w_down.astype(jnp.bfloat16))


NO_EXPERT = -1e30
ROUTE_EXPERT, ROUTE_GATE, ROUTE_RANK = 0, TOP_K, 2 * TOP_K


def _router_kernel(x_ref, g_ref, wr_ref, br_ref, ltri_ref, h_ref, route_ref, count_ref, count_s):
    f32 = jnp.float32
    tm = x_ref.shape[0]

    @pl.when(pl.program_id(0) == 0)
    def _():
        count_s[...] = jnp.zeros_like(count_s)

    x = x_ref[...]
    ms = jnp.mean(x * x, axis=-1, keepdims=True)
    h = x * lax.rsqrt(ms + RMS_EPS) * g_ref[...]
    h_ref[...] = h.astype(jnp.bfloat16)
    logits = _dot(h, wr_ref[...], _HI) + br_ref[...]
    lane = lax.broadcasted_iota(jnp.int32, (tm, LANES), 1)

    def top(vals):
        best = jnp.max(vals, axis=-1, keepdims=True)
        return best, jnp.min(jnp.where(vals == best, lane, LANES), axis=-1, keepdims=True)

    m1, e1 = top(logits)
    m2, e2 = top(jnp.where(lane == e1, 2 * NO_EXPERT, logits))
    t = jnp.exp(m2 - m1)
    gate1 = 1.0 / (1.0 + t)
    gate2 = t / (1.0 + t)
    sel1, sel2 = lane == e1, lane == e2
    chosen = (sel1 | sel2).astype(f32)
    before = _dot(ltri_ref[...], chosen.astype(jnp.bfloat16)) + count_s[...]
    rank1 = jnp.sum(jnp.where(sel1, before, 0.0), axis=-1, keepdims=True)
    rank2 = jnp.sum(jnp.where(sel2, before, 0.0), axis=-1, keepdims=True)
    count_s[...] += jnp.sum(chosen, axis=0, keepdims=True)
    count_ref[...] = count_s[...]
    record = jnp.zeros((tm, LANES), f32)
    for pos, val in ((ROUTE_EXPERT, e1.astype(f32)), (ROUTE_EXPERT + 1, e2.astype(f32)),
                     (ROUTE_GATE, gate1), (ROUTE_GATE + 1, gate2),
                     (ROUTE_RANK, rank1), (ROUTE_RANK + 1, rank2)):
        record = jnp.where(lane == pos, val, record)
    route_ref[...] = record


def norm_router(x2, g, router_w, router_b):
    n, d = x2.shape
    tm = _row_tile(n, 1024)
    wr = jnp.pad(router_w, ((0, 0), (0, LANES - N_EXPERTS)))
    br = jnp.pad(router_b, (0, LANES - N_EXPERTS), constant_values=NO_EXPERT).reshape(1, LANES)
    ti = np.arange(tm)
    ltri = jnp.asarray(ti[None, :] < ti[:, None], jnp.bfloat16)
    return pl.pallas_call(
        _router_kernel,
        grid=(n // tm,),
        in_specs=[pl.BlockSpec((tm, d), lambda i: (i, 0)), _resident((1, d)),
                  _resident((d, LANES)), _resident((1, LANES)), _resident((tm, tm))],
        out_specs=[pl.BlockSpec((tm, d), lambda i: (i, 0)),
                   pl.BlockSpec((tm, LANES), lambda i: (i, 0)),
                   pl.BlockSpec((1, LANES), lambda i: (0, 0))],
        out_shape=[jax.ShapeDtypeStruct((n, d), jnp.bfloat16),
                   jax.ShapeDtypeStruct((n, LANES), jnp.float32),
                   jax.ShapeDtypeStruct((1, LANES), jnp.float32)],
        scratch_shapes=[pltpu.VMEM((1, LANES), jnp.float32)],
        compiler_params=_params("arbitrary"),
        name="norm_router",
    )(x2, g.reshape(1, d), wr, br, ltri)


def _moe_kernel(be_ref, nb_ref, xs_ref, wg_ref, wu_ref, wd_ref, o_ref, acc_s):
    j = pl.program_id(0)
    c = pl.program_id(1)
    last = pl.num_programs(1) - 1

    @pl.when(j < nb_ref[0])
    def _():
        h = xs_ref[...]
        gate = _dot(h, wg_ref[0])
        up = _dot(h, wu_ref[0])
        act = (gate * jax.nn.sigmoid(gate) * up).astype(jnp.bfloat16)
        part = _dot(act, wd_ref[0])

        @pl.when(c == 0)
        def _():
            acc_s[...] = part

        @pl.when(c > 0)
        def _():
            acc_s[...] += part

        @pl.when(c == last)
        def _():
            o_ref[...] = acc_s[...].astype(o_ref.dtype)

    @pl.when(jnp.logical_and(j >= nb_ref[0], c == last))
    def _():
        o_ref[...] = jnp.zeros_like(o_ref)


def moe_experts(xs, block_expert, n_used, w_gate, w_up, w_down, tm, tc):
    n_slots, d = xs.shape
    d_e = w_gate.shape[2]
    n_blocks = n_slots // tm
    n_c = d_e // tc

    def live(j, nb):
        return jnp.minimum(j, nb[0] - 1)

    def chunk(j, c, nb):
        return jnp.where(j < nb[0], c, n_c - 1)

    return pl.pallas_call(
        _moe_kernel,
        grid_spec=pltpu.PrefetchScalarGridSpec(
            num_scalar_prefetch=2,
            grid=(n_blocks, n_c),
            in_specs=[
                pl.BlockSpec((tm, d), lambda j, c, be, nb: (live(j, nb), 0)),
                pl.BlockSpec((1, d, tc), lambda j, c, be, nb: (be[live(j, nb)], 0, chunk(j, c, nb))),
                pl.BlockSpec((1, d, tc), lambda j, c, be, nb: (be[live(j, nb)], 0, chunk(j, c, nb))),
                pl.BlockSpec((1, tc, d), lambda j, c, be, nb: (be[live(j, nb)], chunk(j, c, nb), 0)),
            ],
            out_specs=pl.BlockSpec((tm, d), lambda j, c, be, nb: (j, 0)),
            scratch_shapes=[pltpu.VMEM((tm, d), jnp.float32)],
        ),
        out_shape=jax.ShapeDtypeStruct((n_slots, d), jnp.bfloat16),
        compiler_params=_params("arbitrary", "arbitrary"),
        name="moe_experts",
    )(block_expert, n_used, xs, w_gate, w_up, w_down)


def moe_block(x2, g, router_w, router_b, w_gate, w_up, w_down):
    n, d = x2.shape
    tm = _row_tile(n, 1024)
    h, route, count = norm_router(x2, g, router_w, router_b)
    expert = route[:, ROUTE_EXPERT:ROUTE_EXPERT + TOP_K].astype(jnp.int32)
    gates = route[:, ROUTE_GATE:ROUTE_GATE + TOP_K]
    rank = route[:, ROUTE_RANK:ROUTE_RANK + TOP_K].astype(jnp.int32)
    counts = count[0, :N_EXPERTS].astype(jnp.int32)
    padded = (counts + tm - 1) // tm * tm
    padded_end = jnp.cumsum(padded)
    padded_start = padded_end - padded
    slot = jnp.take(padded_start, expert) + rank
    n_blocks = -(-(n * TOP_K) // tm) + N_EXPERTS
    n_slots = n_blocks * tm
    token = jnp.broadcast_to(jnp.arange(n, dtype=jnp.int32)[:, None], (n, TOP_K))
    slot_token = jnp.zeros((n_slots,), jnp.int32).at[slot.reshape(-1)].set(token.reshape(-1))
    block_start = jnp.arange(n_blocks) * tm
    block_expert = jnp.minimum(
        jnp.sum(block_start[:, None] >= padded_end[None, :], axis=1), N_EXPERTS - 1).astype(jnp.int32)
    n_used = (padded_end[-1] // tm).astype(jnp.int32).reshape(1)
    xs = jnp.take(h, slot_token, axis=0)
    d_e = w_gate.shape[2]
    tc = d_e // 4 if d_e % (4 * LANES) == 0 else d_e
    y_slots = moe_experts(xs, block_expert, n_used, w_gate.astype(jnp.bfloat16),
                          w_up.astype(jnp.bfloat16), w_down.astype(jnp.bfloat16), tm, tc)
    out = x2
    for j in range(TOP_K):
        out = out + gates[:, j:j + 1] * jnp.take(y_slots, slot[:, j], axis=0).astype(jnp.float32)
    return out


GDN_BLOCK = 512
GDN_GROUP = 2
CONV_PAD = 8
GDN_PAIRS = GDN_HEADS // 2
SELECT_TERMS = 3
PAIR_ROWS = 2 * GDN_CHUNK


def _gdn_kernel(aqkv_ref, az_ref, small_ref, convw_ref, aneg_ref, bias_ref, gnorm_ref,
                eeven_ref, eodd_ref, ebeta_ref, bd_ref, ltri_ref, o_ref,
                ext_s, state_s, q_s, k_s, kb_s, qd_s, rhs_s, attn_s, w_s, kdt_s,
                gc_s, kd_s, u_s, oh_s):
    f32, bf16 = jnp.float32, jnp.bfloat16
    c = GDN_CHUNK
    tb = aqkv_ref.shape[0]
    gw = GDN_WIDTH
    nc = tb // c

    @pl.when(pl.program_id(1) == 0)
    def _():
        ext_s[0:CONV_PAD, :] = jnp.zeros((CONV_PAD, 3 * gw), f32)
        state_s[...] = jnp.zeros_like(state_s)

    ext_s[CONV_PAD:CONV_PAD + tb, :] = aqkv_ref[...]
    conv = convw_ref[0:1, :] * ext_s[pl.ds(CONV_PAD - CONV_K + 1, tb), :]
    for j in range(1, CONV_K):
        conv += convw_ref[j:j + 1, :] * ext_s[pl.ds(CONV_PAD - CONV_K + 1 + j, tb), :]
    ext_s[0:CONV_PAD, :] = ext_s[tb:tb + CONV_PAD, :]
    qkv = conv * jax.nn.sigmoid(conv)
    q, k, v = qkv[:, :gw], qkv[:, gw:2 * gw], qkv[:, 2 * gw:]

    bd = bd_ref[...]
    q = q * lax.rsqrt(_group_sum_sq(q, bd) + RMS_EPS) * (HEAD_DIM ** -0.5)
    k = k * lax.rsqrt(_group_sum_sq(k, bd) + RMS_EPS)

    sm = small_ref[...]
    pre = sm + bias_ref[...]
    softplus = jnp.maximum(pre, 0.0) + jnp.log(1.0 + jnp.exp(-jnp.abs(pre)))
    g = _prefix_dot(ltri_ref[...], aneg_ref[...] * softplus, SELECT_TERMS)
    g_even = _select_dot(g, eeven_ref[...])
    g_odd = _select_dot(g, eodd_ref[...])
    bb = _select_dot(jax.nn.sigmoid(sm), ebeta_ref[...])
    lane = lax.broadcasted_iota(jnp.int32, (1, gw), 1)
    even_lane = (lane // HEAD_DIM) % 2 == 0
    gb = jnp.where(even_lane, g_even, g_odd)
    gb3 = gb.reshape(nc, c, gw)
    k_dec = (k.reshape(nc, c, gw) * jnp.exp(gb3[:, c - 1:c, :] - gb3)).reshape(tb, gw)
    eg = jnp.exp(gb)
    kb = k * bb

    def stack(x):
        x3 = x.reshape(nc, c, gw)
        zero = jnp.zeros_like(x3)
        return jnp.concatenate([jnp.where(even_lane, x3, zero), jnp.where(even_lane, zero, x3)], axis=1)

    q_s[...] = stack(q.astype(bf16))
    k_s[...] = stack(k.astype(bf16))
    kb_s[...] = stack(kb.astype(bf16))
    qd_s[...] = stack((q * eg).astype(bf16))
    kd_s[...] = stack(k_dec)
    vb_st = stack((v * bb).astype(bf16))
    kbe_st = stack((kb * eg).astype(bf16))
    for p in range(GDN_PAIRS):
        ps = slice(p * LANES, (p + 1) * LANES)
        rhs_s[:, :, 2 * p * LANES:(2 * p + 1) * LANES] = vb_st[:, :, ps]
        rhs_s[:, :, (2 * p + 1) * LANES:(2 * p + 2) * LANES] = kbe_st[:, :, ps]
    gc_s[...] = jnp.concatenate([g_even.reshape(nc, c, gw), g_odd.reshape(nc, c, gw)], axis=1)

    row = lax.broadcasted_iota(jnp.int32, (PAIR_ROWS, PAIR_ROWS), 0)
    col = lax.broadcasted_iota(jnp.int32, (PAIR_ROWS, PAIR_ROWS), 1)
    causal = (row // c == col // c) & (row >= col)
    strict = causal & (row > col)
    eye = (row == col).astype(f32)

    def phase_a(gi, carry):
        units = [(gi * GDN_GROUP + ci, p) for ci in range(GDN_GROUP) for p in range(GDN_PAIRS)]
        at = lambda ref, n, p: ref[n, :, p * LANES:(p + 1) * LANES]
        k_u = [at(k_s, n, p) for n, p in units]
        kk = [_dot_nt(at(kb_s, n, p), kx) for (n, p), kx in zip(units, k_u)]
        qk = [_dot_nt(at(q_s, n, p), kx) for (n, p), kx in zip(units, k_u)]
        gc = [at(gc_s, n, p) for n, p in units]
        dm = [jnp.where(causal, jnp.exp(jnp.minimum(gx - gx.T, 0.0)), 0.0) for gx in gc]
        a = [jnp.where(strict, x * d, 0.0) for x, d in zip(kk, dm)]
        for (n, p), x, d in zip(units, qk, dm):
            attn_s[n, :, p * LANES:(p + 1) * LANES] = (x * d).astype(bf16)
        for (n, p) in units:
            kdt_s[n, :, p * LANES:(p + 1) * LANES] = at(kd_s, n, p).T.astype(bf16)
        ab = [x.astype(bf16) for x in a]
        sq = [_dot(x, x) for x in ab]
        inv = [eye - x for x in a]
        for _ in range(4):
            sqb = [x.astype(bf16) for x in sq]
            inv = [t + _dot(t.astype(bf16), x) for t, x in zip(inv, sqb)]
            sq = [_dot(x, x) for x in sqb]
        inv = [t + _dot(t.astype(bf16), x.astype(bf16)) for t, x in zip(inv, sq)]
        for (n, p), t in zip(units, inv):
            uw = _dot(t.astype(bf16), rhs_s[n, :, 2 * p * LANES:(2 * p + 2) * LANES])
            u_s[n, :, p * LANES:(p + 1) * LANES] = uw[:, :LANES]
            w_s[n, :, p * LANES:(p + 1) * LANES] = uw[:, LANES:].astype(bf16)
        return carry

    lax.fori_loop(0, nc // GDN_GROUP, phase_a, 0)

    def phase_b(n, carry):
        pairs = range(GDN_PAIRS)
        at = lambda ref, p: ref[n, :, p * LANES:(p + 1) * LANES]
        s = [state_s[p] for p in pairs]
        sb = [x.astype(bf16) for x in s]
        both = [_dot(jnp.concatenate([at(w_s, p), at(qd_s, p)], axis=0), sb[p]) for p in pairs]
        vn = [(at(u_s, p) - both[p][:PAIR_ROWS]).astype(bf16) for p in pairs]
        o_st = [both[p][PAIR_ROWS:] + _dot(at(attn_s, p), vn[p]) for p in pairs]
        upd = [_dot(at(kdt_s, p), vn[p]) for p in pairs]
        for p in pairs:
            ps = slice(p * LANES, (p + 1) * LANES)
            g_last = jnp.where(even_lane[:, :LANES], gc_s[n, c - 1:c, ps], gc_s[n, 2 * c - 1:2 * c, ps])
            state_s[p] = s[p] * jnp.exp(g_last) + upd[p]
            oh_s[pl.ds(pl.multiple_of(n * c, c), c), p * LANES:(p + 1) * LANES] = (
                o_st[p][:c] + o_st[p][c:])
        return carry

    lax.fori_loop(0, nc, phase_b, 0)

    o = oh_s[...]
    ms = _group_sum_sq(o, bd) * (1.0 / HEAD_DIM)
    z = az_ref[...]
    o_ref[...] = o * lax.rsqrt(ms + RMS_EPS) * gnorm_ref[...] * (z * jax.nn.sigmoid(z))


def gdn_mixer(proj, bsz, s_len, conv_w, a_log, dt_bias, gdn_norm):
    n = bsz * s_len
    gw, nh, c = GDN_WIDTH, GDN_HEADS, GDN_CHUNK
    tb = _row_tile(s_len, GDN_BLOCK)
    nt = s_len // tb
    nc = tb // c
    assert nc % GDN_GROUP == 0 and nh % 2 == 0
    f32, bf16 = jnp.float32, jnp.bfloat16
    aneg = jnp.zeros((1, LANES), f32).at[0, :nh].set(-jnp.exp(a_log))
    bias = jnp.zeros((1, LANES), f32).at[0, :nh].set(dt_bias)
    gnorm = jnp.tile(gdn_norm, nh).reshape(1, gw)
    lane_head = np.arange(gw) // HEAD_DIM
    src = np.arange(LANES)[:, None]
    stacked_sel = lambda m: jnp.asarray(np.tile(m, (SELECT_TERMS, 1)), bf16)
    eeven = stacked_sel(src == (lane_head // 2 * 2)[None, :])
    eodd = stacked_sel(src == (lane_head // 2 * 2 + 1)[None, :])
    ebeta = stacked_sel(src == nh + lane_head[None, :])
    bd = jnp.asarray(lane_head[:, None] == lane_head[None, :], bf16)
    ti = np.arange(tb)
    ltri = jnp.asarray((ti[:, None] // c == ti[None, :] // c) & (ti[None, :] <= ti[:, None]), bf16)
    rows = lambda width, cb: pl.BlockSpec((tb, width), lambda b, t: (b * nt + t, cb))
    stacked = lambda dt, width=gw: pltpu.VMEM((nc, PAIR_ROWS, width), dt)
    return pl.pallas_call(
        _gdn_kernel,
        grid=(bsz, nt),
        in_specs=[rows(3 * gw, 0), rows(gw, 3), rows(LANES, SMALL_OFF // LANES),
                  _resident((CONV_K, 3 * gw)), _resident((1, LANES)), _resident((1, LANES)),
                  _resident((1, gw)), _resident(eeven.shape), _resident(eodd.shape),
                  _resident(ebeta.shape), _resident((gw, gw)), _resident((tb, tb))],
        out_specs=pl.BlockSpec((tb, gw), lambda b, t: (b * nt + t, 0)),
        out_shape=jax.ShapeDtypeStruct((n, gw), f32),
        scratch_shapes=[
            pltpu.VMEM((tb + CONV_PAD, 3 * gw), f32),
            pltpu.VMEM((GDN_PAIRS, LANES, LANES), f32),
            stacked(bf16), stacked(bf16), stacked(bf16), stacked(bf16),
            stacked(bf16, 2 * gw),
            stacked(bf16), stacked(bf16), stacked(bf16),
            stacked(f32), stacked(f32), stacked(f32),
            pltpu.VMEM((tb, gw), f32)],
        compiler_params=_params("parallel", "arbitrary"),
        name="gdn_mixer",
    )(proj, proj, proj, conv_w, aneg, bias, gnorm, eeven, eodd, ebeta, bd, ltri)


SWA_SPAN = 128
SWA_PREP_ROWS = 512
SWA_GROUP = 4
MASKED_SCORE = -1e30


def _swa_kernel(q_ref, k_ref, v_ref, cos_ref, sin_ref, qn_ref, kn_ref, bd_ref, o_ref,
                q_s, k_s, v_s, op_s, lse_s):
    f32, bf16 = jnp.float32, jnp.bfloat16
    s_len = q_ref.shape[0]
    blk = SWA_SPAN
    lane = lax.broadcasted_iota(jnp.int32, (1, LANES), 1)
    first_half = (lane % HEAD_DIM) < (HEAD_DIM // 2)
    head0 = lane < HEAD_DIM
    bd = bd_ref[...]
    prep_rows = min(SWA_PREP_ROWS, s_len)

    def normed_rotary(x, gain, cos, sin, scale):
        ms = _group_sum_sq(x, bd) * (1.0 / HEAD_DIM)
        y = x * lax.rsqrt(ms + RMS_EPS) * gain
        partner = jnp.where(first_half, pltpu.roll(y, LANES - HEAD_DIM // 2, 1),
                            pltpu.roll(y, HEAD_DIM // 2, 1))
        return (y * cos + partner * sin) * scale

    def prologue(i, carry):
        rows = pl.ds(pl.multiple_of(i * prep_rows, prep_rows), prep_rows)
        cos, sin = cos_ref[rows, :], sin_ref[rows, :]
        q_s[rows, :] = normed_rotary(q_ref[rows, :], qn_ref[...], cos, sin, HEAD_DIM ** -0.5)
        k_s[rows, :] = normed_rotary(k_ref[rows, :], kn_ref[...], cos, sin, 1.0)
        v_s[rows, :] = v_ref[rows, :]
        return carry

    lax.fori_loop(0, s_len // prep_rows, prologue, 0)

    qi = lax.broadcasted_iota(jnp.int32, (blk, 2 * blk), 0)
    kj = lax.broadcasted_iota(jnp.int32, (blk, 2 * blk), 1)
    dist = blk + qi - kj
    band = (dist >= 0) & (dist <= SWA_SPAN)
    ones_kv = jnp.ones((2 * blk, LANES), bf16)

    for pidx, (window, dil) in enumerate(DILATED_PAIRS):
        n_blk = s_len // (dil * blk)

        def rows_of(r, i):
            start = r + i * (blk * dil)
            if dil == 1:
                return pl.ds(start, blk)
            return pl.ds(start, blk, stride=dil)

        def block_group(gi, carry):
            units = [gi * SWA_GROUP + j for j in range(SWA_GROUP)]
            ri = [(u // n_blk, u % n_blk) for u in units]
            cur = [rows_of(r, i) for r, i in ri]
            prev = [rows_of(r, jnp.maximum(i - 1, 0)) for r, i in ri]
            qb = [q_s[c, :] for c in cur]
            kw = [jnp.concatenate([k_s[pv, :], k_s[c, :]], axis=0).astype(bf16)
                  for pv, c in zip(prev, cur)]
            vw = [jnp.concatenate([jnp.concatenate([v_s[pv, :], v_s[c, :]], axis=0).astype(bf16),
                                   ones_kv], axis=1) for pv, c in zip(prev, cur)]
            valid = [band & (kj >= jnp.where(i > 0, 0, blk)) for _, i in ri]
            chains = [(j, h) for j in range(SWA_GROUP) for h in range(2)]
            qh = [jnp.where(head0 if h == 0 else ~head0, qb[j], 0.0).astype(bf16) for j, h in chains]
            s = [jnp.where(valid[j], _dot_nt(x, kw[j]), MASKED_SCORE) for (j, h), x in zip(chains, qh)]
            m = [jnp.max(x, axis=-1, keepdims=True) for x in s]
            p = [jnp.exp((x - mx).astype(bf16)) for x, mx in zip(s, m)]
            pv = [_dot(x, vw[j]) for (j, h), x in zip(chains, p)]
            o = [x[:, :LANES] / x[:, LANES:] for x in pv]
            lse = [mx + jnp.log(x[:, LANES:]) for mx, x in zip(m, pv)]
            for j in range(SWA_GROUP):
                op_s[pidx, cur[j], :] = jnp.where(head0, o[2 * j], o[2 * j + 1])
                lse_s[pidx, cur[j], :] = jnp.where(head0, lse[2 * j], lse[2 * j + 1])
            return carry

        lax.fori_loop(0, s_len // (blk * SWA_GROUP), block_group, 0)

    def epilogue(i, carry):
        rows = pl.ds(pl.multiple_of(i * prep_rows, prep_rows), prep_rows)
        l0, l1, l2 = lse_s[0, rows, :], lse_s[1, rows, :], lse_s[2, rows, :]
        m = jnp.maximum(jnp.maximum(l0, l1), l2)
        w0, w1, w2 = jnp.exp(l0 - m), jnp.exp(l1 - m), jnp.exp(l2 - m)
        mix = w0 * op_s[0, rows, :] + w1 * op_s[1, rows, :] + w2 * op_s[2, rows, :]
        o_ref[rows, :] = mix / (w0 + w1 + w2)
        return carry

    lax.fori_loop(0, s_len // prep_rows, epilogue, 0)


def swa_mixer(proj, bsz, s_len, q_norm, k_norm):
    assert all(w // d == SWA_SPAN for w, d in DILATED_PAIRS) and len(DILATED_PAIRS) == 3
    assert s_len % (SWA_SPAN * max(d for _, d in DILATED_PAIRS)) == 0
    n = bsz * s_len
    f32 = jnp.float32
    n_pairs = SWA_WIDTH // LANES
    half = HEAD_DIM // 2
    inv_freq = ROPE_THETA ** (-jnp.arange(0, HEAD_DIM, 2, dtype=f32) / HEAD_DIM)
    ang = jnp.arange(s_len, dtype=f32)[:, None] * inv_freq[None, :]
    cos = jnp.tile(jnp.cos(ang), (1, LANES // half))
    sin = jnp.tile(jnp.concatenate([-jnp.sin(ang), jnp.sin(ang)], axis=1), (1, LANES // HEAD_DIM))
    gain = lambda g: jnp.tile(g, LANES // HEAD_DIM).reshape(1, LANES)
    lane_head = np.arange(LANES) // HEAD_DIM
    bd = jnp.asarray(lane_head[:, None] == lane_head[None, :], jnp.bfloat16)
    q_col0 = 4 * GDN_WIDTH // LANES
    col = lambda which: pl.BlockSpec(
        (s_len, LANES), lambda b, p: (b, q_col0 + which * n_pairs + p))
    seq = pltpu.VMEM((s_len, LANES), f32)
    per_pattern = pltpu.VMEM((len(DILATED_PAIRS), s_len, LANES), f32)
    return pl.pallas_call(
        _swa_kernel,
        grid=(bsz, n_pairs),
        in_specs=[col(0), col(1), col(2), _resident((s_len, LANES)), _resident((s_len, LANES)),
                  _resident((1, LANES)), _resident((1, LANES)), _resident((LANES, LANES))],
        out_specs=pl.BlockSpec((s_len, LANES), lambda b, p: (b, p)),
        out_shape=jax.ShapeDtypeStruct((n, SWA_WIDTH), f32),
        scratch_shapes=[seq, seq, seq, per_pattern, per_pattern],
        compiler_params=_params("parallel", "parallel"),
        name="swa_mixer",
    )(proj, proj, proj, cos, sin, gain(q_norm), gain(k_norm), bd)


def kernel(x, norm_mix, w_in, conv_w, a_log, dt_bias, gdn_norm, q_norm, k_norm, pool_w,
           pool_scale, w_out, norm_ffn, ffn_gate, ffn_up, ffn_down, router_w, router_b,
           exp_gate, exp_up, exp_down):
    bsz, s_len, d = x.shape
    n = bsz * s_len
    depth = w_in.shape[0]
    x2 = x.reshape(n, d)
    for layer in range(depth):
        proj = norm_inproj(x2, norm_mix[layer], _arrange_w_in(w_in[layer]))
        ya = gdn_mixer(proj, bsz, s_len, conv_w[layer], a_log[layer], dt_bias[layer],
                       gdn_norm[layer])
        yb = swa_mixer(proj, bsz, s_len, q_norm[layer], k_norm[layer])
        x2 = pool_outproj_residual(x2, ya, yb, proj, bsz, s_len, pool_w[layer],
                                   pool_scale[layer], w_out[layer])
        i = layer // 2
        if layer % 2 == 0:
            x2 = ffn_dense(x2, norm_ffn[layer], ffn_gate[i], ffn_up[i], ffn_down[i])
        else:
            x2 = moe_block(x2, norm_ffn[layer], router_w[i], router_b[i], exp_gate[i],
                           exp_up[i], exp_down[i])
    return x2.reshape(bsz, s_len, d)
```

```python
import functools

import jax
import jax.numpy as jnp
import numpy as np
from jax import lax
from jax.experimental import pallas as pl
from jax.experimental.pallas import tpu as pltpu

D_MODEL = 1024
HEAD_DIM = 64
MIX_WIDTH = D_MODEL
POOL_WIDTH = MIX_WIDTH // 4
POOL_GROUPS = 4
POOL_GROUP_DIM = POOL_WIDTH // POOL_GROUPS
POOL_SIZES = (2, 4, 8, 16)
GDN_HEADS = (MIX_WIDTH - POOL_WIDTH) // (2 * HEAD_DIM)
GDN_WIDTH = GDN_HEADS * HEAD_DIM
SWA_HEADS = (MIX_WIDTH - POOL_WIDTH - GDN_WIDTH) // HEAD_DIM
SWA_WIDTH = SWA_HEADS * HEAD_DIM
CONV_K = 4
GDN_CHUNK = 64
DILATED_PAIRS = ((128, 1), (512, 4), (2048, 16))
ROPE_THETA = 10000.0
RMS_EPS = 1e-6
N_EXPERTS = 8
TOP_K = 2

VMEM_LIMIT_BYTES = 56 * 1024 * 1024
LANES = 128

SMALL_OFF = 7 * GDN_WIDTH
POOL_OFF = SMALL_OFF + LANES
PROJ_WIDTH = POOL_OFF + POOL_WIDTH

_HI = lax.Precision.HIGHEST


def _row_tile(n_rows, want):
    t = min(want, n_rows)
    while n_rows % t:
        t //= 2
    return t


def _params(*sem):
    return pltpu.CompilerParams(dimension_semantics=sem, vmem_limit_bytes=VMEM_LIMIT_BYTES)


def _resident(shape):
    zeros = (0,) * len(shape)
    return pl.BlockSpec(shape, lambda *_: zeros, pipeline_mode=pl.Buffered(1))


def _dot(a, b, precision=None):
    return jnp.dot(a, b, preferred_element_type=jnp.float32, precision=precision)


def _dot_nt(a, b):
    return lax.dot_general(a, b, (((1,), (1,)), ((), ())), preferred_element_type=jnp.float32)


def _bf16_pieces(x, terms):
    pieces, rest = [], x
    for t in range(terms):
        pieces.append(rest.astype(jnp.bfloat16))
        if t + 1 < terms:
            rest = rest - pieces[-1].astype(jnp.float32)
    return pieces


def _select_dot(x, sel_stacked):
    terms = sel_stacked.shape[0] // x.shape[1]
    return _dot(jnp.concatenate(_bf16_pieces(x, terms), axis=1), sel_stacked)


def _prefix_dot(sel, x, terms):
    width = x.shape[1]
    wide = _dot(sel, jnp.concatenate(_bf16_pieces(x, terms), axis=1))
    return sum(wide[:, t * width:(t + 1) * width] for t in range(terms))


def _group_sum_sq(x, group_ones):
    return _dot((x * x).astype(jnp.bfloat16), group_ones)


def _norm_inproj_kernel(x_ref, g_ref, w_ref, o_ref):
    x = x_ref[...]
    ms = jnp.mean(x * x, axis=-1, keepdims=True)
    h = (x * lax.rsqrt(ms + RMS_EPS) * g_ref[...]).astype(jnp.bfloat16)
    o_ref[...] = _dot(h, w_ref[...])


def norm_inproj(x2, g, w):
    n, d = x2.shape
    width = w.shape[1]
    tm = _row_tile(n, 512)
    return pl.pallas_call(
        _norm_inproj_kernel,
        grid=(n // tm,),
        in_specs=[pl.BlockSpec((tm, d), lambda i: (i, 0)),
                  _resident((1, d)),
                  _resident((d, width))],
        out_specs=pl.BlockSpec((tm, width), lambda i: (i, 0)),
        out_shape=jax.ShapeDtypeStruct((n, width), jnp.float32),
        compiler_params=_params("parallel"),
        name="norm_inproj",
    )(x2, g.reshape(1, d), w)


def _arrange_w_in(w):
    gw, gh, sw = GDN_WIDTH, GDN_HEADS, SWA_WIDTH
    a_main = w[:, :4 * gw]
    small = w[:, 4 * gw:4 * gw + 2 * gh]
    b_main = w[:, 4 * gw + 2 * gh:4 * gw + 2 * gh + 3 * sw]
    cu = w[:, 4 * gw + 2 * gh + 3 * sw:]
    small = jnp.pad(small, ((0, 0), (0, LANES - 2 * gh)))
    return jnp.concatenate([a_main, b_main, small, cu], axis=1).astype(jnp.bfloat16)


POOL_HIST = max(POOL_SIZES)


def _pool_outproj_kernel(x_ref, ya_ref, yb_ref, cu_ref, size_ref, pw_ref, ps_ref,
                         wa_ref, wb_ref, wc_ref, o_ref, ext_s):
    f32, bf16 = jnp.float32, jnp.bfloat16
    tm = x_ref.shape[0]
    t_blk = pl.program_id(1)

    @pl.when(t_blk == 0)
    def _():
        ext_s[0:POOL_HIST, :] = jnp.zeros((POOL_HIST, POOL_WIDTH), f32)

    cu = cu_ref[...]
    ext_s[POOL_HIST:POOL_HIST + tm, :] = cu
    ext = ext_s[...]
    ext_s[0:POOL_HIST, :] = ext_s[tm:tm + POOL_HIST, :]
    size = size_ref[...]
    win = ext
    total = jnp.zeros_like(ext)
    shift = 1
    while shift < POOL_HIST:
        win = win + pltpu.roll(win, shift, 0)
        shift *= 2
        total = jnp.where(size == shift, win, total)
    total = total[POOL_HIST:, :]
    pos = t_blk * tm + lax.broadcasted_iota(jnp.int32, (tm, 1), 0)
    count = jnp.minimum(pos + 1, size).astype(f32)
    pooled = total / count - cu
    yc = _dot(pooled.astype(bf16), pw_ref[...]) * ps_ref[...]

    acc = x_ref[...]
    acc += _dot(ya_ref[...].astype(bf16), wa_ref[...])
    acc += _dot(yb_ref[...].astype(bf16), wb_ref[...])
    acc += _dot(yc.astype(bf16), wc_ref[...])
    o_ref[...] = acc


def pool_outproj_residual(x2, ya, yb, proj, bsz, s_len, pool_w, pool_scale, w_out):
    n, d = x2.shape
    tm = _row_tile(s_len, 1024)
    nt = s_len // tm
    bf16 = jnp.bfloat16
    wa = w_out[:GDN_WIDTH].astype(bf16)
    wb = w_out[GDN_WIDTH:GDN_WIDTH + SWA_WIDTH].astype(bf16)
    wc = w_out[GDN_WIDTH + SWA_WIDTH:].astype(bf16)
    size = jnp.asarray(np.repeat(np.asarray(POOL_SIZES, np.int32), POOL_GROUP_DIM).reshape(1, POOL_WIDTH))
    pw = jax.scipy.linalg.block_diag(*[pool_w[g] for g in range(POOL_GROUPS)]).astype(bf16)
    row = lambda width, cb=0: pl.BlockSpec((tm, width), lambda b, t: (b * nt + t, cb))
    return pl.pallas_call(
        _pool_outproj_kernel,
        grid=(bsz, nt),
        in_specs=[row(d), row(GDN_WIDTH), row(SWA_WIDTH), row(POOL_WIDTH, POOL_OFF // POOL_WIDTH),
                  _resident((1, POOL_WIDTH)), _resident(pw.shape), _resident((1, POOL_WIDTH)),
                  _resident(wa.shape), _resident(wb.shape), _resident(wc.shape)],
        out_specs=row(d),
        out_shape=jax.ShapeDtypeStruct((n, d), jnp.float32),
        scratch_shapes=[pltpu.VMEM((tm + POOL_HIST, POOL_WIDTH), jnp.float32)],
        compiler_params=_params("parallel", "arbitrary"),
        name="pool_outproj_residual",
    )(x2, ya, yb, proj, size, pw, pool_scale.reshape(1, POOL_WIDTH), wa, wb, wc)


MXU_DIM = 256
HIDDEN_SLICE = 512


def _hidden_slices(width):
    assert width % MXU_DIM == 0
    return [(s, min(HIDDEN_SLICE, width - s)) for s in range(0, width, HIDDEN_SLICE)]


def _swiglu_slice(h, wg, wu, wd):
    gate = _dot(h, wg)
    up = _dot(h, wu)
    return _dot((gate * jax.nn.sigmoid(gate) * up).astype(jnp.bfloat16), wd)


def _ffn_kernel(x_ref, g_ref, wg_ref, wu_ref, wd_ref, o_ref):
    x = x_ref[...]
    ms = jnp.mean(x * x, axis=-1, keepdims=True)
    h = (x * lax.rsqrt(ms + RMS_EPS) * g_ref[...]).astype(jnp.bfloat16)
    o_ref[...] = x
    for s, w in _hidden_slices(wg_ref.shape[1]):
        o_ref[...] += _swiglu_slice(h, wg_ref[:, s:s + w], wu_ref[:, s:s + w], wd_ref[s:s + w, :])


def ffn_dense(x2, g, w_gate, w_up, w_down):
    n, d = x2.shape
    d_ff = w_gate.shape[1]
    tm = _row_tile(n, 512)
    row = pl.BlockSpec((tm, d), lambda i: (i, 0))
    return pl.pallas_call(
        _ffn_kernel,
        grid=(n // tm,),
        in_specs=[row, _resident((1, d)), _resident((d, d_ff)), _resident((d, d_ff)),
                  _resident((d_ff, d))],
        out_specs=row,
        out_shape=jax.ShapeDtypeStruct((n, d), jnp.float32),
        compiler_params=_params("parallel"),
        name="ffn_dense",
    )(x2, g.reshape(1, d), w_gate.astype(jnp.bfloat16), w_up.astype(jnp.bfloat16),
      w_down.astype(jnp.bfloat16))


NO_EXPERT = -1e30
ROUTE_EXPERT, ROUTE_GATE, ROUTE_RANK = 0, TOP_K, 2 * TOP_K


def _router_kernel(x_ref, g_ref, wr_ref, br_ref, ltri_ref, h_ref, route_ref, count_ref, count_s):
    f32 = jnp.float32
    tm = x_ref.shape[0]

    @pl.when(pl.program_id(0) == 0)
    def _():
        count_s[...] = jnp.zeros_like(count_s)

    x = x_ref[...]
    ms = jnp.mean(x * x, axis=-1, keepdims=True)
    h = x * lax.rsqrt(ms + RMS_EPS) * g_ref[...]
    h_ref[...] = h.astype(jnp.bfloat16)
    logits = _dot(h, wr_ref[...], _HI) + br_ref[...]
    lane = lax.broadcasted_iota(jnp.int32, (tm, LANES), 1)

    def top(vals):
        best = jnp.max(vals, axis=-1, keepdims=True)
        return best, jnp.min(jnp.where(vals == best, lane, LANES), axis=-1, keepdims=True)

    m1, e1 = top(logits)
    m2, e2 = top(jnp.where(lane == e1, 2 * NO_EXPERT, logits))
    t = jnp.exp(m2 - m1)
    gate1 = 1.0 / (1.0 + t)
    gate2 = t / (1.0 + t)
    sel1, sel2 = lane == e1, lane == e2
    chosen = (sel1 | sel2).astype(f32)
    before = _dot(ltri_ref[...], chosen.astype(jnp.bfloat16)) + count_s[...]
    rank1 = jnp.sum(jnp.where(sel1, before, 0.0), axis=-1, keepdims=True)
    rank2 = jnp.sum(jnp.where(sel2, before, 0.0), axis=-1, keepdims=True)
    count_s[...] += jnp.sum(chosen, axis=0, keepdims=True)
    count_ref[...] = count_s[...]
    record = jnp.zeros((tm, LANES), f32)
    for pos, val in ((ROUTE_EXPERT, e1.astype(f32)), (ROUTE_EXPERT + 1, e2.astype(f32)),
                     (ROUTE_GATE, gate1), (ROUTE_GATE + 1, gate2),
                     (ROUTE_RANK, rank1), (ROUTE_RANK + 1, rank2)):
        record = jnp.where(lane == pos, val, record)
    route_ref[...] = record


def norm_router(x2, g, router_w, router_b):
    n, d = x2.shape
    tm = _row_tile(n, 1024)
    wr = jnp.pad(router_w, ((0, 0), (0, LANES - N_EXPERTS)))
    br = jnp.pad(router_b, (0, LANES - N_EXPERTS), constant_values=NO_EXPERT).reshape(1, LANES)
    ti = np.arange(tm)
    ltri = jnp.asarray(ti[None, :] < ti[:, None], jnp.bfloat16)
    return pl.pallas_call(
        _router_kernel,
        grid=(n // tm,),
        in_specs=[pl.BlockSpec((tm, d), lambda i: (i, 0)), _resident((1, d)),
                  _resident((d, LANES)), _resident((1, LANES)), _resident((tm, tm))],
        out_specs=[pl.BlockSpec((tm, d), lambda i: (i, 0)),
                   pl.BlockSpec((tm, LANES), lambda i: (i, 0)),
                   pl.BlockSpec((1, LANES), lambda i: (0, 0))],
        out_shape=[jax.ShapeDtypeStruct((n, d), jnp.bfloat16),
                   jax.ShapeDtypeStruct((n, LANES), jnp.float32),
                   jax.ShapeDtypeStruct((1, LANES), jnp.float32)],
        scratch_shapes=[pltpu.VMEM((1, LANES), jnp.float32)],
        compiler_params=_params("arbitrary"),
        name="norm_router",
    )(x2, g.reshape(1, d), wr, br, ltri)


def _moe_kernel(be_ref, nb_ref, xs_ref, wg_ref, wu_ref, wd_ref, o_ref, acc_s):
    j = pl.program_id(0)
    c = pl.program_id(1)
    last = pl.num_programs(1) - 1

    @pl.when(j < nb_ref[0])
    def _():
        h = xs_ref[...]

        @pl.when(c == 0)
        def _():
            acc_s[...] = jnp.zeros_like(acc_s)

        for s, w in _hidden_slices(wg_ref.shape[2]):
            acc_s[...] += _swiglu_slice(h, wg_ref[0, :, s:s + w], wu_ref[0, :, s:s + w],
                                        wd_ref[0, s:s + w, :])

        @pl.when(c == last)
        def _():
            o_ref[...] = acc_s[...].astype(o_ref.dtype)

    @pl.when(jnp.logical_and(j >= nb_ref[0], c == last))
    def _():
        o_ref[...] = jnp.zeros_like(o_ref)


def moe_experts(xs, block_expert, n_used, w_gate, w_up, w_down, tm, tc):
    n_slots, d = xs.shape
    d_e = w_gate.shape[2]
    n_blocks = n_slots // tm
    n_c = d_e // tc

    def live(j, nb):
        return jnp.minimum(j, nb[0] - 1)

    def chunk(j, c, nb):
        return jnp.where(j < nb[0], c, n_c - 1)

    return pl.pallas_call(
        _moe_kernel,
        grid_spec=pltpu.PrefetchScalarGridSpec(
            num_scalar_prefetch=2,
            grid=(n_blocks, n_c),
            in_specs=[
                pl.BlockSpec((tm, d), lambda j, c, be, nb: (live(j, nb), 0)),
                pl.BlockSpec((1, d, tc), lambda j, c, be, nb: (be[live(j, nb)], 0, chunk(j, c, nb))),
                pl.BlockSpec((1, d, tc), lambda j, c, be, nb: (be[live(j, nb)], 0, chunk(j, c, nb))),
                pl.BlockSpec((1, tc, d), lambda j, c, be, nb: (be[live(j, nb)], chunk(j, c, nb), 0)),
            ],
            out_specs=pl.BlockSpec((tm, d), lambda j, c, be, nb: (j, 0)),
            scratch_shapes=[pltpu.VMEM((tm, d), jnp.float32)],
        ),
        out_shape=jax.ShapeDtypeStruct((n_slots, d), jnp.bfloat16),
        compiler_params=_params("arbitrary", "arbitrary"),
        name="moe_experts",
    )(block_expert, n_used, xs, w_gate, w_up, w_down)


def moe_block(x2, g, router_w, router_b, w_gate, w_up, w_down):
    n, d = x2.shape
    tm = _row_tile(n, 1024)
    h, route, count = norm_router(x2, g, router_w, router_b)
    expert = route[:, ROUTE_EXPERT:ROUTE_EXPERT + TOP_K].astype(jnp.int32)
    gates = route[:, ROUTE_GATE:ROUTE_GATE + TOP_K]
    rank = route[:, ROUTE_RANK:ROUTE_RANK + TOP_K].astype(jnp.int32)
    counts = count[0, :N_EXPERTS].astype(jnp.int32)
    padded = (counts + tm - 1) // tm * tm
    padded_end = jnp.cumsum(padded)
    padded_start = padded_end - padded
    slot = jnp.take(padded_start, expert) + rank
    n_blocks = -(-(n * TOP_K) // tm) + N_EXPERTS
    n_slots = n_blocks * tm
    token = jnp.broadcast_to(jnp.arange(n, dtype=jnp.int32)[:, None], (n, TOP_K))
    slot_token = jnp.zeros((n_slots,), jnp.int32).at[slot.reshape(-1)].set(token.reshape(-1))
    block_start = jnp.arange(n_blocks) * tm
    block_expert = jnp.minimum(
        jnp.sum(block_start[:, None] >= padded_end[None, :], axis=1), N_EXPERTS - 1).astype(jnp.int32)
    n_used = (padded_end[-1] // tm).astype(jnp.int32).reshape(1)
    xs = jnp.take(h, slot_token, axis=0)
    d_e = w_gate.shape[2]
    tc = d_e // 2 if d_e % (2 * MXU_DIM) == 0 else d_e
    y_slots = moe_experts(xs, block_expert, n_used, w_gate.astype(jnp.bfloat16),
                          w_up.astype(jnp.bfloat16), w_down.astype(jnp.bfloat16), tm, tc)
    out = x2
    for j in range(TOP_K):
        out = out + gates[:, j:j + 1] * jnp.take(y_slots, slot[:, j], axis=0).astype(jnp.float32)
    return out


GDN_BLOCK = 512
GDN_GROUP = 2
CONV_PAD = 8
GDN_PAIRS = GDN_HEADS // 2
SELECT_TERMS = 3
PAIR_ROWS = 2 * GDN_CHUNK


def _gdn_kernel(aqkv_ref, az_ref, small_ref, convw_ref, aneg_ref, bias_ref, gnorm_ref,
                eeven_ref, eodd_ref, ebeta_ref, bd_ref, ltri_ref, o_ref,
                ext_s, state_s, q_s, k_s, kb_s, qd_s, rhs_s, attn_s, w_s, kdt_s,
                gc_s, kd_s, u_s, oh_s):
    f32, bf16 = jnp.float32, jnp.bfloat16
    c = GDN_CHUNK
    tb = aqkv_ref.shape[0]
    gw = GDN_WIDTH
    nc = tb // c

    @pl.when(pl.program_id(1) == 0)
    def _():
        ext_s[0:CONV_PAD, :] = jnp.zeros((CONV_PAD, 3 * gw), f32)
        state_s[...] = jnp.zeros_like(state_s)

    ext_s[CONV_PAD:CONV_PAD + tb, :] = aqkv_ref[...]
    conv = convw_ref[0:1, :] * ext_s[pl.ds(CONV_PAD - CONV_K + 1, tb), :]
    for j in range(1, CONV_K):
        conv += convw_ref[j:j + 1, :] * ext_s[pl.ds(CONV_PAD - CONV_K + 1 + j, tb), :]
    ext_s[0:CONV_PAD, :] = ext_s[tb:tb + CONV_PAD, :]
    qkv = conv * jax.nn.sigmoid(conv)
    q, k, v = qkv[:, :gw], qkv[:, gw:2 * gw], qkv[:, 2 * gw:]

    bd = bd_ref[...]
    q = q * lax.rsqrt(_group_sum_sq(q, bd) + RMS_EPS) * (HEAD_DIM ** -0.5)
    k = k * lax.rsqrt(_group_sum_sq(k, bd) + RMS_EPS)

    sm = small_ref[...]
    pre = sm + bias_ref[...]
    softplus = jnp.maximum(pre, 0.0) + jnp.log(1.0 + jnp.exp(-jnp.abs(pre)))
    g = _prefix_dot(ltri_ref[...], aneg_ref[...] * softplus, SELECT_TERMS)
    g_even = _select_dot(g, eeven_ref[...])
    g_odd = _select_dot(g, eodd_ref[...])
    bb = _select_dot(jax.nn.sigmoid(sm), ebeta_ref[...])
    lane = lax.broadcasted_iota(jnp.int32, (1, gw), 1)
    even_lane = (lane // HEAD_DIM) % 2 == 0
    gb = jnp.where(even_lane, g_even, g_odd)
    gb3 = gb.reshape(nc, c, gw)
    k_dec = (k.reshape(nc, c, gw) * jnp.exp(gb3[:, c - 1:c, :] - gb3)).reshape(tb, gw)
    eg = jnp.exp(gb)
    kb = k * bb

    def stack(x):
        x3 = x.reshape(nc, c, gw)
        zero = jnp.zeros_like(x3)
        return jnp.concatenate([jnp.where(even_lane, x3, zero), jnp.where(even_lane, zero, x3)], axis=1)

    q_s[...] = stack(q.astype(bf16))
    k_s[...] = stack(k.astype(bf16))
    kb_s[...] = stack(kb.astype(bf16))
    qd_s[...] = stack((q * eg).astype(bf16))
    kd_s[...] = stack(k_dec)
    vb_st = stack((v * bb).astype(bf16))
    kbe_st = stack((kb * eg).astype(bf16))
    for p in range(GDN_PAIRS):
        ps = slice(p * LANES, (p + 1) * LANES)
        rhs_s[:, :, 2 * p * LANES:(2 * p + 1) * LANES] = vb_st[:, :, ps]
        rhs_s[:, :, (2 * p + 1) * LANES:(2 * p + 2) * LANES] = kbe_st[:, :, ps]
    gc_s[...] = jnp.concatenate([g_even.reshape(nc, c, gw), g_odd.reshape(nc, c, gw)], axis=1)

    row = lax.broadcasted_iota(jnp.int32, (PAIR_ROWS, PAIR_ROWS), 0)
    col = lax.broadcasted_iota(jnp.int32, (PAIR_ROWS, PAIR_ROWS), 1)
    causal = (row // c == col // c) & (row >= col)
    strict = causal & (row > col)
    eye = (row == col).astype(f32)

    def phase_a(gi, carry):
        units = [(gi * GDN_GROUP + ci, p) for ci in range(GDN_GROUP) for p in range(GDN_PAIRS)]
        at = lambda ref, n, p: ref[n, :, p * LANES:(p + 1) * LANES]
        k_u = [at(k_s, n, p) for n, p in units]
        kk = [_dot_nt(at(kb_s, n, p), kx) for (n, p), kx in zip(units, k_u)]
        qk = [_dot_nt(at(q_s, n, p), kx) for (n, p), kx in zip(units, k_u)]
        gc = [at(gc_s, n, p) for n, p in units]
        dm = [jnp.where(causal, jnp.exp(jnp.minimum(gx - gx.T, 0.0)), 0.0) for gx in gc]
        a = [jnp.where(strict, x * d, 0.0) for x, d in zip(kk, dm)]
        for (n, p), x, d in zip(units, qk, dm):
            attn_s[n, :, p * LANES:(p + 1) * LANES] = (x * d).astype(bf16)
        for (n, p) in units:
            kdt_s[n, :, p * LANES:(p + 1) * LANES] = at(kd_s, n, p).T.astype(bf16)
        ab = [x.astype(bf16) for x in a]
        sq = [_dot(x, x) for x in ab]
        inv = [eye - x for x in a]
        for _ in range(4):
            sqb = [x.astype(bf16) for x in sq]
            both = [_dot(x, jnp.concatenate([x, t.astype(bf16)], axis=1)) for x, t in zip(sqb, inv)]
            sq = [x[:, :PAIR_ROWS] for x in both]
            inv = [t + x[:, PAIR_ROWS:] for t, x in zip(inv, both)]
        inv = [t + _dot(x.astype(bf16), t.astype(bf16)) for t, x in zip(inv, sq)]
        for (n, p), t in zip(units, inv):
            uw = _dot(t.astype(bf16), rhs_s[n, :, 2 * p * LANES:(2 * p + 2) * LANES])
            u_s[n, :, p * LANES:(p + 1) * LANES] = uw[:, :LANES]
            w_s[n, :, p * LANES:(p + 1) * LANES] = uw[:, LANES:].astype(bf16)
        return carry

    lax.fori_loop(0, nc // GDN_GROUP, phase_a, 0)

    def phase_b(n, carry):
        pairs = range(GDN_PAIRS)
        at = lambda ref, p: ref[n, :, p * LANES:(p + 1) * LANES]
        s = [state_s[p] for p in pairs]
        sb = [x.astype(bf16) for x in s]
        both = [_dot(jnp.concatenate([at(w_s, p), at(qd_s, p)], axis=0), sb[p]) for p in pairs]
        vn = [(at(u_s, p) - both[p][:PAIR_ROWS]).astype(bf16) for p in pairs]
        o_st = [both[p][PAIR_ROWS:] + _dot(at(attn_s, p), vn[p]) for p in pairs]
        upd = [_dot(at(kdt_s, p), vn[p]) for p in pairs]
        for p in pairs:
            ps = slice(p * LANES, (p + 1) * LANES)
            g_last = jnp.where(even_lane[:, :LANES], gc_s[n, c - 1:c, ps], gc_s[n, 2 * c - 1:2 * c, ps])
            state_s[p] = s[p] * jnp.exp(g_last) + upd[p]
            oh_s[pl.ds(pl.multiple_of(n * c, c), c), p * LANES:(p + 1) * LANES] = (
                o_st[p][:c] + o_st[p][c:])
        return carry

    lax.fori_loop(0, nc, phase_b, 0)

    o = oh_s[...]
    ms = _group_sum_sq(o, bd) * (1.0 / HEAD_DIM)
    z = az_ref[...]
    o_ref[...] = o * lax.rsqrt(ms + RMS_EPS) * gnorm_ref[...] * (z * jax.nn.sigmoid(z))


def gdn_mixer(proj, bsz, s_len, conv_w, a_log, dt_bias, gdn_norm):
    n = bsz * s_len
    gw, nh, c = GDN_WIDTH, GDN_HEADS, GDN_CHUNK
    tb = _row_tile(s_len, GDN_BLOCK)
    nt = s_len // tb
    nc = tb // c
    assert nc % GDN_GROUP == 0 and nh % 2 == 0
    f32, bf16 = jnp.float32, jnp.bfloat16
    aneg = jnp.zeros((1, LANES), f32).at[0, :nh].set(-jnp.exp(a_log))
    bias = jnp.zeros((1, LANES), f32).at[0, :nh].set(dt_bias)
    gnorm = jnp.tile(gdn_norm, nh).reshape(1, gw)
    lane_head = np.arange(gw) // HEAD_DIM
    src = np.arange(LANES)[:, None]
    stacked_sel = lambda m: jnp.asarray(np.tile(m, (SELECT_TERMS, 1)), bf16)
    eeven = stacked_sel(src == (lane_head // 2 * 2)[None, :])
    eodd = stacked_sel(src == (lane_head // 2 * 2 + 1)[None, :])
    ebeta = stacked_sel(src == nh + lane_head[None, :])
    bd = jnp.asarray(lane_head[:, None] == lane_head[None, :], bf16)
    ti = np.arange(tb)
    ltri = jnp.asarray((ti[:, None] // c == ti[None, :] // c) & (ti[None, :] <= ti[:, None]), bf16)
    rows = lambda width, cb: pl.BlockSpec((tb, width), lambda b, t: (b * nt + t, cb))
    stacked = lambda dt, width=gw: pltpu.VMEM((nc, PAIR_ROWS, width), dt)
    return pl.pallas_call(
        _gdn_kernel,
        grid=(bsz, nt),
        in_specs=[rows(3 * gw, 0), rows(gw, 3), rows(LANES, SMALL_OFF // LANES),
                  _resident((CONV_K, 3 * gw)), _resident((1, LANES)), _resident((1, LANES)),
                  _resident((1, gw)), _resident(eeven.shape), _resident(eodd.shape),
                  _resident(ebeta.shape), _resident((gw, gw)), _resident((tb, tb))],
        out_specs=pl.BlockSpec((tb, gw), lambda b, t: (b * nt + t, 0)),
        out_shape=jax.ShapeDtypeStruct((n, gw), f32),
        scratch_shapes=[
            pltpu.VMEM((tb + CONV_PAD, 3 * gw), f32),
            pltpu.VMEM((GDN_PAIRS, LANES, LANES), f32),
            stacked(bf16), stacked(bf16), stacked(bf16), stacked(bf16),
            stacked(bf16, 2 * gw),
            stacked(bf16), stacked(bf16), stacked(bf16),
            stacked(f32), stacked(f32), stacked(f32),
            pltpu.VMEM((tb, gw), f32)],
        compiler_params=_params("parallel", "arbitrary"),
        name="gdn_mixer",
    )(proj, proj, proj, conv_w, aneg, bias, gnorm, eeven, eodd, ebeta, bd, ltri)


SWA_SPAN = 128
SWA_PREP_ROWS = 512
SWA_GROUP = 4
MASKED_SCORE = -1e30


def _swa_kernel(q_ref, k_ref, v_ref, cos_ref, sin_ref, qn_ref, kn_ref, bd_ref, o_ref,
                q_s, k_s, v_s, op_s, lse_s):
    f32, bf16 = jnp.float32, jnp.bfloat16
    s_len = q_ref.shape[0]
    blk = SWA_SPAN
    lane = lax.broadcasted_iota(jnp.int32, (1, LANES), 1)
    first_half = (lane % HEAD_DIM) < (HEAD_DIM // 2)
    head0 = lane < HEAD_DIM
    bd = bd_ref[...]
    prep_rows = min(SWA_PREP_ROWS, s_len)

    def normed_rotary(x, gain, cos, sin, scale):
        ms = _group_sum_sq(x, bd) * (1.0 / HEAD_DIM)
        y = x * lax.rsqrt(ms + RMS_EPS) * gain
        partner = jnp.where(first_half, pltpu.roll(y, LANES - HEAD_DIM // 2, 1),
                            pltpu.roll(y, HEAD_DIM // 2, 1))
        return (y * cos + partner * sin) * scale

    def prologue(i, carry):
        rows = pl.ds(pl.multiple_of(i * prep_rows, prep_rows), prep_rows)
        cos, sin = cos_ref[rows, :], sin_ref[rows, :]
        q_s[rows, :] = normed_rotary(q_ref[rows, :], qn_ref[...], cos, sin, HEAD_DIM ** -0.5)
        k_s[rows, :] = normed_rotary(k_ref[rows, :], kn_ref[...], cos, sin, 1.0)
        v_s[rows, :] = v_ref[rows, :]
        return carry

    lax.fori_loop(0, s_len // prep_rows, prologue, 0)

    qi = lax.broadcasted_iota(jnp.int32, (blk, 2 * blk), 0)
    kj = lax.broadcasted_iota(jnp.int32, (blk, 2 * blk), 1)
    dist = blk + qi - kj
    band = (dist >= 0) & (dist <= SWA_SPAN)
    ones_kv = jnp.ones((2 * blk, LANES), bf16)

    for pidx, (window, dil) in enumerate(DILATED_PAIRS):
        n_blk = s_len // (dil * blk)

        def rows_of(r, i):
            start = r + i * (blk * dil)
            if dil == 1:
                return pl.ds(start, blk)
            return pl.ds(start, blk, stride=dil)

        def block_group(gi, carry):
            units = [gi * SWA_GROUP + j for j in range(SWA_GROUP)]
            ri = [(u // n_blk, u % n_blk) for u in units]
            cur = [rows_of(r, i) for r, i in ri]
            prev = [rows_of(r, jnp.maximum(i - 1, 0)) for r, i in ri]
            qb = [q_s[c, :] for c in cur]
            kw = [jnp.concatenate([k_s[pv, :], k_s[c, :]], axis=0).astype(bf16)
                  for pv, c in zip(prev, cur)]
            vw = [jnp.concatenate([jnp.concatenate([v_s[pv, :], v_s[c, :]], axis=0).astype(bf16),
                                   ones_kv], axis=1) for pv, c in zip(prev, cur)]
            valid = [band & (kj >= jnp.where(i > 0, 0, blk)) for _, i in ri]
            chains = [(j, h) for j in range(SWA_GROUP) for h in range(2)]
            qh = [jnp.where(head0 if h == 0 else ~head0, qb[j], 0.0).astype(bf16) for j, h in chains]
            s = [jnp.where(valid[j], _dot_nt(x, kw[j]), MASKED_SCORE) for (j, h), x in zip(chains, qh)]
            m = [jnp.max(x, axis=-1, keepdims=True) for x in s]
            p = [jnp.exp((x - mx).astype(bf16)) for x, mx in zip(s, m)]
            pv = [_dot(x, vw[j]) for (j, h), x in zip(chains, p)]
            o = [x[:, :LANES] / x[:, LANES:] for x in pv]
            lse = [mx + jnp.log(x[:, LANES:]) for mx, x in zip(m, pv)]
            for j in range(SWA_GROUP):
                op_s[pidx, cur[j], :] = jnp.where(head0, o[2 * j], o[2 * j + 1])
                lse_s[pidx, cur[j], :] = jnp.where(head0, lse[2 * j], lse[2 * j + 1])
            return carry

        lax.fori_loop(0, s_len // (blk * SWA_GROUP), block_group, 0)

    def epilogue(i, carry):
        rows = pl.ds(pl.multiple_of(i * prep_rows, prep_rows), prep_rows)
        l0, l1, l2 = lse_s[0, rows, :], lse_s[1, rows, :], lse_s[2, rows, :]
        m = jnp.maximum(jnp.maximum(l0, l1), l2)
        w0, w1, w2 = jnp.exp(l0 - m), jnp.exp(l1 - m), jnp.exp(l2 - m)
        mix = w0 * op_s[0, rows, :] + w1 * op_s[1, rows, :] + w2 * op_s[2, rows, :]
        o_ref[rows, :] = mix / (w0 + w1 + w2)
        return carry

    lax.fori_loop(0, s_len // prep_rows, epilogue, 0)


def swa_mixer(proj, bsz, s_len, q_norm, k_norm):
    assert all(w // d == SWA_SPAN for w, d in DILATED_PAIRS) and len(DILATED_PAIRS) == 3
    assert s_len % (SWA_SPAN * max(d for _, d in DILATED_PAIRS)) == 0
    n = bsz * s_len
    f32 = jnp.float32
    n_pairs = SWA_WIDTH // LANES
    half = HEAD_DIM // 2
    inv_freq = ROPE_THETA ** (-jnp.arange(0, HEAD_DIM, 2, dtype=f32) / HEAD_DIM)
    ang = jnp.arange(s_len, dtype=f32)[:, None] * inv_freq[None, :]
    cos = jnp.tile(jnp.cos(ang), (1, LANES // half))
    sin = jnp.tile(jnp.concatenate([-jnp.sin(ang), jnp.sin(ang)], axis=1), (1, LANES // HEAD_DIM))
    gain = lambda g: jnp.tile(g, LANES // HEAD_DIM).reshape(1, LANES)
    lane_head = np.arange(LANES) // HEAD_DIM
    bd = jnp.asarray(lane_head[:, None] == lane_head[None, :], jnp.bfloat16)
    q_col0 = 4 * GDN_WIDTH // LANES
    col = lambda which: pl.BlockSpec(
        (s_len, LANES), lambda b, p: (b, q_col0 + which * n_pairs + p))
    seq = pltpu.VMEM((s_len, LANES), f32)
    per_pattern = pltpu.VMEM((len(DILATED_PAIRS), s_len, LANES), f32)
    return pl.pallas_call(
        _swa_kernel,
        grid=(bsz, n_pairs),
        in_specs=[col(0), col(1), col(2), _resident((s_len, LANES)), _resident((s_len, LANES)),
                  _resident((1, LANES)), _resident((1, LANES)), _resident((LANES, LANES))],
        out_specs=pl.BlockSpec((s_len, LANES), lambda b, p: (b, p)),
        out_shape=jax.ShapeDtypeStruct((n, SWA_WIDTH), f32),
        scratch_shapes=[seq, seq, seq, per_pattern, per_pattern],
        compiler_params=_params("parallel", "parallel"),
        name="swa_mixer",
    )(proj, proj, proj, cos, sin, gain(q_norm), gain(k_norm), bd)


def kernel(x, norm_mix, w_in, conv_w, a_log, dt_bias, gdn_norm, q_norm, k_norm, pool_w,
           pool_scale, w_out, norm_ffn, ffn_gate, ffn_up, ffn_down, router_w, router_b,
           exp_gate, exp_up, exp_down):
    bsz, s_len, d = x.shape
    n = bsz * s_len
    depth = w_in.shape[0]
    x2 = x.reshape(n, d)
    for layer in range(depth):
        proj = norm_inproj(x2, norm_mix[layer], _arrange_w_in(w_in[layer]))
        ya = gdn_mixer(proj, bsz, s_len, conv_w[layer], a_log[layer], dt_bias[layer],
                       gdn_norm[layer])
        yb = swa_mixer(proj, bsz, s_len, q_norm[layer], k_norm[layer])
        x2 = pool_outproj_residual(x2, ya, yb, proj, bsz, s_len, pool_w[layer],
                                   pool_scale[layer], w_out[layer])
        i = layer // 2
        if layer % 2 == 0:
            x2 = ffn_dense(x2, norm_ffn[layer], ffn_gate[i], ffn_up[i], ffn_down[i])
        else:
            x2 = moe_block(x2, norm_ffn[layer], router_w[i], router_b[i], exp_gate[i],
                           exp_up[i], exp_down[i])
    return x2.reshape(bsz, s_len, d)
```

```python
import functools

import jax
import jax.numpy as jnp
import numpy as np
from jax import lax
from jax.experimental import pallas as pl
from jax.experimental.pallas import tpu as pltpu

D_MODEL = 1024
HEAD_DIM = 64
MIX_WIDTH = D_MODEL
POOL_WIDTH = MIX_WIDTH // 4
POOL_GROUPS = 4
POOL_GROUP_DIM = POOL_WIDTH // POOL_GROUPS
POOL_SIZES = (2, 4, 8, 16)
GDN_HEADS = (MIX_WIDTH - POOL_WIDTH) // (2 * HEAD_DIM)
GDN_WIDTH = GDN_HEADS * HEAD_DIM
SWA_HEADS = (MIX_WIDTH - POOL_WIDTH - GDN_WIDTH) // HEAD_DIM
SWA_WIDTH = SWA_HEADS * HEAD_DIM
CONV_K = 4
GDN_CHUNK = 64
DILATED_PAIRS = ((128, 1), (512, 4), (2048, 16))
ROPE_THETA = 10000.0
RMS_EPS = 1e-6
N_EXPERTS = 8
TOP_K = 2

VMEM_LIMIT_BYTES = 56 * 1024 * 1024
LANES = 128

SMALL_OFF = 7 * GDN_WIDTH
POOL_OFF = SMALL_OFF + LANES
PROJ_WIDTH = POOL_OFF + POOL_WIDTH

_HI = lax.Precision.HIGHEST


def _row_tile(n_rows, want):
    t = min(want, n_rows)
    while n_rows % t:
        t //= 2
    return t


def _params(*sem):
    return pltpu.CompilerParams(dimension_semantics=sem, vmem_limit_bytes=VMEM_LIMIT_BYTES)


def _resident(shape):
    zeros = (0,) * len(shape)
    return pl.BlockSpec(shape, lambda *_: zeros, pipeline_mode=pl.Buffered(1))


def _dot(a, b, precision=None):
    return jnp.dot(a, b, preferred_element_type=jnp.float32, precision=precision)


def _dot_nt(a, b):
    return lax.dot_general(a, b, (((1,), (1,)), ((), ())), preferred_element_type=jnp.float32)


def _bf16_pieces(x, terms):
    pieces, rest = [], x
    for t in range(terms):
        pieces.append(rest.astype(jnp.bfloat16))
        if t + 1 < terms:
            rest = rest - pieces[-1].astype(jnp.float32)
    return pieces


def _select_dot(x, sel_stacked):
    terms = sel_stacked.shape[0] // x.shape[1]
    return _dot(jnp.concatenate(_bf16_pieces(x, terms), axis=1), sel_stacked)


def _prefix_dot(sel, x, terms):
    width = x.shape[1]
    wide = _dot(sel, jnp.concatenate(_bf16_pieces(x, terms), axis=1))
    return sum(wide[:, t * width:(t + 1) * width] for t in range(terms))


def _group_sum_sq(x, group_ones):
    return _dot((x * x).astype(jnp.bfloat16), group_ones)


def _norm_inproj_kernel(x_ref, g_ref, w_ref, o_ref):
    x = x_ref[...]
    ms = jnp.mean(x * x, axis=-1, keepdims=True)
    h = (x * lax.rsqrt(ms + RMS_EPS) * g_ref[...]).astype(jnp.bfloat16)
    o_ref[...] = _dot(h, w_ref[...])


def norm_inproj(x2, g, w):
    n, d = x2.shape
    width = w.shape[1]
    tm = _row_tile(n, 512)
    return pl.pallas_call(
        _norm_inproj_kernel,
        grid=(n // tm,),
        in_specs=[pl.BlockSpec((tm, d), lambda i: (i, 0)),
                  _resident((1, d)),
                  _resident((d, width))],
        out_specs=pl.BlockSpec((tm, width), lambda i: (i, 0)),
        out_shape=jax.ShapeDtypeStruct((n, width), jnp.float32),
        compiler_params=_params("parallel"),
        name="norm_inproj",
    )(x2, g.reshape(1, d), w)


def _arrange_w_in(w):
    gw, gh, sw = GDN_WIDTH, GDN_HEADS, SWA_WIDTH
    a_main = w[:, :4 * gw]
    small = w[:, 4 * gw:4 * gw + 2 * gh]
    b_main = w[:, 4 * gw + 2 * gh:4 * gw + 2 * gh + 3 * sw]
    cu = w[:, 4 * gw + 2 * gh + 3 * sw:]
    small = jnp.pad(small, ((0, 0), (0, LANES - 2 * gh)))
    return jnp.concatenate([a_main, b_main, small, cu], axis=1).astype(jnp.bfloat16)


POOL_HIST = max(POOL_SIZES)


def _pool_outproj_kernel(x_ref, ya_ref, yb_ref, cu_ref, size_ref, pw_ref, ps_ref,
                         wa_ref, wb_ref, wc_ref, o_ref, ext_s):
    f32, bf16 = jnp.float32, jnp.bfloat16
    tm = x_ref.shape[0]
    t_blk = pl.program_id(1)

    @pl.when(t_blk == 0)
    def _():
        ext_s[0:POOL_HIST, :] = jnp.zeros((POOL_HIST, POOL_WIDTH), f32)

    cu = cu_ref[...]
    ext_s[POOL_HIST:POOL_HIST + tm, :] = cu
    ext = ext_s[...]
    ext_s[0:POOL_HIST, :] = ext_s[tm:tm + POOL_HIST, :]
    size = size_ref[...]
    win = ext
    total = jnp.zeros_like(ext)
    shift = 1
    while shift < POOL_HIST:
        win = win + pltpu.roll(win, shift, 0)
        shift *= 2
        total = jnp.where(size == shift, win, total)
    total = total[POOL_HIST:, :]
    pos = t_blk * tm + lax.broadcasted_iota(jnp.int32, (tm, 1), 0)
    count = jnp.minimum(pos + 1, size).astype(f32)
    pooled = total / count - cu
    yc = _dot(pooled.astype(bf16), pw_ref[...]) * ps_ref[...]

    acc = x_ref[...]
    acc += _dot(ya_ref[...].astype(bf16), wa_ref[...])
    acc += _dot(yb_ref[...].astype(bf16), wb_ref[...])
    acc += _dot(yc.astype(bf16), wc_ref[...])
    o_ref[...] = acc


def pool_outproj_residual(x2, ya, yb, proj, bsz, s_len, pool_w, pool_scale, w_out):
    n, d = x2.shape
    tm = _row_tile(s_len, 1024)
    nt = s_len // tm
    bf16 = jnp.bfloat16
    wa = w_out[:GDN_WIDTH].astype(bf16)
    wb = w_out[GDN_WIDTH:GDN_WIDTH + SWA_WIDTH].astype(bf16)
    wc = w_out[GDN_WIDTH + SWA_WIDTH:].astype(bf16)
    size = jnp.asarray(np.repeat(np.asarray(POOL_SIZES, np.int32), POOL_GROUP_DIM).reshape(1, POOL_WIDTH))
    pw = jax.scipy.linalg.block_diag(*[pool_w[g] for g in range(POOL_GROUPS)]).astype(bf16)
    row = lambda width, cb=0: pl.BlockSpec((tm, width), lambda b, t: (b * nt + t, cb))
    return pl.pallas_call(
        _pool_outproj_kernel,
        grid=(bsz, nt),
        in_specs=[row(d), row(GDN_WIDTH), row(SWA_WIDTH), row(POOL_WIDTH, POOL_OFF // POOL_WIDTH),
                  _resident((1, POOL_WIDTH)), _resident(pw.shape), _resident((1, POOL_WIDTH)),
                  _resident(wa.shape), _resident(wb.shape), _resident(wc.shape)],
        out_specs=row(d),
        out_shape=jax.ShapeDtypeStruct((n, d), jnp.float32),
        scratch_shapes=[pltpu.VMEM((tm + POOL_HIST, POOL_WIDTH), jnp.float32)],
        compiler_params=_params("parallel", "arbitrary"),
        name="pool_outproj_residual",
    )(x2, ya, yb, proj, size, pw, pool_scale.reshape(1, POOL_WIDTH), wa, wb, wc)


MXU_DIM = 256
HIDDEN_SLICE = 512


def _hidden_slices(width):
    assert width % MXU_DIM == 0
    return [(s, min(HIDDEN_SLICE, width - s)) for s in range(0, width, HIDDEN_SLICE)]


def _swiglu_slice(h, wg, wu, wd):
    gate = _dot(h, wg)
    up = _dot(h, wu)
    return _dot((gate * jax.nn.sigmoid(gate) * up).astype(jnp.bfloat16), wd)


def _ffn_kernel(x_ref, g_ref, wg_ref, wu_ref, wd_ref, o_ref):
    x = x_ref[...]
    ms = jnp.mean(x * x, axis=-1, keepdims=True)
    h = (x * lax.rsqrt(ms + RMS_EPS) * g_ref[...]).astype(jnp.bfloat16)
    o_ref[...] = x
    for s, w in _hidden_slices(wg_ref.shape[1]):
        o_ref[...] += _swiglu_slice(h, wg_ref[:, s:s + w], wu_ref[:, s:s + w], wd_ref[s:s + w, :])


def ffn_dense(x2, g, w_gate, w_up, w_down):
    n, d = x2.shape
    d_ff = w_gate.shape[1]
    tm = _row_tile(n, 512)
    row = pl.BlockSpec((tm, d), lambda i: (i, 0))
    return pl.pallas_call(
        _ffn_kernel,
        grid=(n // tm,),
        in_specs=[row, _resident((1, d)), _resident((d, d_ff)), _resident((d, d_ff)),
                  _resident((d_ff, d))],
        out_specs=row,
        out_shape=jax.ShapeDtypeStruct((n, d), jnp.float32),
        compiler_params=_params("parallel"),
        name="ffn_dense",
    )(x2, g.reshape(1, d), w_gate.astype(jnp.bfloat16), w_up.astype(jnp.bfloat16),
      w_down.astype(jnp.bfloat16))


NO_EXPERT = -1e30
ROUTE_EXPERT, ROUTE_GATE, ROUTE_RANK = 0, TOP_K, 2 * TOP_K


def _router_kernel(x_ref, g_ref, wr_ref, br_ref, ltri_ref, h_ref, route_ref, count_ref, count_s):
    f32 = jnp.float32
    tm = x_ref.shape[0]

    @pl.when(pl.program_id(0) == 0)
    def _():
        count_s[...] = jnp.zeros_like(count_s)

    x = x_ref[...]
    ms = jnp.mean(x * x, axis=-1, keepdims=True)
    h = x * lax.rsqrt(ms + RMS_EPS) * g_ref[...]
    h_ref[...] = h.astype(jnp.bfloat16)
    logits = _dot(h, wr_ref[...], _HI) + br_ref[...]
    lane = lax.broadcasted_iota(jnp.int32, (tm, LANES), 1)

    def top(vals):
        best = jnp.max(vals, axis=-1, keepdims=True)
        return best, jnp.min(jnp.where(vals == best, lane, LANES), axis=-1, keepdims=True)

    m1, e1 = top(logits)
    m2, e2 = top(jnp.where(lane == e1, 2 * NO_EXPERT, logits))
    t = jnp.exp(m2 - m1)
    gate1 = 1.0 / (1.0 + t)
    gate2 = t / (1.0 + t)
    sel1, sel2 = lane == e1, lane == e2
    chosen = (sel1 | sel2).astype(f32)
    before = _dot(ltri_ref[...], chosen.astype(jnp.bfloat16)) + count_s[...]
    rank1 = jnp.sum(jnp.where(sel1, before, 0.0), axis=-1, keepdims=True)
    rank2 = jnp.sum(jnp.where(sel2, before, 0.0), axis=-1, keepdims=True)
    count_s[...] += jnp.sum(chosen, axis=0, keepdims=True)
    count_ref[...] = count_s[...]
    record = jnp.zeros((tm, LANES), f32)
    for pos, val in ((ROUTE_EXPERT, e1.astype(f32)), (ROUTE_EXPERT + 1, e2.astype(f32)),
                     (ROUTE_GATE, gate1), (ROUTE_GATE + 1, gate2),
                     (ROUTE_RANK, rank1), (ROUTE_RANK + 1, rank2)):
        record = jnp.where(lane == pos, val, record)
    route_ref[...] = record


def norm_router(x2, part, n_parts, g, router_w, router_b):
    d = x2.shape[1]
    n = x2.shape[0] // n_parts
    tm = _row_tile(n, 1024)
    first = part * (n // tm)
    wr = jnp.pad(router_w, ((0, 0), (0, LANES - N_EXPERTS)))
    br = jnp.pad(router_b, (0, LANES - N_EXPERTS), constant_values=NO_EXPERT).reshape(1, LANES)
    ti = np.arange(tm)
    ltri = jnp.asarray(ti[None, :] < ti[:, None], jnp.bfloat16)
    return pl.pallas_call(
        _router_kernel,
        grid=(n // tm,),
        in_specs=[pl.BlockSpec((tm, d), lambda i: (first + i, 0)), _resident((1, d)),
                  _resident((d, LANES)), _resident((1, LANES)), _resident((tm, tm))],
        out_specs=[pl.BlockSpec((tm, d), lambda i: (i, 0)),
                   pl.BlockSpec((tm, LANES), lambda i: (i, 0)),
                   pl.BlockSpec((1, LANES), lambda i: (0, 0))],
        out_shape=[jax.ShapeDtypeStruct((n, d), jnp.bfloat16),
                   jax.ShapeDtypeStruct((n, LANES), jnp.float32),
                   jax.ShapeDtypeStruct((1, LANES), jnp.float32)],
        scratch_shapes=[pltpu.VMEM((1, LANES), jnp.float32)],
        compiler_params=_params("arbitrary"),
        name="norm_router",
    )(x2, g.reshape(1, d), wr, br, ltri)


def _moe_kernel(be_ref, nb_ref, xs_ref, wg_ref, wu_ref, wd_ref, o_ref, acc_s):
    j = pl.program_id(0)
    c = pl.program_id(1)
    last = pl.num_programs(1) - 1

    @pl.when(j < nb_ref[0])
    def _():
        h = xs_ref[...]

        @pl.when(c == 0)
        def _():
            acc_s[...] = jnp.zeros_like(acc_s)

        for s, w in _hidden_slices(wg_ref.shape[2]):
            acc_s[...] += _swiglu_slice(h, wg_ref[0, :, s:s + w], wu_ref[0, :, s:s + w],
                                        wd_ref[0, s:s + w, :])

        @pl.when(c == last)
        def _():
            o_ref[...] = acc_s[...].astype(o_ref.dtype)

    @pl.when(jnp.logical_and(j >= nb_ref[0], c == last))
    def _():
        o_ref[...] = jnp.zeros_like(o_ref)


def moe_experts(xs, block_expert, n_used, w_gate, w_up, w_down, tm, tc):
    n_slots, d = xs.shape
    d_e = w_gate.shape[2]
    n_blocks = n_slots // tm
    n_c = d_e // tc

    def live(j, nb):
        return jnp.minimum(j, nb[0] - 1)

    def chunk(j, c, nb):
        return jnp.where(j < nb[0], c, n_c - 1)

    return pl.pallas_call(
        _moe_kernel,
        grid_spec=pltpu.PrefetchScalarGridSpec(
            num_scalar_prefetch=2,
            grid=(n_blocks, n_c),
            in_specs=[
                pl.BlockSpec((tm, d), lambda j, c, be, nb: (live(j, nb), 0)),
                pl.BlockSpec((1, d, tc), lambda j, c, be, nb: (be[live(j, nb)], 0, chunk(j, c, nb))),
                pl.BlockSpec((1, d, tc), lambda j, c, be, nb: (be[live(j, nb)], 0, chunk(j, c, nb))),
                pl.BlockSpec((1, tc, d), lambda j, c, be, nb: (be[live(j, nb)], chunk(j, c, nb), 0)),
            ],
            out_specs=pl.BlockSpec((tm, d), lambda j, c, be, nb: (j, 0)),
            scratch_shapes=[pltpu.VMEM((tm, d), jnp.float32)],
        ),
        out_shape=jax.ShapeDtypeStruct((n_slots, d), jnp.bfloat16),
        compiler_params=_params("arbitrary", "arbitrary"),
        name="moe_experts",
    )(block_expert, n_used, xs, w_gate, w_up, w_down)


MOE_PARTS = 2


def moe_block(x2, g, router_w, router_b, w_gate, w_up, w_down):
    n_parts = MOE_PARTS if x2.shape[0] % (MOE_PARTS * 1024) == 0 else 1
    rows = x2.shape[0] // n_parts
    weights = tuple(w.astype(jnp.bfloat16) for w in (w_gate, w_up, w_down))
    outs = [_moe_part(x2, part, n_parts, g, router_w, router_b, weights) for part in range(n_parts)]
    return jnp.concatenate(outs, axis=0) if n_parts > 1 else outs[0]


def _moe_part(x2, part, n_parts, g, router_w, router_b, weights):
    d = x2.shape[1]
    n = x2.shape[0] // n_parts
    tm = _row_tile(n, 1024)
    h, route, count = norm_router(x2, part, n_parts, g, router_w, router_b)
    expert = route[:, ROUTE_EXPERT:ROUTE_EXPERT + TOP_K].astype(jnp.int32)
    gates = route[:, ROUTE_GATE:ROUTE_GATE + TOP_K]
    rank = route[:, ROUTE_RANK:ROUTE_RANK + TOP_K].astype(jnp.int32)
    counts = count[0, :N_EXPERTS].astype(jnp.int32)
    padded = (counts + tm - 1) // tm * tm
    padded_end = jnp.cumsum(padded)
    padded_start = padded_end - padded
    slot = jnp.take(padded_start, expert) + rank
    n_blocks = -(-(n * TOP_K) // tm) + N_EXPERTS
    n_slots = n_blocks * tm
    token = jnp.broadcast_to(jnp.arange(n, dtype=jnp.int32)[:, None], (n, TOP_K))
    slot_token = jnp.zeros((n_slots,), jnp.int32).at[slot.reshape(-1)].set(token.reshape(-1))
    block_start = jnp.arange(n_blocks) * tm
    block_expert = jnp.minimum(
        jnp.sum(block_start[:, None] >= padded_end[None, :], axis=1), N_EXPERTS - 1).astype(jnp.int32)
    n_used = (padded_end[-1] // tm).astype(jnp.int32).reshape(1)
    xs = jnp.take(h, slot_token, axis=0)
    d_e = weights[0].shape[2]
    tc = d_e // 2 if d_e % (2 * MXU_DIM) == 0 else d_e
    y_slots = moe_experts(xs, block_expert, n_used, *weights, tm, tc)
    out = lax.slice_in_dim(x2, part * n, (part + 1) * n, axis=0)
    for j in range(TOP_K):
        out = out + gates[:, j:j + 1] * jnp.take(y_slots, slot[:, j], axis=0).astype(jnp.float32)
    return out


GDN_BLOCK = 512
GDN_GROUP = 2
CONV_PAD = 8
GDN_PAIRS = GDN_HEADS // 2
SELECT_TERMS = 3
PAIR_ROWS = 2 * GDN_CHUNK


def _gdn_kernel(aqkv_ref, az_ref, small_ref, convw_ref, aneg_ref, bias_ref, gnorm_ref,
                eeven_ref, eodd_ref, ebeta_ref, bd_ref, ltri_ref, o_ref,
                ext_s, state_s, q_s, k_s, kb_s, qd_s, rhs_s, attn_s, w_s, kdt_s,
                gc_s, kd_s, u_s, oh_s):
    f32, bf16 = jnp.float32, jnp.bfloat16
    c = GDN_CHUNK
    tb = aqkv_ref.shape[0]
    gw = GDN_WIDTH
    nc = tb // c

    @pl.when(pl.program_id(1) == 0)
    def _():
        ext_s[0:CONV_PAD, :] = jnp.zeros((CONV_PAD, 3 * gw), f32)
        state_s[...] = jnp.zeros_like(state_s)

    ext_s[CONV_PAD:CONV_PAD + tb, :] = aqkv_ref[...]
    conv = convw_ref[0:1, :] * ext_s[pl.ds(CONV_PAD - CONV_K + 1, tb), :]
    for j in range(1, CONV_K):
        conv += convw_ref[j:j + 1, :] * ext_s[pl.ds(CONV_PAD - CONV_K + 1 + j, tb), :]
    ext_s[0:CONV_PAD, :] = ext_s[tb:tb + CONV_PAD, :]
    qkv = conv * jax.nn.sigmoid(conv)
    q, k, v = qkv[:, :gw], qkv[:, gw:2 * gw], qkv[:, 2 * gw:]

    bd = bd_ref[...]
    q = q * lax.rsqrt(_group_sum_sq(q, bd) + RMS_EPS) * (HEAD_DIM ** -0.5)
    k = k * lax.rsqrt(_group_sum_sq(k, bd) + RMS_EPS)

    sm = small_ref[...]
    pre = sm + bias_ref[...]
    softplus = jnp.maximum(pre, 0.0) + jnp.log(1.0 + jnp.exp(-jnp.abs(pre)))
    g = _prefix_dot(ltri_ref[...], aneg_ref[...] * softplus, SELECT_TERMS)
    g_even = _select_dot(g, eeven_ref[...])
    g_odd = _select_dot(g, eodd_ref[...])
    bb = _select_dot(jax.nn.sigmoid(sm), ebeta_ref[...])
    lane = lax.broadcasted_iota(jnp.int32, (1, gw), 1)
    even_lane = (lane // HEAD_DIM) % 2 == 0
    gb = jnp.where(even_lane, g_even, g_odd)
    gb3 = gb.reshape(nc, c, gw)
    k_dec = (k.reshape(nc, c, gw) * jnp.exp(gb3[:, c - 1:c, :] - gb3)).reshape(tb, gw)
    eg = jnp.exp(gb)
    kb = k * bb

    def stack(x):
        x3 = x.reshape(nc, c, gw)
        zero = jnp.zeros_like(x3)
        return jnp.concatenate([jnp.where(even_lane, x3, zero), jnp.where(even_lane, zero, x3)], axis=1)

    q_s[...] = stack(q.astype(bf16))
    k_s[...] = stack(k.astype(bf16))
    kb_s[...] = stack(kb.astype(bf16))
    qd_s[...] = stack((q * eg).astype(bf16))
    kd_s[...] = stack(k_dec)
    vb_st = stack((v * bb).astype(bf16))
    kbe_st = stack((kb * eg).astype(bf16))
    for p in range(GDN_PAIRS):
        ps = slice(p * LANES, (p + 1) * LANES)
        rhs_s[:, :, 2 * p * LANES:(2 * p + 1) * LANES] = vb_st[:, :, ps]
        rhs_s[:, :, (2 * p + 1) * LANES:(2 * p + 2) * LANES] = kbe_st[:, :, ps]
    gc_s[...] = jnp.concatenate([g_even.reshape(nc, c, gw), g_odd.reshape(nc, c, gw)], axis=1)

    row = lax.broadcasted_iota(jnp.int32, (PAIR_ROWS, PAIR_ROWS), 0)
    col = lax.broadcasted_iota(jnp.int32, (PAIR_ROWS, PAIR_ROWS), 1)
    causal = (row // c == col // c) & (row >= col)
    strict = causal & (row > col)
    eye = (row == col).astype(f32)

    def phase_a(gi, carry):
        units = [(gi * GDN_GROUP + ci, p) for ci in range(GDN_GROUP) for p in range(GDN_PAIRS)]
        at = lambda ref, n, p: ref[n, :, p * LANES:(p + 1) * LANES]
        k_u = [at(k_s, n, p) for n, p in units]
        kk = [_dot_nt(at(kb_s, n, p), kx) for (n, p), kx in zip(units, k_u)]
        qk = [_dot_nt(at(q_s, n, p), kx) for (n, p), kx in zip(units, k_u)]
        gc = [at(gc_s, n, p) for n, p in units]
        dm = [jnp.where(causal, jnp.exp(jnp.minimum(gx - gx.T, 0.0)), 0.0) for gx in gc]
        a = [jnp.where(strict, x * d, 0.0) for x, d in zip(kk, dm)]
        for (n, p), x, d in zip(units, qk, dm):
            attn_s[n, :, p * LANES:(p + 1) * LANES] = (x * d).astype(bf16)
        for (n, p) in units:
            kdt_s[n, :, p * LANES:(p + 1) * LANES] = at(kd_s, n, p).T.astype(bf16)
        ab = [x.astype(bf16) for x in a]
        sq = [_dot(x, x) for x in ab]
        inv = [eye - x for x in a]
        for _ in range(4):
            sqb = [x.astype(bf16) for x in sq]
            both = [_dot(x, jnp.concatenate([x, t.astype(bf16)], axis=1)) for x, t in zip(sqb, inv)]
            sq = [x[:, :PAIR_ROWS] for x in both]
            inv = [t + x[:, PAIR_ROWS:] for t, x in zip(inv, both)]
        inv = [t + _dot(x.astype(bf16), t.astype(bf16)) for t, x in zip(inv, sq)]
        for (n, p), t in zip(units, inv):
            uw = _dot(t.astype(bf16), rhs_s[n, :, 2 * p * LANES:(2 * p + 2) * LANES])
            u_s[n, :, p * LANES:(p + 1) * LANES] = uw[:, :LANES]
            w_s[n, :, p * LANES:(p + 1) * LANES] = uw[:, LANES:].astype(bf16)
        return carry

    lax.fori_loop(0, nc // GDN_GROUP, phase_a, 0)

    def phase_b(n, carry):
        pairs = range(GDN_PAIRS)
        at = lambda ref, p: ref[n, :, p * LANES:(p + 1) * LANES]
        s = [state_s[p] for p in pairs]
        sb = [x.astype(bf16) for x in s]
        both = [_dot(jnp.concatenate([at(w_s, p), at(qd_s, p)], axis=0), sb[p]) for p in pairs]
        vn = [(at(u_s, p) - both[p][:PAIR_ROWS]).astype(bf16) for p in pairs]
        o_st = [both[p][PAIR_ROWS:] + _dot(at(attn_s, p), vn[p]) for p in pairs]
        upd = [_dot(at(kdt_s, p), vn[p]) for p in pairs]
        for p in pairs:
            ps = slice(p * LANES, (p + 1) * LANES)
            g_last = jnp.where(even_lane[:, :LANES], gc_s[n, c - 1:c, ps], gc_s[n, 2 * c - 1:2 * c, ps])
            state_s[p] = s[p] * jnp.exp(g_last) + upd[p]
            oh_s[pl.ds(pl.multiple_of(n * c, c), c), p * LANES:(p + 1) * LANES] = (
                o_st[p][:c] + o_st[p][c:])
        return carry

    lax.fori_loop(0, nc, phase_b, 0)

    o = oh_s[...]
    ms = _group_sum_sq(o, bd) * (1.0 / HEAD_DIM)
    z = az_ref[...]
    o_ref[...] = o * lax.rsqrt(ms + RMS_EPS) * gnorm_ref[...] * (z * jax.nn.sigmoid(z))


def gdn_mixer(proj, bsz, s_len, conv_w, a_log, dt_bias, gdn_norm):
    n = bsz * s_len
    gw, nh, c = GDN_WIDTH, GDN_HEADS, GDN_CHUNK
    tb = _row_tile(s_len, GDN_BLOCK)
    nt = s_len // tb
    nc = tb // c
    assert nc % GDN_GROUP == 0 and nh % 2 == 0
    f32, bf16 = jnp.float32, jnp.bfloat16
    aneg = jnp.zeros((1, LANES), f32).at[0, :nh].set(-jnp.exp(a_log))
    bias = jnp.zeros((1, LANES), f32).at[0, :nh].set(dt_bias)
    gnorm = jnp.tile(gdn_norm, nh).reshape(1, gw)
    lane_head = np.arange(gw) // HEAD_DIM
    src = np.arange(LANES)[:, None]
    stacked_sel = lambda m: jnp.asarray(np.tile(m, (SELECT_TERMS, 1)), bf16)
    eeven = stacked_sel(src == (lane_head // 2 * 2)[None, :])
    eodd = stacked_sel(src == (lane_head // 2 * 2 + 1)[None, :])
    ebeta = stacked_sel(src == nh + lane_head[None, :])
    bd = jnp.asarray(lane_head[:, None] == lane_head[None, :], bf16)
    ti = np.arange(tb)
    ltri = jnp.asarray((ti[:, None] // c == ti[None, :] // c) & (ti[None, :] <= ti[:, None]), bf16)
    rows = lambda width, cb: pl.BlockSpec((tb, width), lambda b, t: (b * nt + t, cb))
    stacked = lambda dt, width=gw: pltpu.VMEM((nc, PAIR_ROWS, width), dt)
    return pl.pallas_call(
        _gdn_kernel,
        grid=(bsz, nt),
        in_specs=[rows(3 * gw, 0), rows(gw, 3), rows(LANES, SMALL_OFF // LANES),
                  _resident((CONV_K, 3 * gw)), _resident((1, LANES)), _resident((1, LANES)),
                  _resident((1, gw)), _resident(eeven.shape), _resident(eodd.shape),
                  _resident(ebeta.shape), _resident((gw, gw)), _resident((tb, tb))],
        out_specs=pl.BlockSpec((tb, gw), lambda b, t: (b * nt + t, 0)),
        out_shape=jax.ShapeDtypeStruct((n, gw), f32),
        scratch_shapes=[
            pltpu.VMEM((tb + CONV_PAD, 3 * gw), f32),
            pltpu.VMEM((GDN_PAIRS, LANES, LANES), f32),
            stacked(bf16), stacked(bf16), stacked(bf16), stacked(bf16),
            stacked(bf16, 2 * gw),
            stacked(bf16), stacked(bf16), stacked(bf16),
            stacked(f32), stacked(f32), stacked(f32),
            pltpu.VMEM((tb, gw), f32)],
        compiler_params=_params("parallel", "arbitrary"),
        name="gdn_mixer",
    )(proj, proj, proj, conv_w, aneg, bias, gnorm, eeven, eodd, ebeta, bd, ltri)


SWA_SPAN = 128
SWA_PREP_ROWS = 512
SWA_GROUP = 4
MASKED_SCORE = -1e30


def _swa_kernel(q_ref, k_ref, v_ref, cos_ref, sin_ref, qn_ref, kn_ref, bd_ref, o_ref,
                q_s, k_s, v_s, op_s, lse_s):
    f32, bf16 = jnp.float32, jnp.bfloat16
    s_len = q_ref.shape[0]
    blk = SWA_SPAN
    lane = lax.broadcasted_iota(jnp.int32, (1, LANES), 1)
    first_half = (lane % HEAD_DIM) < (HEAD_DIM // 2)
    head0 = lane < HEAD_DIM
    bd = bd_ref[...]
    prep_rows = min(SWA_PREP_ROWS, s_len)

    def normed_rotary(x, gain, cos, sin, scale):
        ms = _group_sum_sq(x, bd) * (1.0 / HEAD_DIM)
        y = x * lax.rsqrt(ms + RMS_EPS) * gain
        partner = jnp.where(first_half, pltpu.roll(y, LANES - HEAD_DIM // 2, 1),
                            pltpu.roll(y, HEAD_DIM // 2, 1))
        return (y * cos + partner * sin) * scale

    def prologue(i, carry):
        rows = pl.ds(pl.multiple_of(i * prep_rows, prep_rows), prep_rows)
        cos, sin = cos_ref[rows, :], sin_ref[rows, :]
        q_s[rows, :] = normed_rotary(q_ref[rows, :], qn_ref[...], cos, sin, HEAD_DIM ** -0.5)
        k_s[rows, :] = normed_rotary(k_ref[rows, :], kn_ref[...], cos, sin, 1.0)
        v_s[rows, :] = v_ref[rows, :]
        return carry

    lax.fori_loop(0, s_len // prep_rows, prologue, 0)

    qi = lax.broadcasted_iota(jnp.int32, (blk, 2 * blk), 0)
    kj = lax.broadcasted_iota(jnp.int32, (blk, 2 * blk), 1)
    dist = blk + qi - kj
    band = (dist >= 0) & (dist <= SWA_SPAN)
    ones_kv = jnp.ones((2 * blk, LANES), bf16)

    for pidx, (window, dil) in enumerate(DILATED_PAIRS):
        n_blk = s_len // (dil * blk)

        def rows_of(r, i):
            start = r + i * (blk * dil)
            if dil == 1:
                return pl.ds(start, blk)
            return pl.ds(start, blk, stride=dil)

        def block_group(gi, carry):
            units = [gi * SWA_GROUP + j for j in range(SWA_GROUP)]
            ri = [(u // n_blk, u % n_blk) for u in units]
            cur = [rows_of(r, i) for r, i in ri]
            prev = [rows_of(r, jnp.maximum(i - 1, 0)) for r, i in ri]
            qb = [q_s[c, :] for c in cur]
            kw = [jnp.concatenate([k_s[pv, :], k_s[c, :]], axis=0).astype(bf16)
                  for pv, c in zip(prev, cur)]
            vw = [jnp.concatenate([jnp.concatenate([v_s[pv, :], v_s[c, :]], axis=0).astype(bf16),
                                   ones_kv], axis=1) for pv, c in zip(prev, cur)]
            valid = [band & (kj >= jnp.where(i > 0, 0, blk)) for _, i in ri]
            chains = [(j, h) for j in range(SWA_GROUP) for h in range(2)]
            qh = [jnp.where(head0 if h == 0 else ~head0, qb[j], 0.0).astype(bf16) for j, h in chains]
            s = [jnp.where(valid[j], _dot_nt(x, kw[j]), MASKED_SCORE) for (j, h), x in zip(chains, qh)]
            m = [jnp.max(x, axis=-1, keepdims=True) for x in s]
            p = [jnp.exp((x - mx).astype(bf16)) for x, mx in zip(s, m)]
            pv = [_dot(x, vw[j]) for (j, h), x in zip(chains, p)]
            o = [x[:, :LANES] / x[:, LANES:] for x in pv]
            lse = [mx + jnp.log(x[:, LANES:]) for mx, x in zip(m, pv)]
            for j in range(SWA_GROUP):
                op_s[pidx, cur[j], :] = jnp.where(head0, o[2 * j], o[2 * j + 1])
                lse_s[pidx, cur[j], :] = jnp.where(head0, lse[2 * j], lse[2 * j + 1])
            return carry

        lax.fori_loop(0, s_len // (blk * SWA_GROUP), block_group, 0)

    def epilogue(i, carry):
        rows = pl.ds(pl.multiple_of(i * prep_rows, prep_rows), prep_rows)
        l0, l1, l2 = lse_s[0, rows, :], lse_s[1, rows, :], lse_s[2, rows, :]
        m = jnp.maximum(jnp.maximum(l0, l1), l2)
        w0, w1, w2 = jnp.exp(l0 - m), jnp.exp(l1 - m), jnp.exp(l2 - m)
        mix = w0 * op_s[0, rows, :] + w1 * op_s[1, rows, :] + w2 * op_s[2, rows, :]
        o_ref[rows, :] = mix / (w0 + w1 + w2)
        return carry

    lax.fori_loop(0, s_len // prep_rows, epilogue, 0)


def swa_mixer(proj, bsz, s_len, q_norm, k_norm):
    assert all(w // d == SWA_SPAN for w, d in DILATED_PAIRS) and len(DILATED_PAIRS) == 3
    assert s_len % (SWA_SPAN * max(d for _, d in DILATED_PAIRS)) == 0
    n = bsz * s_len
    f32 = jnp.float32
    n_pairs = SWA_WIDTH // LANES
    half = HEAD_DIM // 2
    inv_freq = ROPE_THETA ** (-jnp.arange(0, HEAD_DIM, 2, dtype=f32) / HEAD_DIM)
    ang = jnp.arange(s_len, dtype=f32)[:, None] * inv_freq[None, :]
    cos = jnp.tile(jnp.cos(ang), (1, LANES // half))
    sin = jnp.tile(jnp.concatenate([-jnp.sin(ang), jnp.sin(ang)], axis=1), (1, LANES // HEAD_DIM))
    gain = lambda g: jnp.tile(g, LANES // HEAD_DIM).reshape(1, LANES)
    lane_head = np.arange(LANES) // HEAD_DIM
    bd = jnp.asarray(lane_head[:, None] == lane_head[None, :], jnp.bfloat16)
    q_col0 = 4 * GDN_WIDTH // LANES
    col = lambda which: pl.BlockSpec(
        (s_len, LANES), lambda b, p: (b, q_col0 + which * n_pairs + p))
    seq = pltpu.VMEM((s_len, LANES), f32)
    per_pattern = pltpu.VMEM((len(DILATED_PAIRS), s_len, LANES), f32)
    return pl.pallas_call(
        _swa_kernel,
        grid=(bsz, n_pairs),
        in_specs=[col(0), col(1), col(2), _resident((s_len, LANES)), _resident((s_len, LANES)),
                  _resident((1, LANES)), _resident((1, LANES)), _resident((LANES, LANES))],
        out_specs=pl.BlockSpec((s_len, LANES), lambda b, p: (b, p)),
        out_shape=jax.ShapeDtypeStruct((n, SWA_WIDTH), f32),
        scratch_shapes=[seq, seq, seq, per_pattern, per_pattern],
        compiler_params=_params("parallel", "parallel"),
        name="swa_mixer",
    )(proj, proj, proj, cos, sin, gain(q_norm), gain(k_norm), bd)


def kernel(x, norm_mix, w_in, conv_w, a_log, dt_bias, gdn_norm, q_norm, k_norm, pool_w,
           pool_scale, w_out, norm_ffn, ffn_gate, ffn_up, ffn_down, router_w, router_b,
           exp_gate, exp_up, exp_down):
    bsz, s_len, d = x.shape
    n = bsz * s_len
    depth = w_in.shape[0]
    x2 = x.reshape(n, d)
    for layer in range(depth):
        proj = norm_inproj(x2, norm_mix[layer], _arrange_w_in(w_in[layer]))
        ya = gdn_mixer(proj, bsz, s_len, conv_w[layer], a_log[layer], dt_bias[layer],
                       gdn_norm[layer])
        yb = swa_mixer(proj, bsz, s_len, q_norm[layer], k_norm[layer])
        x2 = pool_outproj_residual(x2, ya, yb, proj, bsz, s_len, pool_w[layer],
                                   pool_scale[layer], w_out[layer])
        i = layer // 2
        if layer % 2 == 0:
            x2 = ffn_dense(x2, norm_ffn[layer], ffn_gate[i], ffn_up[i], ffn_down[i])
        else:
            x2 = moe_block(x2, norm_ffn[layer], router_w[i], router_b[i], exp_gate[i],
                           exp_up[i], exp_down[i])
    return x2.reshape(bsz, s_len, d)
```

```python
import functools

import jax
import jax.numpy as jnp
import numpy as np
from jax import lax
from jax.experimental import pallas as pl
from jax.experimental.pallas import tpu as pltpu

D_MODEL = 1024
HEAD_DIM = 64
MIX_WIDTH = D_MODEL
POOL_WIDTH = MIX_WIDTH // 4
POOL_GROUPS = 4
POOL_GROUP_DIM = POOL_WIDTH // POOL_GROUPS
POOL_SIZES = (2, 4, 8, 16)
GDN_HEADS = (MIX_WIDTH - POOL_WIDTH) // (2 * HEAD_DIM)
GDN_WIDTH = GDN_HEADS * HEAD_DIM
SWA_HEADS = (MIX_WIDTH - POOL_WIDTH - GDN_WIDTH) // HEAD_DIM
SWA_WIDTH = SWA_HEADS * HEAD_DIM
CONV_K = 4
GDN_CHUNK = 64
DILATED_PAIRS = ((128, 1), (512, 4), (2048, 16))
ROPE_THETA = 10000.0
RMS_EPS = 1e-6
N_EXPERTS = 8
TOP_K = 2

VMEM_LIMIT_BYTES = 56 * 1024 * 1024
LANES = 128

SMALL_OFF = 7 * GDN_WIDTH
POOL_OFF = SMALL_OFF + LANES
PROJ_WIDTH = POOL_OFF + POOL_WIDTH


def _row_tile(n_rows, want):
    t = min(want, n_rows)
    while n_rows % t:
        t //= 2
    return t


def _params(*sem):
    return pltpu.CompilerParams(dimension_semantics=sem, vmem_limit_bytes=VMEM_LIMIT_BYTES)


def _resident(shape):
    zeros = (0,) * len(shape)
    return pl.BlockSpec(shape, lambda *_: zeros, pipeline_mode=pl.Buffered(1))


def _dot(a, b, precision=None):
    return jnp.dot(a, b, preferred_element_type=jnp.float32, precision=precision)


def _dot_nt(a, b):
    return lax.dot_general(a, b, (((1,), (1,)), ((), ())), preferred_element_type=jnp.float32)


def _bf16_pieces(x, terms):
    pieces, rest = [], x
    for t in range(terms):
        pieces.append(rest.astype(jnp.bfloat16))
        if t + 1 < terms:
            rest = rest - pieces[-1].astype(jnp.float32)
    return pieces


def _select_dot(x, sel_stacked):
    terms = sel_stacked.shape[0] // x.shape[1]
    return _dot(jnp.concatenate(_bf16_pieces(x, terms), axis=1), sel_stacked)


def _prefix_dot(sel, x, terms):
    width = x.shape[1]
    wide = _dot(sel, jnp.concatenate(_bf16_pieces(x, terms), axis=1))
    return sum(wide[:, t * width:(t + 1) * width] for t in range(terms))


def _group_sum_sq(x, group_ones):
    return _dot((x * x).astype(jnp.bfloat16), group_ones)


def _norm_inproj_kernel(x_ref, g_ref, w_ref, o_ref):
    x = x_ref[...]
    ms = jnp.mean(x * x, axis=-1, keepdims=True)
    h = (x * lax.rsqrt(ms + RMS_EPS) * g_ref[...]).astype(jnp.bfloat16)
    o_ref[...] = _dot(h, w_ref[...])


def norm_inproj(x2, g, w):
    n, d = x2.shape
    width = w.shape[1]
    tm = _row_tile(n, 512)
    return pl.pallas_call(
        _norm_inproj_kernel,
        grid=(n // tm,),
        in_specs=[pl.BlockSpec((tm, d), lambda i: (i, 0)),
                  _resident((1, d)),
                  _resident((d, width))],
        out_specs=pl.BlockSpec((tm, width), lambda i: (i, 0)),
        out_shape=jax.ShapeDtypeStruct((n, width), jnp.float32),
        compiler_params=_params("parallel"),
        name="norm_inproj",
    )(x2, g.reshape(1, d), w)


def _arrange_w_in(w):
    gw, gh, sw = GDN_WIDTH, GDN_HEADS, SWA_WIDTH
    a_main = w[:, :4 * gw]
    small = w[:, 4 * gw:4 * gw + 2 * gh]
    b_main = w[:, 4 * gw + 2 * gh:4 * gw + 2 * gh + 3 * sw]
    cu = w[:, 4 * gw + 2 * gh + 3 * sw:]
    small = jnp.pad(small, ((0, 0), (0, LANES - 2 * gh)))
    return jnp.concatenate([a_main, b_main, small, cu], axis=1).astype(jnp.bfloat16)


POOL_HIST = max(POOL_SIZES)


def _pool_outproj_kernel(x_ref, ya_ref, yb_ref, cu_ref, size_ref, pw_ref, ps_ref,
                         w_ref, o_ref, ext_s):
    f32, bf16 = jnp.float32, jnp.bfloat16
    tm = x_ref.shape[0]
    t_blk = pl.program_id(1)

    @pl.when(t_blk == 0)
    def _():
        ext_s[0:POOL_HIST, :] = jnp.zeros((POOL_HIST, POOL_WIDTH), f32)

    cu = cu_ref[...]
    ext_s[POOL_HIST:POOL_HIST + tm, :] = cu
    ext = ext_s[...]
    ext_s[0:POOL_HIST, :] = ext_s[tm:tm + POOL_HIST, :]
    size = size_ref[...]
    win = ext
    total = jnp.zeros_like(ext)
    shift = 1
    while shift < POOL_HIST:
        win = win + pltpu.roll(win, shift, 0)
        shift *= 2
        total = jnp.where(size == shift, win, total)
    total = total[POOL_HIST:, :]
    pos = t_blk * tm + lax.broadcasted_iota(jnp.int32, (tm, 1), 0)
    count = jnp.minimum(pos + 1, size).astype(f32)
    pooled = total / count - cu
    yc = _dot(pooled.astype(bf16), pw_ref[...]) * ps_ref[...]

    mixed = jnp.concatenate([ya_ref[...].astype(bf16), yb_ref[...].astype(bf16), yc.astype(bf16)], axis=1)
    o_ref[...] = x_ref[...] + _dot(mixed, w_ref[...])


def pool_outproj_residual(x2, ya, yb, proj, bsz, s_len, pool_w, pool_scale, w_out):
    n, d = x2.shape
    tm = _row_tile(s_len, 1024)
    nt = s_len // tm
    bf16 = jnp.bfloat16
    size = jnp.asarray(np.repeat(np.asarray(POOL_SIZES, np.int32), POOL_GROUP_DIM).reshape(1, POOL_WIDTH))
    pw = jax.scipy.linalg.block_diag(*[pool_w[g] for g in range(POOL_GROUPS)]).astype(bf16)
    row = lambda width, cb=0: pl.BlockSpec((tm, width), lambda b, t: (b * nt + t, cb))
    return pl.pallas_call(
        _pool_outproj_kernel,
        grid=(bsz, nt),
        in_specs=[row(d), row(GDN_WIDTH), row(SWA_WIDTH), row(POOL_WIDTH, POOL_OFF // POOL_WIDTH),
                  _resident((1, POOL_WIDTH)), _resident(pw.shape), _resident((1, POOL_WIDTH)),
                  _resident(w_out.shape)],
        out_specs=row(d),
        out_shape=jax.ShapeDtypeStruct((n, d), jnp.float32),
        scratch_shapes=[pltpu.VMEM((tm + POOL_HIST, POOL_WIDTH), jnp.float32)],
        compiler_params=_params("parallel", "arbitrary"),
        name="pool_outproj_residual",
    )(x2, ya, yb, proj, size, pw, pool_scale.reshape(1, POOL_WIDTH), w_out.astype(bf16))


MXU_DIM = 256
HIDDEN_SLICE = 512


def _hidden_slices(width):
    assert width % MXU_DIM == 0
    return [(s, min(HIDDEN_SLICE, width - s)) for s in range(0, width, HIDDEN_SLICE)]


def _swiglu_slice(h, wg, wu, wd):
    gate = _dot(h, wg)
    up = _dot(h, wu)
    return _dot((gate * jax.nn.sigmoid(gate) * up).astype(jnp.bfloat16), wd)


def _ffn_kernel(x_ref, g_ref, wg_ref, wu_ref, wd_ref, o_ref):
    x = x_ref[...]
    ms = jnp.mean(x * x, axis=-1, keepdims=True)
    h = (x * lax.rsqrt(ms + RMS_EPS) * g_ref[...]).astype(jnp.bfloat16)
    o_ref[...] = x
    for s, w in _hidden_slices(wg_ref.shape[1]):
        o_ref[...] += _swiglu_slice(h, wg_ref[:, s:s + w], wu_ref[:, s:s + w], wd_ref[s:s + w, :])


def ffn_dense(x2, g, w_gate, w_up, w_down):
    n, d = x2.shape
    d_ff = w_gate.shape[1]
    tm = _row_tile(n, 512)
    row = pl.BlockSpec((tm, d), lambda i: (i, 0))
    return pl.pallas_call(
        _ffn_kernel,
        grid=(n // tm,),
        in_specs=[row, _resident((1, d)), _resident((d, d_ff)), _resident((d, d_ff)),
                  _resident((d_ff, d))],
        out_specs=row,
        out_shape=jax.ShapeDtypeStruct((n, d), jnp.float32),
        compiler_params=_params("parallel"),
        name="ffn_dense",
    )(x2, g.reshape(1, d), w_gate.astype(jnp.bfloat16), w_up.astype(jnp.bfloat16),
      w_down.astype(jnp.bfloat16))


NO_EXPERT = -1e30
ROUTE_EXPERT, ROUTE_GATE, ROUTE_RANK = 0, TOP_K, 2 * TOP_K


def _router_kernel(x_ref, g_ref, wr_ref, br_ref, ltri_ref, h_ref, route_ref, count_ref, count_s):
    f32 = jnp.float32
    tm = x_ref.shape[0]

    @pl.when(pl.program_id(0) == 0)
    def _():
        count_s[...] = jnp.zeros_like(count_s)

    x = x_ref[...]
    ms = jnp.mean(x * x, axis=-1, keepdims=True)
    h = x * lax.rsqrt(ms + RMS_EPS) * g_ref[...]
    h_hi, h_lo = _bf16_pieces(h, 2)
    h_ref[...] = h_hi
    logits = _dot(jnp.concatenate([h_hi, h_lo, h_hi], axis=1), wr_ref[...]) + br_ref[...]
    lane = lax.broadcasted_iota(jnp.int32, (tm, LANES), 1)

    def top(vals):
        best = jnp.max(vals, axis=-1, keepdims=True)
        return best, jnp.min(jnp.where(vals == best, lane, LANES), axis=-1, keepdims=True)

    m1, e1 = top(logits)
    m2, e2 = top(jnp.where(lane == e1, 2 * NO_EXPERT, logits))
    t = jnp.exp(m2 - m1)
    gate1 = 1.0 / (1.0 + t)
    gate2 = t / (1.0 + t)
    sel1, sel2 = lane == e1, lane == e2
    chosen = (sel1 | sel2).astype(f32)
    before = _dot(ltri_ref[...], chosen.astype(jnp.bfloat16)) + count_s[...]
    rank1 = jnp.sum(jnp.where(sel1, before, 0.0), axis=-1, keepdims=True)
    rank2 = jnp.sum(jnp.where(sel2, before, 0.0), axis=-1, keepdims=True)
    count_s[...] += jnp.sum(chosen, axis=0, keepdims=True)
    count_ref[...] = count_s[...]
    record = jnp.zeros((tm, LANES), f32)
    for pos, val in ((ROUTE_EXPERT, e1.astype(f32)), (ROUTE_EXPERT + 1, e2.astype(f32)),
                     (ROUTE_GATE, gate1), (ROUTE_GATE + 1, gate2),
                     (ROUTE_RANK, rank1), (ROUTE_RANK + 1, rank2)):
        record = jnp.where(lane == pos, val, record)
    route_ref[...] = record


def norm_router(x2, g, router_w, router_b):
    n, d = x2.shape
    tm = _row_tile(n, 1024)
    w_hi, w_lo = _bf16_pieces(jnp.pad(router_w, ((0, 0), (0, LANES - N_EXPERTS))), 2)
    wr = jnp.concatenate([w_hi, w_hi, w_lo], axis=0)
    br = jnp.pad(router_b, (0, LANES - N_EXPERTS), constant_values=NO_EXPERT).reshape(1, LANES)
    ti = np.arange(tm)
    ltri = jnp.asarray(ti[None, :] < ti[:, None], jnp.bfloat16)
    return pl.pallas_call(
        _router_kernel,
        grid=(n // tm,),
        in_specs=[pl.BlockSpec((tm, d), lambda i: (i, 0)), _resident((1, d)),
                  _resident(wr.shape), _resident((1, LANES)), _resident((tm, tm))],
        out_specs=[pl.BlockSpec((tm, d), lambda i: (i, 0)),
                   pl.BlockSpec((tm, LANES), lambda i: (i, 0)),
                   pl.BlockSpec((1, LANES), lambda i: (0, 0))],
        out_shape=[jax.ShapeDtypeStruct((n, d), jnp.bfloat16),
                   jax.ShapeDtypeStruct((n, LANES), jnp.float32),
                   jax.ShapeDtypeStruct((1, LANES), jnp.float32)],
        scratch_shapes=[pltpu.VMEM((1, LANES), jnp.float32)],
        compiler_params=_params("arbitrary"),
        name="norm_router",
    )(x2, g.reshape(1, d), wr, br, ltri)


def _moe_kernel(be_ref, nb_ref, xs_ref, wg_ref, wu_ref, wd_ref, o_ref, acc_s):
    j = pl.program_id(0)
    c = pl.program_id(1)
    last = pl.num_programs(1) - 1

    @pl.when(j < nb_ref[0])
    def _():
        h = xs_ref[...]

        @pl.when(c == 0)
        def _():
            acc_s[...] = jnp.zeros_like(acc_s)

        for s, w in _hidden_slices(wg_ref.shape[2]):
            acc_s[...] += _swiglu_slice(h, wg_ref[0, :, s:s + w], wu_ref[0, :, s:s + w],
                                        wd_ref[0, s:s + w, :])

        @pl.when(c == last)
        def _():
            o_ref[...] = acc_s[...].astype(o_ref.dtype)

    @pl.when(jnp.logical_and(j >= nb_ref[0], c == last))
    def _():
        o_ref[...] = jnp.zeros_like(o_ref)


def moe_experts(xs, block_expert, n_used, w_gate, w_up, w_down, tm, tc):
    n_slots, d = xs.shape
    d_e = w_gate.shape[2]
    n_blocks = n_slots // tm
    n_c = d_e // tc

    def live(j, nb):
        return jnp.minimum(j, nb[0] - 1)

    def chunk(j, c, nb):
        return jnp.where(j < nb[0], c, n_c - 1)

    return pl.pallas_call(
        _moe_kernel,
        grid_spec=pltpu.PrefetchScalarGridSpec(
            num_scalar_prefetch=2,
            grid=(n_blocks, n_c),
            in_specs=[
                pl.BlockSpec((tm, d), lambda j, c, be, nb: (live(j, nb), 0)),
                pl.BlockSpec((1, d, tc), lambda j, c, be, nb: (be[live(j, nb)], 0, chunk(j, c, nb))),
                pl.BlockSpec((1, d, tc), lambda j, c, be, nb: (be[live(j, nb)], 0, chunk(j, c, nb))),
                pl.BlockSpec((1, tc, d), lambda j, c, be, nb: (be[live(j, nb)], chunk(j, c, nb), 0)),
            ],
            out_specs=pl.BlockSpec((tm, d), lambda j, c, be, nb: (j, 0)),
            scratch_shapes=[pltpu.VMEM((tm, d), jnp.float32)],
        ),
        out_shape=jax.ShapeDtypeStruct((n_slots, d), jnp.bfloat16),
        compiler_params=_params("arbitrary", "arbitrary"),
        name="moe_experts",
    )(block_expert, n_used, xs, w_gate, w_up, w_down)


def moe_block(x2, g, router_w, router_b, w_gate, w_up, w_down):
    n, d = x2.shape
    tm = _row_tile(n, 1024)
    h, route, count = norm_router(x2, g, router_w, router_b)
    expert = route[:, ROUTE_EXPERT:ROUTE_EXPERT + TOP_K].astype(jnp.int32)
    gates = route[:, ROUTE_GATE:ROUTE_GATE + TOP_K]
    rank = route[:, ROUTE_RANK:ROUTE_RANK + TOP_K].astype(jnp.int32)
    counts = count[0, :N_EXPERTS].astype(jnp.int32)
    padded = (counts + tm - 1) // tm * tm
    padded_end = jnp.cumsum(padded)
    padded_start = padded_end - padded
    slot = jnp.take(padded_start, expert) + rank
    n_blocks = -(-(n * TOP_K) // tm) + N_EXPERTS
    n_slots = n_blocks * tm
    token = jnp.broadcast_to(jnp.arange(n, dtype=jnp.int32)[:, None], (n, TOP_K))
    slot_token = jnp.zeros((n_slots,), jnp.int32).at[slot.reshape(-1)].set(
        token.reshape(-1), unique_indices=True, mode="promise_in_bounds")
    block_start = jnp.arange(n_blocks) * tm
    block_expert = jnp.minimum(
        jnp.sum(block_start[:, None] >= padded_end[None, :], axis=1), N_EXPERTS - 1).astype(jnp.int32)
    n_used = (padded_end[-1] // tm).astype(jnp.int32).reshape(1)
    rows_at = lambda table, idx: table.at[idx].get(mode="promise_in_bounds")
    xs = rows_at(h, slot_token)
    d_e = w_gate.shape[2]
    tc = d_e // 2 if d_e % (2 * MXU_DIM) == 0 else d_e
    y_slots = moe_experts(xs, block_expert, n_used, w_gate.astype(jnp.bfloat16),
                          w_up.astype(jnp.bfloat16), w_down.astype(jnp.bfloat16), tm, tc)
    out = x2
    for j in range(TOP_K):
        out = out + gates[:, j:j + 1] * rows_at(y_slots, slot[:, j]).astype(jnp.float32)
    return out


GDN_BLOCK = 512
GDN_GROUP = 2
CONV_PAD = 8
GDN_PAIRS = GDN_HEADS // 2
SELECT_TERMS = 3
PAIR_ROWS = 2 * GDN_CHUNK


def _gdn_kernel(aqkv_ref, az_ref, small_ref, convw_ref, aneg_ref, bias_ref, gnorm_ref,
                eeven_ref, eodd_ref, ebeta_ref, bd_ref, ltri_ref, o_ref,
                ext_s, state_s, q_s, k_s, kb_s, qd_s, rhs_s, attn_s, w_s, kdt_s,
                gc_s, kd_s, u_s, oh_s):
    f32, bf16 = jnp.float32, jnp.bfloat16
    c = GDN_CHUNK
    tb = aqkv_ref.shape[0]
    gw = GDN_WIDTH
    nc = tb // c

    @pl.when(pl.program_id(1) == 0)
    def _():
        ext_s[0:CONV_PAD, :] = jnp.zeros((CONV_PAD, 3 * gw), f32)
        state_s[...] = jnp.zeros_like(state_s)

    ext_s[CONV_PAD:CONV_PAD + tb, :] = aqkv_ref[...]
    conv = convw_ref[0:1, :] * ext_s[pl.ds(CONV_PAD - CONV_K + 1, tb), :]
    for j in range(1, CONV_K):
        conv += convw_ref[j:j + 1, :] * ext_s[pl.ds(CONV_PAD - CONV_K + 1 + j, tb), :]
    ext_s[0:CONV_PAD, :] = ext_s[tb:tb + CONV_PAD, :]
    qkv = conv * jax.nn.sigmoid(conv)
    q, k, v = qkv[:, :gw], qkv[:, gw:2 * gw], qkv[:, 2 * gw:]

    bd = bd_ref[...]
    q = q * lax.rsqrt(_group_sum_sq(q, bd) + RMS_EPS) * (HEAD_DIM ** -0.5)
    k = k * lax.rsqrt(_group_sum_sq(k, bd) + RMS_EPS)

    sm = small_ref[...]
    pre = sm + bias_ref[...]
    softplus = jnp.maximum(pre, 0.0) + jnp.log(1.0 + jnp.exp(-jnp.abs(pre)))
    g = _prefix_dot(ltri_ref[...], aneg_ref[...] * softplus, SELECT_TERMS)
    g_even = _select_dot(g, eeven_ref[...])
    g_odd = _select_dot(g, eodd_ref[...])
    bb = _select_dot(jax.nn.sigmoid(sm), ebeta_ref[...])
    lane = lax.broadcasted_iota(jnp.int32, (1, gw), 1)
    even_lane = (lane // HEAD_DIM) % 2 == 0
    gb = jnp.where(even_lane, g_even, g_odd)
    gb3 = gb.reshape(nc, c, gw)
    k_dec = (k.reshape(nc, c, gw) * jnp.exp(gb3[:, c - 1:c, :] - gb3)).reshape(tb, gw)
    eg = jnp.exp(gb)
    kb = k * bb

    def stack(x):
        x3 = x.reshape(nc, c, gw)
        zero = jnp.zeros_like(x3)
        return jnp.concatenate([jnp.where(even_lane, x3, zero), jnp.where(even_lane, zero, x3)], axis=1)

    q_s[...] = stack(q.astype(bf16))
    k_s[...] = stack(k.astype(bf16))
    kb_s[...] = stack(kb.astype(bf16))
    qd_s[...] = stack((q * eg).astype(bf16))
    kd_s[...] = stack(k_dec)
    vb_st = stack((v * bb).astype(bf16))
    kbe_st = stack((kb * eg).astype(bf16))
    for p in range(GDN_PAIRS):
        ps = slice(p * LANES, (p + 1) * LANES)
        rhs_s[:, :, 2 * p * LANES:(2 * p + 1) * LANES] = vb_st[:, :, ps]
        rhs_s[:, :, (2 * p + 1) * LANES:(2 * p + 2) * LANES] = kbe_st[:, :, ps]
    gc_s[...] = jnp.concatenate([g_even.reshape(nc, c, gw), g_odd.reshape(nc, c, gw)], axis=1)

    row = lax.broadcasted_iota(jnp.int32, (PAIR_ROWS, PAIR_ROWS), 0)
    col = lax.broadcasted_iota(jnp.int32, (PAIR_ROWS, PAIR_ROWS), 1)
    causal = (row // c == col // c) & (row >= col)
    strict = causal & (row > col)
    eye = (row == col).astype(f32)

    def phase_a(gi, carry):
        units = [(gi * GDN_GROUP + ci, p) for ci in range(GDN_GROUP) for p in range(GDN_PAIRS)]
        at = lambda ref, n, p: ref[n, :, p * LANES:(p + 1) * LANES]
        k_u = [at(k_s, n, p) for n, p in units]
        kk = [_dot_nt(at(kb_s, n, p), kx) for (n, p), kx in zip(units, k_u)]
        qk = [_dot_nt(at(q_s, n, p), kx) for (n, p), kx in zip(units, k_u)]
        gc = [at(gc_s, n, p) for n, p in units]
        dm = [jnp.where(causal, jnp.exp(jnp.minimum(gx - gx.T, 0.0)), 0.0) for gx in gc]
        a = [jnp.where(strict, x * d, 0.0) for x, d in zip(kk, dm)]
        for (n, p), x, d in zip(units, qk, dm):
            attn_s[n, :, p * LANES:(p + 1) * LANES] = (x * d).astype(bf16)
        for (n, p) in units:
            kdt_s[n, :, p * LANES:(p + 1) * LANES] = at(kd_s, n, p).T.astype(bf16)
        ab = [x.astype(bf16) for x in a]
        sq = [_dot(x, x) for x in ab]
        inv = [eye - x for x in a]
        for _ in range(4):
            sqb = [x.astype(bf16) for x in sq]
            both = [_dot(x, jnp.concatenate([x, t.astype(bf16)], axis=1)) for x, t in zip(sqb, inv)]
            sq = [x[:, :PAIR_ROWS] for x in both]
            inv = [t + x[:, PAIR_ROWS:] for t, x in zip(inv, both)]
        inv = [t + _dot(x.astype(bf16), t.astype(bf16)) for t, x in zip(inv, sq)]
        for (n, p), t in zip(units, inv):
            uw = _dot(t.astype(bf16), rhs_s[n, :, 2 * p * LANES:(2 * p + 2) * LANES])
            u_s[n, :, p * LANES:(p + 1) * LANES] = uw[:, :LANES]
            w_s[n, :, p * LANES:(p + 1) * LANES] = uw[:, LANES:].astype(bf16)
        return carry

    lax.fori_loop(0, nc // GDN_GROUP, phase_a, 0)

    def phase_b(n, carry):
        pairs = range(GDN_PAIRS)
        at = lambda ref, p: ref[n, :, p * LANES:(p + 1) * LANES]
        s = [state_s[p] for p in pairs]
        sb = [x.astype(bf16) for x in s]
        both = [_dot(jnp.concatenate([at(w_s, p), at(qd_s, p)], axis=0), sb[p]) for p in pairs]
        vn = [(at(u_s, p) - both[p][:PAIR_ROWS]).astype(bf16) for p in pairs]
        o_st = [both[p][PAIR_ROWS:] + _dot(at(attn_s, p), vn[p]) for p in pairs]
        upd = [_dot(at(kdt_s, p), vn[p]) for p in pairs]
        for p in pairs:
            ps = slice(p * LANES, (p + 1) * LANES)
            g_last = jnp.where(even_lane[:, :LANES], gc_s[n, c - 1:c, ps], gc_s[n, 2 * c - 1:2 * c, ps])
            state_s[p] = s[p] * jnp.exp(g_last) + upd[p]
            oh_s[pl.ds(pl.multiple_of(n * c, c), c), p * LANES:(p + 1) * LANES] = (
                o_st[p][:c] + o_st[p][c:])
        return carry

    lax.fori_loop(0, nc, phase_b, 0)

    o = oh_s[...]
    ms = _group_sum_sq(o, bd) * (1.0 / HEAD_DIM)
    z = az_ref[...]
    o_ref[...] = o * lax.rsqrt(ms + RMS_EPS) * gnorm_ref[...] * (z * jax.nn.sigmoid(z))


def gdn_mixer(proj, bsz, s_len, conv_w, a_log, dt_bias, gdn_norm):
    n = bsz * s_len
    gw, nh, c = GDN_WIDTH, GDN_HEADS, GDN_CHUNK
    tb = _row_tile(s_len, GDN_BLOCK)
    nt = s_len // tb
    nc = tb // c
    assert nc % GDN_GROUP == 0 and nh % 2 == 0
    f32, bf16 = jnp.float32, jnp.bfloat16
    aneg = jnp.zeros((1, LANES), f32).at[0, :nh].set(-jnp.exp(a_log))
    bias = jnp.zeros((1, LANES), f32).at[0, :nh].set(dt_bias)
    gnorm = jnp.tile(gdn_norm, nh).reshape(1, gw)
    lane_head = np.arange(gw) // HEAD_DIM
    src = np.arange(LANES)[:, None]
    stacked_sel = lambda m: jnp.asarray(np.tile(m, (SELECT_TERMS, 1)), bf16)
    eeven = stacked_sel(src == (lane_head // 2 * 2)[None, :])
    eodd = stacked_sel(src == (lane_head // 2 * 2 + 1)[None, :])
    ebeta = stacked_sel(src == nh + lane_head[None, :])
    bd = jnp.asarray(lane_head[:, None] == lane_head[None, :], bf16)
    ti = np.arange(tb)
    ltri = jnp.asarray((ti[:, None] // c == ti[None, :] // c) & (ti[None, :] <= ti[:, None]), bf16)
    rows = lambda width, cb: pl.BlockSpec((tb, width), lambda b, t: (b * nt + t, cb))
    stacked = lambda dt, width=gw: pltpu.VMEM((nc, PAIR_ROWS, width), dt)
    return pl.pallas_call(
        _gdn_kernel,
        grid=(bsz, nt),
        in_specs=[rows(3 * gw, 0), rows(gw, 3), rows(LANES, SMALL_OFF // LANES),
                  _resident((CONV_K, 3 * gw)), _resident((1, LANES)), _resident((1, LANES)),
                  _resident((1, gw)), _resident(eeven.shape), _resident(eodd.shape),
                  _resident(ebeta.shape), _resident((gw, gw)), _resident((tb, tb))],
        out_specs=pl.BlockSpec((tb, gw), lambda b, t: (b * nt + t, 0)),
        out_shape=jax.ShapeDtypeStruct((n, gw), f32),
        scratch_shapes=[
            pltpu.VMEM((tb + CONV_PAD, 3 * gw), f32),
            pltpu.VMEM((GDN_PAIRS, LANES, LANES), f32),
            stacked(bf16), stacked(bf16), stacked(bf16), stacked(bf16),
            stacked(bf16, 2 * gw),
            stacked(bf16), stacked(bf16), stacked(bf16),
            stacked(f32), stacked(f32), stacked(f32),
            pltpu.VMEM((tb, gw), f32)],
        compiler_params=_params("parallel", "arbitrary"),
        name="gdn_mixer",
    )(proj, proj, proj, conv_w, aneg, bias, gnorm, eeven, eodd, ebeta, bd, ltri)


SWA_SPAN = 128
SWA_PREP_ROWS = 512
SWA_GROUP = 4
MASKED_SCORE = -1e30


def _swa_kernel(q_ref, k_ref, v_ref, cos_ref, sin_ref, qn_ref, kn_ref, bd_ref, o_ref,
                q_s, k_s, v_s, op_s, lse_s):
    f32, bf16 = jnp.float32, jnp.bfloat16
    s_len = q_ref.shape[0]
    blk = SWA_SPAN
    lane = lax.broadcasted_iota(jnp.int32, (1, LANES), 1)
    first_half = (lane % HEAD_DIM) < (HEAD_DIM // 2)
    head0 = lane < HEAD_DIM
    bd = bd_ref[...]
    prep_rows = min(SWA_PREP_ROWS, s_len)

    def normed_rotary(x, gain, cos, sin, scale):
        ms = _group_sum_sq(x, bd) * (1.0 / HEAD_DIM)
        y = x * lax.rsqrt(ms + RMS_EPS) * gain
        partner = jnp.where(first_half, pltpu.roll(y, LANES - HEAD_DIM // 2, 1),
                            pltpu.roll(y, HEAD_DIM // 2, 1))
        return (y * cos + partner * sin) * scale

    def prologue(i, carry):
        rows = pl.ds(pl.multiple_of(i * prep_rows, prep_rows), prep_rows)
        cos, sin = cos_ref[rows, :], sin_ref[rows, :]
        q_s[rows, :] = normed_rotary(q_ref[rows, :], qn_ref[...], cos, sin, HEAD_DIM ** -0.5)
        k_s[rows, :] = normed_rotary(k_ref[rows, :], kn_ref[...], cos, sin, 1.0)
        v_s[rows, :] = v_ref[rows, :]
        return carry

    lax.fori_loop(0, s_len // prep_rows, prologue, 0)

    qi = lax.broadcasted_iota(jnp.int32, (blk, 2 * blk), 0)
    kj = lax.broadcasted_iota(jnp.int32, (blk, 2 * blk), 1)
    dist = blk + qi - kj
    band = (dist >= 0) & (dist <= SWA_SPAN)
    ones_kv = jnp.ones((2 * blk, LANES), bf16)

    for pidx, (window, dil) in enumerate(DILATED_PAIRS):
        n_blk = s_len // (dil * blk)

        def rows_of(r, i):
            start = r + i * (blk * dil)
            if dil == 1:
                return pl.ds(start, blk)
            return pl.ds(start, blk, stride=dil)

        def block_group(gi, carry):
            units = [gi * SWA_GROUP + j for j in range(SWA_GROUP)]
            ri = [(u // n_blk, u % n_blk) for u in units]
            cur = [rows_of(r, i) for r, i in ri]
            prev = [rows_of(r, jnp.maximum(i - 1, 0)) for r, i in ri]
            qb = [q_s[c, :] for c in cur]
            kw = [jnp.concatenate([k_s[pv, :], k_s[c, :]], axis=0).astype(bf16)
                  for pv, c in zip(prev, cur)]
            vw = [jnp.concatenate([jnp.concatenate([v_s[pv, :], v_s[c, :]], axis=0).astype(bf16),
                                   ones_kv], axis=1) for pv, c in zip(prev, cur)]
            valid = [band & (kj >= jnp.where(i > 0, 0, blk)) for _, i in ri]
            chains = [(j, h) for j in range(SWA_GROUP) for h in range(2)]
            qh = [jnp.where(head0 if h == 0 else ~head0, qb[j], 0.0).astype(bf16) for j, h in chains]
            s = [jnp.where(valid[j], _dot_nt(x, kw[j]), MASKED_SCORE) for (j, h), x in zip(chains, qh)]
            m = [jnp.max(x, axis=-1, keepdims=True) for x in s]
            p = [jnp.exp((x - mx).astype(bf16)) for x, mx in zip(s, m)]
            pv = [_dot(x, vw[j]) for (j, h), x in zip(chains, p)]
            o = [x[:, :LANES] / x[:, LANES:] for x in pv]
            lse = [mx + jnp.log(x[:, LANES:]) for mx, x in zip(m, pv)]
            for j in range(SWA_GROUP):
                op_s[pidx, cur[j], :] = jnp.where(head0, o[2 * j], o[2 * j + 1])
                lse_s[pidx, cur[j], :] = jnp.where(head0, lse[2 * j], lse[2 * j + 1])
            return carry

        lax.fori_loop(0, s_len // (blk * SWA_GROUP), block_group, 0)

    def epilogue(i, carry):
        rows = pl.ds(pl.multiple_of(i * prep_rows, prep_rows), prep_rows)
        l0, l1, l2 = lse_s[0, rows, :], lse_s[1, rows, :], lse_s[2, rows, :]
        m = jnp.maximum(jnp.maximum(l0, l1), l2)
        w0, w1, w2 = jnp.exp(l0 - m), jnp.exp(l1 - m), jnp.exp(l2 - m)
        mix = w0 * op_s[0, rows, :] + w1 * op_s[1, rows, :] + w2 * op_s[2, rows, :]
        o_ref[rows, :] = mix / (w0 + w1 + w2)
        return carry

    lax.fori_loop(0, s_len // prep_rows, epilogue, 0)


def swa_mixer(proj, bsz, s_len, q_norm, k_norm):
    assert all(w // d == SWA_SPAN for w, d in DILATED_PAIRS) and len(DILATED_PAIRS) == 3
    assert s_len % (SWA_SPAN * max(d for _, d in DILATED_PAIRS)) == 0
    n = bsz * s_len
    f32 = jnp.float32
    n_pairs = SWA_WIDTH // LANES
    half = HEAD_DIM // 2
    inv_freq = ROPE_THETA ** (-jnp.arange(0, HEAD_DIM, 2, dtype=f32) / HEAD_DIM)
    ang = jnp.arange(s_len, dtype=f32)[:, None] * inv_freq[None, :]
    cos = jnp.tile(jnp.cos(ang), (1, LANES // half))
    sin = jnp.tile(jnp.concatenate([-jnp.sin(ang), jnp.sin(ang)], axis=1), (1, LANES // HEAD_DIM))
    gain = lambda g: jnp.tile(g, LANES // HEAD_DIM).reshape(1, LANES)
    lane_head = np.arange(LANES) // HEAD_DIM
    bd = jnp.asarray(lane_head[:, None] == lane_head[None, :], jnp.bfloat16)
    q_col0 = 4 * GDN_WIDTH // LANES
    col = lambda which: pl.BlockSpec(
        (s_len, LANES), lambda b, p: (b, q_col0 + which * n_pairs + p))
    seq = pltpu.VMEM((s_len, LANES), f32)
    per_pattern = pltpu.VMEM((len(DILATED_PAIRS), s_len, LANES), f32)
    return pl.pallas_call(
        _swa_kernel,
        grid=(bsz, n_pairs),
        in_specs=[col(0), col(1), col(2), _resident((s_len, LANES)), _resident((s_len, LANES)),
                  _resident((1, LANES)), _resident((1, LANES)), _resident((LANES, LANES))],
        out_specs=pl.BlockSpec((s_len, LANES), lambda b, p: (b, p)),
        out_shape=jax.ShapeDtypeStruct((n, SWA_WIDTH), f32),
        scratch_shapes=[seq, seq, seq, per_pattern, per_pattern],
        compiler_params=_params("parallel", "parallel"),
        name="swa_mixer",
    )(proj, proj, proj, cos, sin, gain(q_norm), gain(k_norm), bd)


def kernel(x, norm_mix, w_in, conv_w, a_log, dt_bias, gdn_norm, q_norm, k_norm, pool_w,
           pool_scale, w_out, norm_ffn, ffn_gate, ffn_up, ffn_down, router_w, router_b,
           exp_gate, exp_up, exp_down):
    bsz, s_len, d = x.shape
    n = bsz * s_len
    depth = w_in.shape[0]
    x2 = x.reshape(n, d)
    for layer in range(depth):
        proj = norm_inproj(x2, norm_mix[layer], _arrange_w_in(w_in[layer]))
        ya = gdn_mixer(proj, bsz, s_len, conv_w[layer], a_log[layer], dt_bias[layer],
                       gdn_norm[layer])
        yb = swa_mixer(proj, bsz, s_len, q_norm[layer], k_norm[layer])
        x2 = pool_outproj_residual(x2, ya, yb, proj, bsz, s_len, pool_w[layer],
                                   pool_scale[layer], w_out[layer])
        i = layer // 2
        if layer % 2 == 0:
            x2 = ffn_dense(x2, norm_ffn[layer], ffn_gate[i], ffn_up[i], ffn_down[i])
        else:
            x2 = moe_block(x2, norm_ffn[layer], router_w[i], router_b[i], exp_gate[i],
                           exp_up[i], exp_down[i])
    return x2.reshape(bsz, s_len, d)
```

```python
import functools

import jax
import jax.numpy as jnp
import numpy as np
from jax import lax
from jax.experimental import pallas as pl
from jax.experimental.pallas import tpu as pltpu

D_MODEL = 1024
HEAD_DIM = 64
MIX_WIDTH = D_MODEL
POOL_WIDTH = MIX_WIDTH // 4
POOL_GROUPS = 4
POOL_GROUP_DIM = POOL_WIDTH // POOL_GROUPS
POOL_SIZES = (2, 4, 8, 16)
GDN_HEADS = (MIX_WIDTH - POOL_WIDTH) // (2 * HEAD_DIM)
GDN_WIDTH = GDN_HEADS * HEAD_DIM
SWA_HEADS = (MIX_WIDTH - POOL_WIDTH - GDN_WIDTH) // HEAD_DIM
SWA_WIDTH = SWA_HEADS * HEAD_DIM
CONV_K = 4
GDN_CHUNK = 64
DILATED_PAIRS = ((128, 1), (512, 4), (2048, 16))
ROPE_THETA = 10000.0
RMS_EPS = 1e-6
N_EXPERTS = 8
TOP_K = 2

VMEM_LIMIT_BYTES = 56 * 1024 * 1024
LANES = 128

SMALL_OFF = 7 * GDN_WIDTH
POOL_OFF = SMALL_OFF + LANES
PROJ_WIDTH = POOL_OFF + POOL_WIDTH


def _row_tile(n_rows, want):
    t = min(want, n_rows)
    while n_rows % t:
        t //= 2
    return t


def _params(*sem):
    return pltpu.CompilerParams(dimension_semantics=sem, vmem_limit_bytes=VMEM_LIMIT_BYTES)


def _resident(shape):
    zeros = (0,) * len(shape)
    return pl.BlockSpec(shape, lambda *_: zeros, pipeline_mode=pl.Buffered(1))


def _dot(a, b, precision=None):
    return jnp.dot(a, b, preferred_element_type=jnp.float32, precision=precision)


def _dot_nt(a, b):
    return lax.dot_general(a, b, (((1,), (1,)), ((), ())), preferred_element_type=jnp.float32)


def _bf16_pieces(x, terms):
    pieces, rest = [], x
    for t in range(terms):
        pieces.append(rest.astype(jnp.bfloat16))
        if t + 1 < terms:
            rest = rest - pieces[-1].astype(jnp.float32)
    return pieces


def _select_dot(x, sel_stacked):
    terms = sel_stacked.shape[0] // x.shape[1]
    return _dot(jnp.concatenate(_bf16_pieces(x, terms), axis=1), sel_stacked)


def _prefix_dot(sel, x, terms):
    width = x.shape[1]
    wide = _dot(sel, jnp.concatenate(_bf16_pieces(x, terms), axis=1))
    return sum(wide[:, t * width:(t + 1) * width] for t in range(terms))


def _group_sum_sq(x, group_ones):
    return _dot((x * x).astype(jnp.bfloat16), group_ones)


def _norm_inproj_kernel(x_ref, g_ref, w_ref, o_ref, wb_s):
    @pl.when(pl.program_id(0) == 0)
    def _():
        bf16 = jnp.bfloat16
        a_end = 4 * GDN_WIDTH
        b_start = a_end + 2 * GDN_HEADS
        b_end = b_start + 3 * SWA_WIDTH
        wb_s[:, :a_end] = w_ref[0, :, :a_end].astype(bf16)
        wb_s[:, a_end:SMALL_OFF] = w_ref[0, :, b_start:b_end].astype(bf16)
        lane = lax.broadcasted_iota(jnp.int32, (1, LANES), 1)
        wb_s[:, SMALL_OFF:POOL_OFF] = jnp.where(
            lane < 2 * GDN_HEADS, w_ref[0, :, a_end:a_end + LANES], 0.0).astype(bf16)
        wb_s[:, POOL_OFF:] = w_ref[0, :, b_end:].astype(bf16)

    x = x_ref[...]
    ms = jnp.mean(x * x, axis=-1, keepdims=True)
    h = (x * lax.rsqrt(ms + RMS_EPS) * g_ref[...]).astype(jnp.bfloat16)
    o_ref[...] = _dot(h, wb_s[...])


def norm_inproj(x2, g, w_in, layer):
    n, d = x2.shape
    in_width = w_in.shape[2]
    assert in_width == 4 * GDN_WIDTH + 2 * GDN_HEADS + 3 * SWA_WIDTH + POOL_WIDTH
    tm = _row_tile(n, 512)
    return pl.pallas_call(
        _norm_inproj_kernel,
        grid=(n // tm,),
        in_specs=[pl.BlockSpec((tm, d), lambda i: (i, 0)),
                  _resident((1, d)),
                  pl.BlockSpec((1, d, in_width), lambda i: (layer, 0, 0), pipeline_mode=pl.Buffered(1))],
        out_specs=pl.BlockSpec((tm, PROJ_WIDTH), lambda i: (i, 0)),
        out_shape=jax.ShapeDtypeStruct((n, PROJ_WIDTH), jnp.float32),
        scratch_shapes=[pltpu.VMEM((d, PROJ_WIDTH), jnp.bfloat16)],
        compiler_params=_params("arbitrary"),
        name="norm_inproj",
    )(x2, g.reshape(1, d), w_in)


POOL_HIST = max(POOL_SIZES)


def _pool_outproj_kernel(x_ref, ya_ref, yb_ref, cu_ref, size_ref, pw_ref, ps_ref,
                         w_ref, o_ref, ext_s):
    f32, bf16 = jnp.float32, jnp.bfloat16
    tm = x_ref.shape[0]
    t_blk = pl.program_id(1)

    @pl.when(t_blk == 0)
    def _():
        ext_s[0:POOL_HIST, :] = jnp.zeros((POOL_HIST, POOL_WIDTH), f32)

    cu = cu_ref[...]
    ext_s[POOL_HIST:POOL_HIST + tm, :] = cu
    ext = ext_s[...]
    ext_s[0:POOL_HIST, :] = ext_s[tm:tm + POOL_HIST, :]
    size = size_ref[...]
    win = ext
    total = jnp.zeros_like(ext)
    shift = 1
    while shift < POOL_HIST:
        win = win + pltpu.roll(win, shift, 0)
        shift *= 2
        total = jnp.where(size == shift, win, total)
    total = total[POOL_HIST:, :]
    pos = t_blk * tm + lax.broadcasted_iota(jnp.int32, (tm, 1), 0)
    count = jnp.minimum(pos + 1, size).astype(f32)
    pooled = total / count - cu
    yc = _dot(pooled.astype(bf16), pw_ref[...]) * ps_ref[...]

    mixed = jnp.concatenate([ya_ref[...].astype(bf16), yb_ref[...].astype(bf16), yc.astype(bf16)], axis=1)
    o_ref[...] = x_ref[...] + _dot(mixed, w_ref[...])


def pool_outproj_residual(x2, ya, yb, proj, bsz, s_len, pool_w, pool_scale, w_out):
    n, d = x2.shape
    tm = _row_tile(s_len, 1024)
    nt = s_len // tm
    bf16 = jnp.bfloat16
    size = jnp.asarray(np.repeat(np.asarray(POOL_SIZES, np.int32), POOL_GROUP_DIM).reshape(1, POOL_WIDTH))
    pw = jax.scipy.linalg.block_diag(*[pool_w[g] for g in range(POOL_GROUPS)]).astype(bf16)
    row = lambda width, cb=0: pl.BlockSpec((tm, width), lambda b, t: (b * nt + t, cb))
    return pl.pallas_call(
        _pool_outproj_kernel,
        grid=(bsz, nt),
        in_specs=[row(d), row(GDN_WIDTH), row(SWA_WIDTH), row(POOL_WIDTH, POOL_OFF // POOL_WIDTH),
                  _resident((1, POOL_WIDTH)), _resident(pw.shape), _resident((1, POOL_WIDTH)),
                  _resident(w_out.shape)],
        out_specs=row(d),
        out_shape=jax.ShapeDtypeStruct((n, d), jnp.float32),
        scratch_shapes=[pltpu.VMEM((tm + POOL_HIST, POOL_WIDTH), jnp.float32)],
        compiler_params=_params("parallel", "arbitrary"),
        name="pool_outproj_residual",
    )(x2, ya, yb, proj, size, pw, pool_scale.reshape(1, POOL_WIDTH), w_out.astype(bf16))


MXU_DIM = 256
HIDDEN_SLICE = 512


def _hidden_slices(width):
    assert width % MXU_DIM == 0
    return [(s, min(HIDDEN_SLICE, width - s)) for s in range(0, width, HIDDEN_SLICE)]


def _swiglu_slice(h, wg, wu, wd):
    gate = _dot(h, wg)
    up = _dot(h, wu)
    return _dot((gate * jax.nn.sigmoid(gate) * up).astype(jnp.bfloat16), wd)


def _ffn_kernel(x_ref, g_ref, wg_ref, wu_ref, wd_ref, o_ref):
    x = x_ref[...]
    ms = jnp.mean(x * x, axis=-1, keepdims=True)
    h = (x * lax.rsqrt(ms + RMS_EPS) * g_ref[...]).astype(jnp.bfloat16)
    o_ref[...] = x
    for s, w in _hidden_slices(wg_ref.shape[1]):
        o_ref[...] += _swiglu_slice(h, wg_ref[:, s:s + w], wu_ref[:, s:s + w], wd_ref[s:s + w, :])


def ffn_dense(x2, g, w_gate, w_up, w_down):
    n, d = x2.shape
    d_ff = w_gate.shape[1]
    tm = _row_tile(n, 512)
    row = pl.BlockSpec((tm, d), lambda i: (i, 0))
    return pl.pallas_call(
        _ffn_kernel,
        grid=(n // tm,),
        in_specs=[row, _resident((1, d)), _resident((d, d_ff)), _resident((d, d_ff)),
                  _resident((d_ff, d))],
        out_specs=row,
        out_shape=jax.ShapeDtypeStruct((n, d), jnp.float32),
        compiler_params=_params("parallel"),
        name="ffn_dense",
    )(x2, g.reshape(1, d), w_gate.astype(jnp.bfloat16), w_up.astype(jnp.bfloat16),
      w_down.astype(jnp.bfloat16))


NO_EXPERT = -1e30
ROUTE_EXPERT, ROUTE_GATE, ROUTE_RANK = 0, TOP_K, 2 * TOP_K


def _router_kernel(x_ref, g_ref, wr_ref, br_ref, ltri_ref, h_ref, route_ref, count_ref, count_s):
    f32 = jnp.float32
    tm = x_ref.shape[0]

    @pl.when(pl.program_id(0) == 0)
    def _():
        count_s[...] = jnp.zeros_like(count_s)

    x = x_ref[...]
    ms = jnp.mean(x * x, axis=-1, keepdims=True)
    h = x * lax.rsqrt(ms + RMS_EPS) * g_ref[...]
    h_hi, h_lo = _bf16_pieces(h, 2)
    h_ref[...] = h_hi
    logits = _dot(jnp.concatenate([h_hi, h_lo, h_hi], axis=1), wr_ref[...]) + br_ref[...]
    lane = lax.broadcasted_iota(jnp.int32, (tm, LANES), 1)

    def top(vals):
        best = jnp.max(vals, axis=-1, keepdims=True)
        return best, jnp.min(jnp.where(vals == best, lane, LANES), axis=-1, keepdims=True)

    m1, e1 = top(logits)
    m2, e2 = top(jnp.where(lane == e1, 2 * NO_EXPERT, logits))
    t = jnp.exp(m2 - m1)
    gate1 = 1.0 / (1.0 + t)
    gate2 = t / (1.0 + t)
    sel1, sel2 = lane == e1, lane == e2
    chosen = (sel1 | sel2).astype(f32)
    before = _dot(ltri_ref[...], chosen.astype(jnp.bfloat16)) + count_s[...]
    rank1 = jnp.sum(jnp.where(sel1, before, 0.0), axis=-1, keepdims=True)
    rank2 = jnp.sum(jnp.where(sel2, before, 0.0), axis=-1, keepdims=True)
    count_s[...] += jnp.sum(chosen, axis=0, keepdims=True)
    count_ref[...] = count_s[...]
    record = jnp.zeros((tm, LANES), f32)
    for pos, val in ((ROUTE_EXPERT, e1.astype(f32)), (ROUTE_EXPERT + 1, e2.astype(f32)),
                     (ROUTE_GATE, gate1), (ROUTE_GATE + 1, gate2),
                     (ROUTE_RANK, rank1), (ROUTE_RANK + 1, rank2)):
        record = jnp.where(lane == pos, val, record)
    route_ref[...] = record


def norm_router(x2, g, router_w, router_b):
    n, d = x2.shape
    tm = _row_tile(n, 1024)
    w_hi, w_lo = _bf16_pieces(jnp.pad(router_w, ((0, 0), (0, LANES - N_EXPERTS))), 2)
    wr = jnp.concatenate([w_hi, w_hi, w_lo], axis=0)
    br = jnp.pad(router_b, (0, LANES - N_EXPERTS), constant_values=NO_EXPERT).reshape(1, LANES)
    ti = np.arange(tm)
    ltri = jnp.asarray(ti[None, :] < ti[:, None], jnp.bfloat16)
    return pl.pallas_call(
        _router_kernel,
        grid=(n // tm,),
        in_specs=[pl.BlockSpec((tm, d), lambda i: (i, 0)), _resident((1, d)),
                  _resident(wr.shape), _resident((1, LANES)), _resident((tm, tm))],
        out_specs=[pl.BlockSpec((tm, d), lambda i: (i, 0)),
                   pl.BlockSpec((tm, LANES), lambda i: (i, 0)),
                   pl.BlockSpec((1, LANES), lambda i: (0, 0))],
        out_shape=[jax.ShapeDtypeStruct((n, d), jnp.bfloat16),
                   jax.ShapeDtypeStruct((n, LANES), jnp.float32),
                   jax.ShapeDtypeStruct((1, LANES), jnp.float32)],
        scratch_shapes=[pltpu.VMEM((1, LANES), jnp.float32)],
        compiler_params=_params("arbitrary"),
        name="norm_router",
    )(x2, g.reshape(1, d), wr, br, ltri)


def _moe_kernel(first, be_ref, nb_ref, xs_ref, wg_ref, wu_ref, wd_ref, *refs):
    o_ref, acc_s = refs[-2:]
    j = first + pl.program_id(0)
    c = pl.program_id(1)
    last = pl.num_programs(1) - 1

    @pl.when(j < nb_ref[0])
    def _():
        h = xs_ref[...]

        @pl.when(c == 0)
        def _():
            acc_s[...] = jnp.zeros_like(acc_s)

        for s, w in _hidden_slices(wg_ref.shape[2]):
            acc_s[...] += _swiglu_slice(h, wg_ref[0, :, s:s + w], wu_ref[0, :, s:s + w],
                                        wd_ref[0, s:s + w, :])

        @pl.when(c == last)
        def _():
            o_ref[...] = acc_s[...].astype(o_ref.dtype)

    @pl.when(jnp.logical_and(j >= nb_ref[0], c == last))
    def _():
        o_ref[...] = jnp.zeros_like(o_ref)


def moe_experts(xs, first, n_blocks_total, y_prev, block_expert, n_used, w_gate, w_up, w_down, tm, tc):
    d = xs.shape[1]
    n_here = xs.shape[0] // tm
    d_e = w_gate.shape[2]
    n_c = d_e // tc

    def live(j, nb):
        return jnp.clip(jnp.minimum(first + j, nb[0] - 1), first, first + n_here - 1)

    def chunk(j, c, nb):
        return jnp.where(first + j < nb[0], c, n_c - 1)

    in_specs = [
        pl.BlockSpec((tm, d), lambda j, c, be, nb: (live(j, nb) - first, 0)),
        pl.BlockSpec((1, d, tc), lambda j, c, be, nb: (be[live(j, nb)], 0, chunk(j, c, nb))),
        pl.BlockSpec((1, d, tc), lambda j, c, be, nb: (be[live(j, nb)], 0, chunk(j, c, nb))),
        pl.BlockSpec((1, tc, d), lambda j, c, be, nb: (be[live(j, nb)], chunk(j, c, nb), 0)),
    ]
    operands = [block_expert, n_used, xs, w_gate, w_up, w_down]
    aliases = {}
    if y_prev is not None:
        in_specs.append(pl.BlockSpec(memory_space=pl.ANY))
        aliases = {len(operands): 0}
        operands.append(y_prev)
    return pl.pallas_call(
        functools.partial(_moe_kernel, first),
        grid_spec=pltpu.PrefetchScalarGridSpec(
            num_scalar_prefetch=2,
            grid=(n_here, n_c),
            in_specs=in_specs,
            out_specs=pl.BlockSpec((tm, d), lambda j, c, be, nb: (first + j, 0)),
            scratch_shapes=[pltpu.VMEM((tm, d), jnp.float32)],
        ),
        out_shape=jax.ShapeDtypeStruct((n_blocks_total * tm, d), jnp.bfloat16),
        input_output_aliases=aliases,
        compiler_params=_params("arbitrary", "arbitrary"),
        name="moe_experts",
    )(*operands)


MOE_STAGES = 2


def moe_block(x2, g, router_w, router_b, w_gate, w_up, w_down):
    n, d = x2.shape
    tm = _row_tile(n, 1024)
    h, route, count = norm_router(x2, g, router_w, router_b)
    expert = route[:, ROUTE_EXPERT:ROUTE_EXPERT + TOP_K].astype(jnp.int32)
    gates = route[:, ROUTE_GATE:ROUTE_GATE + TOP_K]
    rank = route[:, ROUTE_RANK:ROUTE_RANK + TOP_K].astype(jnp.int32)
    counts = count[0, :N_EXPERTS].astype(jnp.int32)
    padded = (counts + tm - 1) // tm * tm
    padded_end = jnp.cumsum(padded)
    padded_start = padded_end - padded
    slot = jnp.take(padded_start, expert) + rank
    n_blocks = -(-(n * TOP_K) // tm) + N_EXPERTS
    n_slots = n_blocks * tm
    token = jnp.broadcast_to(jnp.arange(n, dtype=jnp.int32)[:, None], (n, TOP_K))
    slot_token = jnp.zeros((n_slots,), jnp.int32).at[slot.reshape(-1)].set(
        token.reshape(-1), unique_indices=True, mode="promise_in_bounds")
    block_start = jnp.arange(n_blocks) * tm
    block_expert = jnp.minimum(
        jnp.sum(block_start[:, None] >= padded_end[None, :], axis=1), N_EXPERTS - 1).astype(jnp.int32)
    n_used = (padded_end[-1] // tm).astype(jnp.int32).reshape(1)
    rows_at = lambda table, idx: table.at[idx].get(mode="promise_in_bounds")
    d_e = w_gate.shape[2]
    tc = d_e // 2 if d_e % (2 * MXU_DIM) == 0 else d_e
    weights = [w.astype(jnp.bfloat16) for w in (w_gate, w_up, w_down)]
    n_stages = MOE_STAGES if n_blocks % MOE_STAGES == 0 else 1
    per_stage = n_blocks // n_stages
    y_slots = None
    for stage in range(n_stages):
        first = stage * per_stage
        xs = rows_at(h, slot_token[first * tm:(first + per_stage) * tm])
        y_slots = moe_experts(xs, first, n_blocks, y_slots, block_expert, n_used, *weights, tm, tc)
    out = x2
    for j in range(TOP_K):
        out = out + gates[:, j:j + 1] * rows_at(y_slots, slot[:, j]).astype(jnp.float32)
    return out


GDN_BLOCK = 512
GDN_GROUP = 2
CONV_PAD = 8
GDN_PAIRS = GDN_HEADS // 2
SELECT_TERMS = 3
PAIR_ROWS = 2 * GDN_CHUNK


def _gdn_kernel(aqkv_ref, az_ref, small_ref, convw_ref, aneg_ref, bias_ref, gnorm_ref,
                eeven_ref, eodd_ref, ebeta_ref, bd_ref, ltri_ref, o_ref,
                ext_s, state_s, q_s, k_s, kb_s, qd_s, rhs_s, attn_s, w_s, kdt_s,
                gc_s, kd_s, u_s, oh_s):
    f32, bf16 = jnp.float32, jnp.bfloat16
    c = GDN_CHUNK
    tb = aqkv_ref.shape[0]
    gw = GDN_WIDTH
    nc = tb // c

    @pl.when(pl.program_id(1) == 0)
    def _():
        ext_s[0:CONV_PAD, :] = jnp.zeros((CONV_PAD, 3 * gw), f32)
        state_s[...] = jnp.zeros_like(state_s)

    ext_s[CONV_PAD:CONV_PAD + tb, :] = aqkv_ref[...]
    conv = convw_ref[0:1, :] * ext_s[pl.ds(CONV_PAD - CONV_K + 1, tb), :]
    for j in range(1, CONV_K):
        conv += convw_ref[j:j + 1, :] * ext_s[pl.ds(CONV_PAD - CONV_K + 1 + j, tb), :]
    ext_s[0:CONV_PAD, :] = ext_s[tb:tb + CONV_PAD, :]
    qkv = conv * jax.nn.sigmoid(conv)
    q, k, v = qkv[:, :gw], qkv[:, gw:2 * gw], qkv[:, 2 * gw:]

    bd = bd_ref[...]
    q = q * lax.rsqrt(_group_sum_sq(q, bd) + RMS_EPS) * (HEAD_DIM ** -0.5)
    k = k * lax.rsqrt(_group_sum_sq(k, bd) + RMS_EPS)

    sm = small_ref[...]
    pre = sm + bias_ref[...]
    softplus = jnp.maximum(pre, 0.0) + jnp.log(1.0 + jnp.exp(-jnp.abs(pre)))
    g = _prefix_dot(ltri_ref[...], aneg_ref[...] * softplus, SELECT_TERMS)
    g_even = _select_dot(g, eeven_ref[...])
    g_odd = _select_dot(g, eodd_ref[...])
    bb = _select_dot(jax.nn.sigmoid(sm), ebeta_ref[...])
    lane = lax.broadcasted_iota(jnp.int32, (1, gw), 1)
    even_lane = (lane // HEAD_DIM) % 2 == 0
    gb = jnp.where(even_lane, g_even, g_odd)
    gb3 = gb.reshape(nc, c, gw)
    k_dec = (k.reshape(nc, c, gw) * jnp.exp(gb3[:, c - 1:c, :] - gb3)).reshape(tb, gw)
    eg = jnp.exp(gb)
    kb = k * bb

    def stack(x):
        x3 = x.reshape(nc, c, gw)
        zero = jnp.zeros_like(x3)
        return jnp.concatenate([jnp.where(even_lane, x3, zero), jnp.where(even_lane, zero, x3)], axis=1)

    q_s[...] = stack(q.astype(bf16))
    k_s[...] = stack(k.astype(bf16))
    kb_s[...] = stack(kb.astype(bf16))
    qd_s[...] = stack((q * eg).astype(bf16))
    kd_s[...] = stack(k_dec)
    vb_st = stack((v * bb).astype(bf16))
    kbe_st = stack((kb * eg).astype(bf16))
    for p in range(GDN_PAIRS):
        ps = slice(p * LANES, (p + 1) * LANES)
        rhs_s[:, :, 2 * p * LANES:(2 * p + 1) * LANES] = vb_st[:, :, ps]
        rhs_s[:, :, (2 * p + 1) * LANES:(2 * p + 2) * LANES] = kbe_st[:, :, ps]
    gc_s[...] = jnp.concatenate([g_even.reshape(nc, c, gw), g_odd.reshape(nc, c, gw)], axis=1)

    row = lax.broadcasted_iota(jnp.int32, (PAIR_ROWS, PAIR_ROWS), 0)
    col = lax.broadcasted_iota(jnp.int32, (PAIR_ROWS, PAIR_ROWS), 1)
    causal = (row // c == col // c) & (row >= col)
    strict = causal & (row > col)
    eye = (row == col).astype(f32)

    def phase_a(gi, carry):
        units = [(gi * GDN_GROUP + ci, p) for ci in range(GDN_GROUP) for p in range(GDN_PAIRS)]
        at = lambda ref, n, p: ref[n, :, p * LANES:(p + 1) * LANES]
        k_u = [at(k_s, n, p) for n, p in units]
        kk = [_dot_nt(at(kb_s, n, p), kx) for (n, p), kx in zip(units, k_u)]
        qk = [_dot_nt(at(q_s, n, p), kx) for (n, p), kx in zip(units, k_u)]
        gc = [at(gc_s, n, p) for n, p in units]
        dm = [jnp.where(causal, jnp.exp(jnp.minimum(gx - gx.T, 0.0)), 0.0) for gx in gc]
        a = [jnp.where(strict, x * d, 0.0) for x, d in zip(kk, dm)]
        for (n, p), x, d in zip(units, qk, dm):
            attn_s[n, :, p * LANES:(p + 1) * LANES] = (x * d).astype(bf16)
        for (n, p) in units:
            kdt_s[n, :, p * LANES:(p + 1) * LANES] = at(kd_s, n, p).T.astype(bf16)
        ab = [x.astype(bf16) for x in a]
        sq = [_dot(x, x) for x in ab]
        inv = [eye - x for x in a]
        for _ in range(4):
            sqb = [x.astype(bf16) for x in sq]
            both = [_dot(x, jnp.concatenate([x, t.astype(bf16)], axis=1)) for x, t in zip(sqb, inv)]
            sq = [x[:, :PAIR_ROWS] for x in both]
            inv = [t + x[:, PAIR_ROWS:] for t, x in zip(inv, both)]
        inv = [t + _dot(x.astype(bf16), t.astype(bf16)) for t, x in zip(inv, sq)]
        for (n, p), t in zip(units, inv):
            uw = _dot(t.astype(bf16), rhs_s[n, :, 2 * p * LANES:(2 * p + 2) * LANES])
            u_s[n, :, p * LANES:(p + 1) * LANES] = uw[:, :LANES]
            w_s[n, :, p * LANES:(p + 1) * LANES] = uw[:, LANES:].astype(bf16)
        return carry

    lax.fori_loop(0, nc // GDN_GROUP, phase_a, 0)

    def phase_b(n, carry):
        pairs = range(GDN_PAIRS)
        at = lambda ref, p: ref[n, :, p * LANES:(p + 1) * LANES]
        s = [state_s[p] for p in pairs]
        sb = [x.astype(bf16) for x in s]
        both = [_dot(jnp.concatenate([at(w_s, p), at(qd_s, p)], axis=0), sb[p]) for p in pairs]
        vn = [(at(u_s, p) - both[p][:PAIR_ROWS]).astype(bf16) for p in pairs]
        o_st = [both[p][PAIR_ROWS:] + _dot(at(attn_s, p), vn[p]) for p in pairs]
        upd = [_dot(at(kdt_s, p), vn[p]) for p in pairs]
        for p in pairs:
            ps = slice(p * LANES, (p + 1) * LANES)
            g_last = jnp.where(even_lane[:, :LANES], gc_s[n, c - 1:c, ps], gc_s[n, 2 * c - 1:2 * c, ps])
            state_s[p] = s[p] * jnp.exp(g_last) + upd[p]
            oh_s[pl.ds(pl.multiple_of(n * c, c), c), p * LANES:(p + 1) * LANES] = (
                o_st[p][:c] + o_st[p][c:])
        return carry

    lax.fori_loop(0, nc, phase_b, 0)

    o = oh_s[...]
    ms = _group_sum_sq(o, bd) * (1.0 / HEAD_DIM)
    z = az_ref[...]
    o_ref[...] = o * lax.rsqrt(ms + RMS_EPS) * gnorm_ref[...] * (z * jax.nn.sigmoid(z))


def gdn_mixer(proj, bsz, s_len, conv_w, a_log, dt_bias, gdn_norm):
    n = bsz * s_len
    gw, nh, c = GDN_WIDTH, GDN_HEADS, GDN_CHUNK
    tb = _row_tile(s_len, GDN_BLOCK)
    nt = s_len // tb
    nc = tb // c
    assert nc % GDN_GROUP == 0 and nh % 2 == 0
    f32, bf16 = jnp.float32, jnp.bfloat16
    aneg = jnp.zeros((1, LANES), f32).at[0, :nh].set(-jnp.exp(a_log))
    bias = jnp.zeros((1, LANES), f32).at[0, :nh].set(dt_bias)
    gnorm = jnp.tile(gdn_norm, nh).reshape(1, gw)
    lane_head = np.arange(gw) // HEAD_DIM
    src = np.arange(LANES)[:, None]
    stacked_sel = lambda m: jnp.asarray(np.tile(m, (SELECT_TERMS, 1)), bf16)
    eeven = stacked_sel(src == (lane_head // 2 * 2)[None, :])
    eodd = stacked_sel(src == (lane_head // 2 * 2 + 1)[None, :])
    ebeta = stacked_sel(src == nh + lane_head[None, :])
    bd = jnp.asarray(lane_head[:, None] == lane_head[None, :], bf16)
    ti = np.arange(tb)
    ltri = jnp.asarray((ti[:, None] // c == ti[None, :] // c) & (ti[None, :] <= ti[:, None]), bf16)
    rows = lambda width, cb: pl.BlockSpec((tb, width), lambda b, t: (b * nt + t, cb))
    stacked = lambda dt, width=gw: pltpu.VMEM((nc, PAIR_ROWS, width), dt)
    return pl.pallas_call(
        _gdn_kernel,
        grid=(bsz, nt),
        in_specs=[rows(3 * gw, 0), rows(gw, 3), rows(LANES, SMALL_OFF // LANES),
                  _resident((CONV_K, 3 * gw)), _resident((1, LANES)), _resident((1, LANES)),
                  _resident((1, gw)), _resident(eeven.shape), _resident(eodd.shape),
                  _resident(ebeta.shape), _resident((gw, gw)), _resident((tb, tb))],
        out_specs=pl.BlockSpec((tb, gw), lambda b, t: (b * nt + t, 0)),
        out_shape=jax.ShapeDtypeStruct((n, gw), f32),
        scratch_shapes=[
            pltpu.VMEM((tb + CONV_PAD, 3 * gw), f32),
            pltpu.VMEM((GDN_PAIRS, LANES, LANES), f32),
            stacked(bf16), stacked(bf16), stacked(bf16), stacked(bf16),
            stacked(bf16, 2 * gw),
            stacked(bf16), stacked(bf16), stacked(bf16),
            stacked(f32), stacked(f32), stacked(f32),
            pltpu.VMEM((tb, gw), f32)],
        compiler_params=_params("parallel", "arbitrary"),
        name="gdn_mixer",
    )(proj, proj, proj, conv_w, aneg, bias, gnorm, eeven, eodd, ebeta, bd, ltri)


SWA_SPAN = 128
SWA_PREP_ROWS = 512
SWA_GROUP = 4
MASKED_SCORE = -1e30


def _swa_kernel(q_ref, k_ref, v_ref, cos_ref, sin_ref, qn_ref, kn_ref, bd_ref, o_ref,
                q_s, k_s, v_s, op_s, lse_s):
    f32, bf16 = jnp.float32, jnp.bfloat16
    s_len = q_ref.shape[0]
    blk = SWA_SPAN
    lane = lax.broadcasted_iota(jnp.int32, (1, LANES), 1)
    first_half = (lane % HEAD_DIM) < (HEAD_DIM // 2)
    head0 = lane < HEAD_DIM
    bd = bd_ref[...]
    prep_rows = min(SWA_PREP_ROWS, s_len)

    def normed_rotary(x, gain, cos, sin, scale):
        ms = _group_sum_sq(x, bd) * (1.0 / HEAD_DIM)
        y = x * lax.rsqrt(ms + RMS_EPS) * gain
        partner = jnp.where(first_half, pltpu.roll(y, LANES - HEAD_DIM // 2, 1),
                            pltpu.roll(y, HEAD_DIM // 2, 1))
        return (y * cos + partner * sin) * scale

    def prologue(i, carry):
        rows = pl.ds(pl.multiple_of(i * prep_rows, prep_rows), prep_rows)
        cos, sin = cos_ref[rows, :], sin_ref[rows, :]
        q_s[rows, :] = normed_rotary(q_ref[rows, :], qn_ref[...], cos, sin, HEAD_DIM ** -0.5)
        k_s[rows, :] = normed_rotary(k_ref[rows, :], kn_ref[...], cos, sin, 1.0)
        v_s[rows, :] = v_ref[rows, :]
        return carry

    lax.fori_loop(0, s_len // prep_rows, prologue, 0)

    qi = lax.broadcasted_iota(jnp.int32, (blk, 2 * blk), 0)
    kj = lax.broadcasted_iota(jnp.int32, (blk, 2 * blk), 1)
    dist = blk + qi - kj
    band = (dist >= 0) & (dist <= SWA_SPAN)
    ones_kv = jnp.ones((2 * blk, LANES), bf16)

    for pidx, (window, dil) in enumerate(DILATED_PAIRS):
        n_blk = s_len // (dil * blk)

        def rows_of(r, i):
            start = r + i * (blk * dil)
            if dil == 1:
                return pl.ds(start, blk)
            return pl.ds(start, blk, stride=dil)

        def block_group(gi, carry):
            units = [gi * SWA_GROUP + j for j in range(SWA_GROUP)]
            ri = [(u // n_blk, u % n_blk) for u in units]
            cur = [rows_of(r, i) for r, i in ri]
            prev = [rows_of(r, jnp.maximum(i - 1, 0)) for r, i in ri]
            qb = [q_s[c, :] for c in cur]
            kw = [jnp.concatenate([k_s[pv, :], k_s[c, :]], axis=0).astype(bf16)
                  for pv, c in zip(prev, cur)]
            vw = [jnp.concatenate([jnp.concatenate([v_s[pv, :], v_s[c, :]], axis=0).astype(bf16),
                                   ones_kv], axis=1) for pv, c in zip(prev, cur)]
            valid = [band & (kj >= jnp.where(i > 0, 0, blk)) for _, i in ri]
            chains = [(j, h) for j in range(SWA_GROUP) for h in range(2)]
            qh = [jnp.where(head0 if h == 0 else ~head0, qb[j], 0.0).astype(bf16) for j, h in chains]
            s = [jnp.where(valid[j], _dot_nt(x, kw[j]), MASKED_SCORE) for (j, h), x in zip(chains, qh)]
            m = [jnp.max(x, axis=-1, keepdims=True) for x in s]
            p = [jnp.exp((x - mx).astype(bf16)) for x, mx in zip(s, m)]
            pv = [_dot(x, vw[j]) for (j, h), x in zip(chains, p)]
            o = [x[:, :LANES] / x[:, LANES:] for x in pv]
            lse = [mx + jnp.log(x[:, LANES:]) for mx, x in zip(m, pv)]
            for j in range(SWA_GROUP):
                op_s[pidx, cur[j], :] = jnp.where(head0, o[2 * j], o[2 * j + 1])
                lse_s[pidx, cur[j], :] = jnp.where(head0, lse[2 * j], lse[2 * j + 1])
            return carry

        lax.fori_loop(0, s_len // (blk * SWA_GROUP), block_group, 0)

    def epilogue(i, carry):
        rows = pl.ds(pl.multiple_of(i * prep_rows, prep_rows), prep_rows)
        l0, l1, l2 = lse_s[0, rows, :], lse_s[1, rows, :], lse_s[2, rows, :]
        m = jnp.maximum(jnp.maximum(l0, l1), l2)
        w0, w1, w2 = jnp.exp(l0 - m), jnp.exp(l1 - m), jnp.exp(l2 - m)
        mix = w0 * op_s[0, rows, :] + w1 * op_s[1, rows, :] + w2 * op_s[2, rows, :]
        o_ref[rows, :] = mix / (w0 + w1 + w2)
        return carry

    lax.fori_loop(0, s_len // prep_rows, epilogue, 0)


def swa_mixer(proj, bsz, s_len, q_norm, k_norm):
    assert all(w // d == SWA_SPAN for w, d in DILATED_PAIRS) and len(DILATED_PAIRS) == 3
    assert s_len % (SWA_SPAN * max(d for _, d in DILATED_PAIRS)) == 0
    n = bsz * s_len
    f32 = jnp.float32
    n_pairs = SWA_WIDTH // LANES
    half = HEAD_DIM // 2
    inv_freq = ROPE_THETA ** (-jnp.arange(0, HEAD_DIM, 2, dtype=f32) / HEAD_DIM)
    ang = jnp.arange(s_len, dtype=f32)[:, None] * inv_freq[None, :]
    cos = jnp.tile(jnp.cos(ang), (1, LANES // half))
    sin = jnp.tile(jnp.concatenate([-jnp.sin(ang), jnp.sin(ang)], axis=1), (1, LANES // HEAD_DIM))
    gain = lambda g: jnp.tile(g, LANES // HEAD_DIM).reshape(1, LANES)
    lane_head = np.arange(LANES) // HEAD_DIM
    bd = jnp.asarray(lane_head[:, None] == lane_head[None, :], jnp.bfloat16)
    q_col0 = 4 * GDN_WIDTH // LANES
    col = lambda which: pl.BlockSpec(
        (s_len, LANES), lambda b, p: (b, q_col0 + which * n_pairs + p))
    seq = pltpu.VMEM((s_len, LANES), f32)
    per_pattern = pltpu.VMEM((len(DILATED_PAIRS), s_len, LANES), f32)
    return pl.pallas_call(
        _swa_kernel,
        grid=(bsz, n_pairs),
        in_specs=[col(0), col(1), col(2), _resident((s_len, LANES)), _resident((s_len, LANES)),
                  _resident((1, LANES)), _resident((1, LANES)), _resident((LANES, LANES))],
        out_specs=pl.BlockSpec((s_len, LANES), lambda b, p: (b, p)),
        out_shape=jax.ShapeDtypeStruct((n, SWA_WIDTH), f32),
        scratch_shapes=[seq, seq, seq, per_pattern, per_pattern],
        compiler_params=_params("parallel", "parallel"),
        name="swa_mixer",
    )(proj, proj, proj, cos, sin, gain(q_norm), gain(k_norm), bd)


def kernel(x, norm_mix, w_in, conv_w, a_log, dt_bias, gdn_norm, q_norm, k_norm, pool_w,
           pool_scale, w_out, norm_ffn, ffn_gate, ffn_up, ffn_down, router_w, router_b,
           exp_gate, exp_up, exp_down):
    bsz, s_len, d = x.shape
    n = bsz * s_len
    depth = w_in.shape[0]
    x2 = x.reshape(n, d)
    for layer in range(depth):
        proj = norm_inproj(x2, norm_mix[layer], w_in, layer)
        ya = gdn_mixer(proj, bsz, s_len, conv_w[layer], a_log[layer], dt_bias[layer],
                       gdn_norm[layer])
        yb = swa_mixer(proj, bsz, s_len, q_norm[layer], k_norm[layer])
        x2 = pool_outproj_residual(x2, ya, yb, proj, bsz, s_len, pool_w[layer],
                                   pool_scale[layer], w_out[layer])
        i = layer // 2
        if layer % 2 == 0:
            x2 = ffn_dense(x2, norm_ffn[layer], ffn_gate[i], ffn_up[i], ffn_down[i])
        else:
            x2 = moe_block(x2, norm_ffn[layer], router_w[i], router_b[i], exp_gate[i],
                           exp_up[i], exp_down[i])
    return x2.reshape(bsz, s_len, d)
```

```python
import functools

import jax
import jax.numpy as jnp
import numpy as np
from jax import lax
from jax.experimental import pallas as pl
from jax.experimental.pallas import tpu as pltpu

D_MODEL = 1024
HEAD_DIM = 64
MIX_WIDTH = D_MODEL
POOL_WIDTH = MIX_WIDTH // 4
POOL_GROUPS = 4
POOL_GROUP_DIM = POOL_WIDTH // POOL_GROUPS
POOL_SIZES = (2, 4, 8, 16)
GDN_HEADS = (MIX_WIDTH - POOL_WIDTH) // (2 * HEAD_DIM)
GDN_WIDTH = GDN_HEADS * HEAD_DIM
SWA_HEADS = (MIX_WIDTH - POOL_WIDTH - GDN_WIDTH) // HEAD_DIM
SWA_WIDTH = SWA_HEADS * HEAD_DIM
CONV_K = 4
GDN_CHUNK = 64
DILATED_PAIRS = ((128, 1), (512, 4), (2048, 16))
ROPE_THETA = 10000.0
RMS_EPS = 1e-6
N_EXPERTS = 8
TOP_K = 2

VMEM_LIMIT_BYTES = 56 * 1024 * 1024
LANES = 128

SMALL_OFF = 7 * GDN_WIDTH
POOL_OFF = SMALL_OFF + LANES
PROJ_WIDTH = POOL_OFF + POOL_WIDTH


def _row_tile(n_rows, want):
    t = min(want, n_rows)
    while n_rows % t:
        t //= 2
    return t


def _params(*sem):
    return pltpu.CompilerParams(dimension_semantics=sem, vmem_limit_bytes=VMEM_LIMIT_BYTES)


def _resident(shape):
    zeros = (0,) * len(shape)
    return pl.BlockSpec(shape, lambda *_: zeros, pipeline_mode=pl.Buffered(1))


def _dot(a, b, precision=None):
    return jnp.dot(a, b, preferred_element_type=jnp.float32, precision=precision)


def _dot_nt(a, b):
    return lax.dot_general(a, b, (((1,), (1,)), ((), ())), preferred_element_type=jnp.float32)


def _bf16_pieces(x, terms):
    pieces, rest = [], x
    for t in range(terms):
        pieces.append(rest.astype(jnp.bfloat16))
        if t + 1 < terms:
            rest = rest - pieces[-1].astype(jnp.float32)
    return pieces


def _select_dot(x, sel_stacked):
    terms = sel_stacked.shape[0] // x.shape[1]
    return _dot(jnp.concatenate(_bf16_pieces(x, terms), axis=1), sel_stacked)


def _prefix_dot(sel, x, terms):
    width = x.shape[1]
    wide = _dot(sel, jnp.concatenate(_bf16_pieces(x, terms), axis=1))
    return sum(wide[:, t * width:(t + 1) * width] for t in range(terms))


def _group_sum_sq(x, group_ones):
    return _dot((x * x).astype(jnp.bfloat16), group_ones)


def _norm_inproj_kernel(x_ref, g_ref, w_ref, o_ref, wb_s):
    @pl.when(pl.program_id(0) == 0)
    def _():
        bf16 = jnp.bfloat16
        a_end = 4 * GDN_WIDTH
        b_start = a_end + 2 * GDN_HEADS
        b_end = b_start + 3 * SWA_WIDTH
        wb_s[:, :a_end] = w_ref[0, :, :a_end].astype(bf16)
        wb_s[:, a_end:SMALL_OFF] = w_ref[0, :, b_start:b_end].astype(bf16)
        lane = lax.broadcasted_iota(jnp.int32, (1, LANES), 1)
        wb_s[:, SMALL_OFF:POOL_OFF] = jnp.where(
            lane < 2 * GDN_HEADS, w_ref[0, :, a_end:a_end + LANES], 0.0).astype(bf16)
        wb_s[:, POOL_OFF:] = w_ref[0, :, b_end:].astype(bf16)

    x = x_ref[...]
    ms = jnp.mean(x * x, axis=-1, keepdims=True)
    h = (x * lax.rsqrt(ms + RMS_EPS) * g_ref[...]).astype(jnp.bfloat16)
    o_ref[...] = _dot(h, wb_s[...])


def norm_inproj(x2, g, w_in, layer):
    n, d = x2.shape
    in_width = w_in.shape[2]
    assert in_width == 4 * GDN_WIDTH + 2 * GDN_HEADS + 3 * SWA_WIDTH + POOL_WIDTH
    tm = _row_tile(n, 512)
    return pl.pallas_call(
        _norm_inproj_kernel,
        grid=(n // tm,),
        in_specs=[pl.BlockSpec((tm, d), lambda i: (i, 0)),
                  _resident((1, d)),
                  pl.BlockSpec((1, d, in_width), lambda i: (layer, 0, 0), pipeline_mode=pl.Buffered(1))],
        out_specs=pl.BlockSpec((tm, PROJ_WIDTH), lambda i: (i, 0)),
        out_shape=jax.ShapeDtypeStruct((n, PROJ_WIDTH), jnp.float32),
        scratch_shapes=[pltpu.VMEM((d, PROJ_WIDTH), jnp.bfloat16)],
        compiler_params=_params("arbitrary"),
        name="norm_inproj",
    )(x2, g.reshape(1, d), w_in)


POOL_HIST = max(POOL_SIZES)


def _pool_outproj_kernel(x_ref, ya_ref, yb_ref, cu_ref, size_ref, pw_ref, ps_ref,
                         w_ref, o_ref, ext_s):
    f32, bf16 = jnp.float32, jnp.bfloat16
    tm = x_ref.shape[0]
    t_blk = pl.program_id(1)

    @pl.when(t_blk == 0)
    def _():
        ext_s[0:POOL_HIST, :] = jnp.zeros((POOL_HIST, POOL_WIDTH), f32)

    cu = cu_ref[...]
    ext_s[POOL_HIST:POOL_HIST + tm, :] = cu
    ext = ext_s[...]
    ext_s[0:POOL_HIST, :] = ext_s[tm:tm + POOL_HIST, :]
    size = size_ref[...]
    win = ext
    total = jnp.zeros_like(ext)
    shift = 1
    while shift < POOL_HIST:
        win = win + pltpu.roll(win, shift, 0)
        shift *= 2
        total = jnp.where(size == shift, win, total)
    total = total[POOL_HIST:, :]
    pos = t_blk * tm + lax.broadcasted_iota(jnp.int32, (tm, 1), 0)
    count = jnp.minimum(pos + 1, size).astype(f32)
    pooled = total / count - cu
    yc = _dot(pooled.astype(bf16), pw_ref[...]) * ps_ref[...]

    mixed = jnp.concatenate([ya_ref[...].astype(bf16), yb_ref[...].astype(bf16), yc.astype(bf16)], axis=1)
    o_ref[...] = x_ref[...] + _dot(mixed, w_ref[...])


def pool_outproj_residual(x2, ya, yb, proj, bsz, s_len, pool_w, pool_scale, w_out):
    n, d = x2.shape
    tm = _row_tile(s_len, 1024)
    nt = s_len // tm
    bf16 = jnp.bfloat16
    size = jnp.asarray(np.repeat(np.asarray(POOL_SIZES, np.int32), POOL_GROUP_DIM).reshape(1, POOL_WIDTH))
    pw = jax.scipy.linalg.block_diag(*[pool_w[g] for g in range(POOL_GROUPS)]).astype(bf16)
    row = lambda width, cb=0: pl.BlockSpec((tm, width), lambda b, t: (b * nt + t, cb))
    return pl.pallas_call(
        _pool_outproj_kernel,
        grid=(bsz, nt),
        in_specs=[row(d), row(GDN_WIDTH), row(SWA_WIDTH), row(POOL_WIDTH, POOL_OFF // POOL_WIDTH),
                  _resident((1, POOL_WIDTH)), _resident(pw.shape), _resident((1, POOL_WIDTH)),
                  _resident(w_out.shape)],
        out_specs=row(d),
        out_shape=jax.ShapeDtypeStruct((n, d), jnp.float32),
        scratch_shapes=[pltpu.VMEM((tm + POOL_HIST, POOL_WIDTH), jnp.float32)],
        compiler_params=_params("parallel", "arbitrary"),
        name="pool_outproj_residual",
    )(x2, ya, yb, proj, size, pw, pool_scale.reshape(1, POOL_WIDTH), w_out.astype(bf16))


MXU_DIM = 256
HIDDEN_SLICE = 512


def _hidden_slices(width):
    assert width % MXU_DIM == 0
    return [(s, min(HIDDEN_SLICE, width - s)) for s in range(0, width, HIDDEN_SLICE)]


def _swiglu_slice(h, wg, wu, wd):
    gate = _dot(h, wg)
    up = _dot(h, wu)
    return _dot((gate * jax.nn.sigmoid(gate) * up).astype(jnp.bfloat16), wd)


def _ffn_kernel(x_ref, g_ref, wg_ref, wu_ref, wd_ref, o_ref):
    x = x_ref[...]
    ms = jnp.mean(x * x, axis=-1, keepdims=True)
    h = (x * lax.rsqrt(ms + RMS_EPS) * g_ref[...]).astype(jnp.bfloat16)
    o_ref[...] = x
    for s, w in _hidden_slices(wg_ref.shape[1]):
        o_ref[...] += _swiglu_slice(h, wg_ref[:, s:s + w], wu_ref[:, s:s + w], wd_ref[s:s + w, :])


def ffn_dense(x2, g, w_gate, w_up, w_down):
    n, d = x2.shape
    d_ff = w_gate.shape[1]
    tm = _row_tile(n, 512)
    row = pl.BlockSpec((tm, d), lambda i: (i, 0))
    return pl.pallas_call(
        _ffn_kernel,
        grid=(n // tm,),
        in_specs=[row, _resident((1, d)), _resident((d, d_ff)), _resident((d, d_ff)),
                  _resident((d_ff, d))],
        out_specs=row,
        out_shape=jax.ShapeDtypeStruct((n, d), jnp.float32),
        compiler_params=_params("parallel"),
        name="ffn_dense",
    )(x2, g.reshape(1, d), w_gate.astype(jnp.bfloat16), w_up.astype(jnp.bfloat16),
      w_down.astype(jnp.bfloat16))


NO_EXPERT = -1e30
ROUTE_EXPERT, ROUTE_GATE, ROUTE_RANK = 0, TOP_K, 2 * TOP_K


def _router_kernel(x_ref, g_ref, wr_ref, br_ref, ltri_ref, h_ref, route_ref, count_ref, count_s):
    f32 = jnp.float32
    tm = x_ref.shape[0]

    @pl.when(pl.program_id(0) == 0)
    def _():
        count_s[...] = jnp.zeros_like(count_s)

    x = x_ref[...]
    ms = jnp.mean(x * x, axis=-1, keepdims=True)
    h = x * lax.rsqrt(ms + RMS_EPS) * g_ref[...]
    h_hi, h_lo = _bf16_pieces(h, 2)
    h_ref[...] = h_hi
    logits = _dot(jnp.concatenate([h_hi, h_lo, h_hi], axis=1), wr_ref[...]) + br_ref[...]
    lane = lax.broadcasted_iota(jnp.int32, (tm, LANES), 1)

    def top(vals):
        best = jnp.max(vals, axis=-1, keepdims=True)
        return best, jnp.min(jnp.where(vals == best, lane, LANES), axis=-1, keepdims=True)

    m1, e1 = top(logits)
    m2, e2 = top(jnp.where(lane == e1, 2 * NO_EXPERT, logits))
    t = jnp.exp(m2 - m1)
    gate1 = 1.0 / (1.0 + t)
    gate2 = t / (1.0 + t)
    sel1, sel2 = lane == e1, lane == e2
    chosen = (sel1 | sel2).astype(f32)
    before = _dot(ltri_ref[...], chosen.astype(jnp.bfloat16)) + count_s[...]
    rank1 = jnp.sum(jnp.where(sel1, before, 0.0), axis=-1, keepdims=True)
    rank2 = jnp.sum(jnp.where(sel2, before, 0.0), axis=-1, keepdims=True)
    count_s[...] += jnp.sum(chosen, axis=0, keepdims=True)
    count_ref[...] = count_s[...]
    record = jnp.zeros((tm, LANES), f32)
    for pos, val in ((ROUTE_EXPERT, e1.astype(f32)), (ROUTE_EXPERT + 1, e2.astype(f32)),
                     (ROUTE_GATE, gate1), (ROUTE_GATE + 1, gate2),
                     (ROUTE_RANK, rank1), (ROUTE_RANK + 1, rank2)):
        record = jnp.where(lane == pos, val, record)
    route_ref[...] = record


def norm_router(x2, g, router_w, router_b):
    n, d = x2.shape
    tm = _row_tile(n, 1024)
    w_hi, w_lo = _bf16_pieces(jnp.pad(router_w, ((0, 0), (0, LANES - N_EXPERTS))), 2)
    wr = jnp.concatenate([w_hi, w_hi, w_lo], axis=0)
    br = jnp.pad(router_b, (0, LANES - N_EXPERTS), constant_values=NO_EXPERT).reshape(1, LANES)
    ti = np.arange(tm)
    ltri = jnp.asarray(ti[None, :] < ti[:, None], jnp.bfloat16)
    return pl.pallas_call(
        _router_kernel,
        grid=(n // tm,),
        in_specs=[pl.BlockSpec((tm, d), lambda i: (i, 0)), _resident((1, d)),
                  _resident(wr.shape), _resident((1, LANES)), _resident((tm, tm))],
        out_specs=[pl.BlockSpec((tm, d), lambda i: (i, 0)),
                   pl.BlockSpec((tm, LANES), lambda i: (i, 0)),
                   pl.BlockSpec((1, LANES), lambda i: (0, 0))],
        out_shape=[jax.ShapeDtypeStruct((n, d), jnp.bfloat16),
                   jax.ShapeDtypeStruct((n, LANES), jnp.float32),
                   jax.ShapeDtypeStruct((1, LANES), jnp.float32)],
        scratch_shapes=[pltpu.VMEM((1, LANES), jnp.float32)],
        compiler_params=_params("arbitrary"),
        name="norm_router",
    )(x2, g.reshape(1, d), wr, br, ltri)


def _moe_kernel(first, be_ref, nb_ref, xs_ref, wg_ref, wu_ref, wd_ref, *refs):
    o_ref, acc_s = refs[-2:]
    j = first + pl.program_id(0)
    c = pl.program_id(1)
    last = pl.num_programs(1) - 1

    @pl.when(j < nb_ref[0])
    def _():
        h = xs_ref[...]

        @pl.when(c == 0)
        def _():
            acc_s[...] = jnp.zeros_like(acc_s)

        for s, w in _hidden_slices(wg_ref.shape[2]):
            acc_s[...] += _swiglu_slice(h, wg_ref[0, :, s:s + w], wu_ref[0, :, s:s + w],
                                        wd_ref[0, s:s + w, :])

        @pl.when(c == last)
        def _():
            o_ref[...] = acc_s[...].astype(o_ref.dtype)

    @pl.when(jnp.logical_and(j >= nb_ref[0], c == last))
    def _():
        o_ref[...] = jnp.zeros_like(o_ref)


def moe_experts(xs, first, n_blocks_total, y_prev, block_expert, n_used, w_gate, w_up, w_down, tm, tc):
    d = xs.shape[1]
    n_here = xs.shape[0] // tm
    d_e = w_gate.shape[2]
    n_c = d_e // tc

    def live(j, nb):
        return jnp.clip(jnp.minimum(first + j, nb[0] - 1), first, first + n_here - 1)

    def chunk(j, c, nb):
        return jnp.where(first + j < nb[0], c, n_c - 1)

    in_specs = [
        pl.BlockSpec((tm, d), lambda j, c, be, nb: (live(j, nb) - first, 0)),
        pl.BlockSpec((1, d, tc), lambda j, c, be, nb: (be[live(j, nb)], 0, chunk(j, c, nb))),
        pl.BlockSpec((1, d, tc), lambda j, c, be, nb: (be[live(j, nb)], 0, chunk(j, c, nb))),
        pl.BlockSpec((1, tc, d), lambda j, c, be, nb: (be[live(j, nb)], chunk(j, c, nb), 0)),
    ]
    operands = [block_expert, n_used, xs, w_gate, w_up, w_down]
    aliases = {}
    if y_prev is not None:
        in_specs.append(pl.BlockSpec(memory_space=pl.ANY))
        aliases = {len(operands): 0}
        operands.append(y_prev)
    return pl.pallas_call(
        functools.partial(_moe_kernel, first),
        grid_spec=pltpu.PrefetchScalarGridSpec(
            num_scalar_prefetch=2,
            grid=(n_here, n_c),
            in_specs=in_specs,
            out_specs=pl.BlockSpec((tm, d), lambda j, c, be, nb: (first + j, 0)),
            scratch_shapes=[pltpu.VMEM((tm, d), jnp.float32)],
        ),
        out_shape=jax.ShapeDtypeStruct((n_blocks_total * tm, d), jnp.bfloat16),
        input_output_aliases=aliases,
        compiler_params=_params("arbitrary", "arbitrary"),
        name="moe_experts",
    )(*operands)


MOE_STAGES = 2


def moe_block(x2, g, router_w, router_b, w_gate, w_up, w_down):
    n, d = x2.shape
    tm = _row_tile(n, 1024)
    h, route, count = norm_router(x2, g, router_w, router_b)
    expert = route[:, ROUTE_EXPERT:ROUTE_EXPERT + TOP_K].astype(jnp.int32)
    gates = route[:, ROUTE_GATE:ROUTE_GATE + TOP_K]
    rank = route[:, ROUTE_RANK:ROUTE_RANK + TOP_K].astype(jnp.int32)
    counts = count[0, :N_EXPERTS].astype(jnp.int32)
    padded = (counts + tm - 1) // tm * tm
    padded_end = jnp.cumsum(padded)
    padded_start = padded_end - padded
    slot = jnp.take(padded_start, expert) + rank
    n_blocks = -(-(n * TOP_K) // tm) + N_EXPERTS
    n_slots = n_blocks * tm
    token = jnp.broadcast_to(jnp.arange(n, dtype=jnp.int32)[:, None], (n, TOP_K))
    slot_token = jnp.zeros((n_slots,), jnp.int32).at[slot.reshape(-1)].set(
        token.reshape(-1), unique_indices=True, mode="promise_in_bounds")
    block_start = jnp.arange(n_blocks) * tm
    block_expert = jnp.minimum(
        jnp.sum(block_start[:, None] >= padded_end[None, :], axis=1), N_EXPERTS - 1).astype(jnp.int32)
    n_used = (padded_end[-1] // tm).astype(jnp.int32).reshape(1)
    rows_at = lambda table, idx: table.at[idx].get(mode="promise_in_bounds")
    d_e = w_gate.shape[2]
    tc = d_e // 2 if d_e % (2 * MXU_DIM) == 0 else d_e
    weights = [w.astype(jnp.bfloat16) for w in (w_gate, w_up, w_down)]
    n_stages = MOE_STAGES if n_blocks % MOE_STAGES == 0 else 1
    per_stage = n_blocks // n_stages
    y_slots = None
    for stage in range(n_stages):
        first = stage * per_stage
        xs = rows_at(h, slot_token[first * tm:(first + per_stage) * tm])
        y_slots = moe_experts(xs, first, n_blocks, y_slots, block_expert, n_used, *weights, tm, tc)
    out = x2
    for j in range(TOP_K):
        out = out + gates[:, j:j + 1] * rows_at(y_slots, slot[:, j]).astype(jnp.float32)
    return out


GDN_BLOCK = 512
GDN_GROUP = 4
CONV_PAD = 8
GDN_PAIRS = GDN_HEADS // 2
SELECT_TERMS = 3
PAIR_ROWS = 2 * GDN_CHUNK


def _gdn_kernel(aqkv_ref, az_ref, small_ref, convw_ref, aneg_ref, bias_ref, gnorm_ref,
                eeven_ref, eodd_ref, ebeta_ref, bd_ref, ltri_ref, o_ref,
                ext_s, state_s, q_s, k_s, kb_s, qd_s, rhs_s, attn_s, w_s, kdt_s,
                gc_s, kd_s, u_s, oh_s):
    f32, bf16 = jnp.float32, jnp.bfloat16
    c = GDN_CHUNK
    tb = aqkv_ref.shape[0]
    gw = GDN_WIDTH
    nc = tb // c

    @pl.when(pl.program_id(1) == 0)
    def _():
        ext_s[0:CONV_PAD, :] = jnp.zeros((CONV_PAD, 3 * gw), f32)
        state_s[...] = jnp.zeros_like(state_s)

    ext_s[CONV_PAD:CONV_PAD + tb, :] = aqkv_ref[...]
    conv = convw_ref[0:1, :] * ext_s[pl.ds(CONV_PAD - CONV_K + 1, tb), :]
    for j in range(1, CONV_K):
        conv += convw_ref[j:j + 1, :] * ext_s[pl.ds(CONV_PAD - CONV_K + 1 + j, tb), :]
    ext_s[0:CONV_PAD, :] = ext_s[tb:tb + CONV_PAD, :]
    qkv = conv * jax.nn.sigmoid(conv)
    q, k, v = qkv[:, :gw], qkv[:, gw:2 * gw], qkv[:, 2 * gw:]

    bd = bd_ref[...]
    q = q * lax.rsqrt(_group_sum_sq(q, bd) + RMS_EPS) * (HEAD_DIM ** -0.5)
    k = k * lax.rsqrt(_group_sum_sq(k, bd) + RMS_EPS)

    sm = small_ref[...]
    pre = sm + bias_ref[...]
    softplus = jnp.maximum(pre, 0.0) + jnp.log(1.0 + jnp.exp(-jnp.abs(pre)))
    g = _prefix_dot(ltri_ref[...], aneg_ref[...] * softplus, SELECT_TERMS)
    g_even = _select_dot(g, eeven_ref[...])
    g_odd = _select_dot(g, eodd_ref[...])
    bb = _select_dot(jax.nn.sigmoid(sm), ebeta_ref[...])
    lane = lax.broadcasted_iota(jnp.int32, (1, gw), 1)
    even_lane = (lane // HEAD_DIM) % 2 == 0
    gb = jnp.where(even_lane, g_even, g_odd)
    gb3 = gb.reshape(nc, c, gw)
    k_dec = (k.reshape(nc, c, gw) * jnp.exp(gb3[:, c - 1:c, :] - gb3)).reshape(tb, gw)
    eg = jnp.exp(gb)
    kb = k * bb

    def stack(x):
        x3 = x.reshape(nc, c, gw)
        zero = jnp.zeros_like(x3)
        return jnp.concatenate([jnp.where(even_lane, x3, zero), jnp.where(even_lane, zero, x3)], axis=1)

    q_s[...] = stack(q.astype(bf16))
    k_s[...] = stack(k.astype(bf16))
    kb_s[...] = stack(kb.astype(bf16))
    qd_s[...] = stack((q * eg).astype(bf16))
    kd_s[...] = stack(k_dec)
    vb_st = stack((v * bb).astype(bf16))
    kbe_st = stack((kb * eg).astype(bf16))
    for p in range(GDN_PAIRS):
        ps = slice(p * LANES, (p + 1) * LANES)
        rhs_s[:, :, 2 * p * LANES:(2 * p + 1) * LANES] = vb_st[:, :, ps]
        rhs_s[:, :, (2 * p + 1) * LANES:(2 * p + 2) * LANES] = kbe_st[:, :, ps]
    gc_s[...] = jnp.concatenate([g_even.reshape(nc, c, gw), g_odd.reshape(nc, c, gw)], axis=1)

    row = lax.broadcasted_iota(jnp.int32, (PAIR_ROWS, PAIR_ROWS), 0)
    col = lax.broadcasted_iota(jnp.int32, (PAIR_ROWS, PAIR_ROWS), 1)
    causal = (row // c == col // c) & (row >= col)
    strict = causal & (row > col)
    eye = (row == col).astype(f32)

    def phase_a(gi):
        units = [(gi * GDN_GROUP + ci, p) for ci in range(GDN_GROUP) for p in range(GDN_PAIRS)]
        at = lambda ref, n, p: ref[n, :, p * LANES:(p + 1) * LANES]
        k_u = [at(k_s, n, p) for n, p in units]
        kk = [_dot_nt(at(kb_s, n, p), kx) for (n, p), kx in zip(units, k_u)]
        qk = [_dot_nt(at(q_s, n, p), kx) for (n, p), kx in zip(units, k_u)]
        gc = [at(gc_s, n, p) for n, p in units]
        dm = [jnp.where(causal, jnp.exp(jnp.minimum(gx - gx.T, 0.0)), 0.0) for gx in gc]
        a = [jnp.where(strict, x * d, 0.0) for x, d in zip(kk, dm)]
        for (n, p), x, d in zip(units, qk, dm):
            attn_s[n, :, p * LANES:(p + 1) * LANES] = (x * d).astype(bf16)
        for (n, p) in units:
            kdt_s[n, :, p * LANES:(p + 1) * LANES] = at(kd_s, n, p).T.astype(bf16)
        ab = [x.astype(bf16) for x in a]
        sq = [_dot(x, x) for x in ab]
        inv = [eye - x for x in a]
        for _ in range(4):
            sqb = [x.astype(bf16) for x in sq]
            both = [_dot(x, jnp.concatenate([x, t.astype(bf16)], axis=1)) for x, t in zip(sqb, inv)]
            sq = [x[:, :PAIR_ROWS] for x in both]
            inv = [t + x[:, PAIR_ROWS:] for t, x in zip(inv, both)]
        inv = [t + _dot(x.astype(bf16), t.astype(bf16)) for t, x in zip(inv, sq)]
        for (n, p), t in zip(units, inv):
            uw = _dot(t.astype(bf16), rhs_s[n, :, 2 * p * LANES:(2 * p + 2) * LANES])
            u_s[n, :, p * LANES:(p + 1) * LANES] = uw[:, :LANES]
            w_s[n, :, p * LANES:(p + 1) * LANES] = uw[:, LANES:].astype(bf16)

    pairs = range(GDN_PAIRS)

    def phase_b(n, s):
        at = lambda ref, p: ref[n, :, p * LANES:(p + 1) * LANES]
        sb = [x.astype(bf16) for x in s]
        both = [_dot(jnp.concatenate([at(w_s, p), at(qd_s, p)], axis=0), sb[p]) for p in pairs]
        vn = [(at(u_s, p) - both[p][:PAIR_ROWS]).astype(bf16) for p in pairs]
        o_st = [both[p][PAIR_ROWS:] + _dot(at(attn_s, p), vn[p]) for p in pairs]
        upd = [_dot(at(kdt_s, p), vn[p]) for p in pairs]
        new = []
        for p in pairs:
            ps = slice(p * LANES, (p + 1) * LANES)
            g_last = jnp.where(even_lane[:, :LANES], gc_s[n, c - 1:c, ps], gc_s[n, 2 * c - 1:2 * c, ps])
            new.append(s[p] * jnp.exp(g_last) + upd[p])
            oh_s[n * c:(n + 1) * c, ps] = o_st[p][:c] + o_st[p][c:]
        return new

    n_groups = nc // GDN_GROUP
    state = [state_s[p] for p in pairs]
    phase_a(0)
    for gi in range(1, n_groups + 1):
        if gi < n_groups:
            phase_a(gi)
        for n in range((gi - 1) * GDN_GROUP, gi * GDN_GROUP):
            state = phase_b(n, state)
    for p in pairs:
        state_s[p] = state[p]

    o = oh_s[...]
    ms = _group_sum_sq(o, bd) * (1.0 / HEAD_DIM)
    z = az_ref[...]
    o_ref[...] = o * lax.rsqrt(ms + RMS_EPS) * gnorm_ref[...] * (z * jax.nn.sigmoid(z))


def gdn_mixer(proj, bsz, s_len, conv_w, a_log, dt_bias, gdn_norm):
    n = bsz * s_len
    gw, nh, c = GDN_WIDTH, GDN_HEADS, GDN_CHUNK
    tb = _row_tile(s_len, GDN_BLOCK)
    nt = s_len // tb
    nc = tb // c
    assert nc % GDN_GROUP == 0 and nh % 2 == 0
    f32, bf16 = jnp.float32, jnp.bfloat16
    aneg = jnp.zeros((1, LANES), f32).at[0, :nh].set(-jnp.exp(a_log))
    bias = jnp.zeros((1, LANES), f32).at[0, :nh].set(dt_bias)
    gnorm = jnp.tile(gdn_norm, nh).reshape(1, gw)
    lane_head = np.arange(gw) // HEAD_DIM
    src = np.arange(LANES)[:, None]
    stacked_sel = lambda m: jnp.asarray(np.tile(m, (SELECT_TERMS, 1)), bf16)
    eeven = stacked_sel(src == (lane_head // 2 * 2)[None, :])
    eodd = stacked_sel(src == (lane_head // 2 * 2 + 1)[None, :])
    ebeta = stacked_sel(src == nh + lane_head[None, :])
    bd = jnp.asarray(lane_head[:, None] == lane_head[None, :], bf16)
    ti = np.arange(tb)
    ltri = jnp.asarray((ti[:, None] // c == ti[None, :] // c) & (ti[None, :] <= ti[:, None]), bf16)
    rows = lambda width, cb: pl.BlockSpec((tb, width), lambda b, t: (b * nt + t, cb))
    stacked = lambda dt, width=gw: pltpu.VMEM((nc, PAIR_ROWS, width), dt)
    return pl.pallas_call(
        _gdn_kernel,
        grid=(bsz, nt),
        in_specs=[rows(3 * gw, 0), rows(gw, 3), rows(LANES, SMALL_OFF // LANES),
                  _resident((CONV_K, 3 * gw)), _resident((1, LANES)), _resident((1, LANES)),
                  _resident((1, gw)), _resident(eeven.shape), _resident(eodd.shape),
                  _resident(ebeta.shape), _resident((gw, gw)), _resident((tb, tb))],
        out_specs=pl.BlockSpec((tb, gw), lambda b, t: (b * nt + t, 0)),
        out_shape=jax.ShapeDtypeStruct((n, gw), f32),
        scratch_shapes=[
            pltpu.VMEM((tb + CONV_PAD, 3 * gw), f32),
            pltpu.VMEM((GDN_PAIRS, LANES, LANES), f32),
            stacked(bf16), stacked(bf16), stacked(bf16), stacked(bf16),
            stacked(bf16, 2 * gw),
            stacked(bf16), stacked(bf16), stacked(bf16),
            stacked(f32), stacked(f32), stacked(f32),
            pltpu.VMEM((tb, gw), f32)],
        compiler_params=_params("parallel", "arbitrary"),
        name="gdn_mixer",
    )(proj, proj, proj, conv_w, aneg, bias, gnorm, eeven, eodd, ebeta, bd, ltri)


SWA_SPAN = 128
SWA_PREP_ROWS = 512
SWA_GROUP = 4
MASKED_SCORE = -1e30


def _swa_kernel(q_ref, k_ref, v_ref, cos_ref, sin_ref, qn_ref, kn_ref, bd_ref, o_ref,
                q_s, k_s, v_s, op_s, lse_s):
    f32, bf16 = jnp.float32, jnp.bfloat16
    s_len = q_ref.shape[0]
    blk = SWA_SPAN
    lane = lax.broadcasted_iota(jnp.int32, (1, LANES), 1)
    first_half = (lane % HEAD_DIM) < (HEAD_DIM // 2)
    head0 = lane < HEAD_DIM
    bd = bd_ref[...]
    prep_rows = min(SWA_PREP_ROWS, s_len)

    def normed_rotary(x, gain, cos, sin, scale):
        ms = _group_sum_sq(x, bd) * (1.0 / HEAD_DIM)
        y = x * lax.rsqrt(ms + RMS_EPS) * gain
        partner = jnp.where(first_half, pltpu.roll(y, LANES - HEAD_DIM // 2, 1),
                            pltpu.roll(y, HEAD_DIM // 2, 1))
        return (y * cos + partner * sin) * scale

    def prologue(i, carry):
        rows = pl.ds(pl.multiple_of(i * prep_rows, prep_rows), prep_rows)
        cos, sin = cos_ref[rows, :], sin_ref[rows, :]
        q_s[rows, :] = normed_rotary(q_ref[rows, :], qn_ref[...], cos, sin, HEAD_DIM ** -0.5)
        k_s[rows, :] = normed_rotary(k_ref[rows, :], kn_ref[...], cos, sin, 1.0)
        v_s[rows, :] = v_ref[rows, :]
        return carry

    lax.fori_loop(0, s_len // prep_rows, prologue, 0)

    qi = lax.broadcasted_iota(jnp.int32, (blk, 2 * blk), 0)
    kj = lax.broadcasted_iota(jnp.int32, (blk, 2 * blk), 1)
    dist = blk + qi - kj
    band = (dist >= 0) & (dist <= SWA_SPAN)
    ones_kv = jnp.ones((2 * blk, LANES), bf16)

    for pidx, (window, dil) in enumerate(DILATED_PAIRS):
        n_blk = s_len // (dil * blk)

        def rows_of(r, i):
            start = r + i * (blk * dil)
            if dil == 1:
                return pl.ds(start, blk)
            return pl.ds(start, blk, stride=dil)

        def block_group(gi, carry):
            units = [gi * SWA_GROUP + j for j in range(SWA_GROUP)]
            ri = [(u // n_blk, u % n_blk) for u in units]
            cur = [rows_of(r, i) for r, i in ri]
            prev = [rows_of(r, jnp.maximum(i - 1, 0)) for r, i in ri]
            qb = [q_s[c, :] for c in cur]
            kw = [jnp.concatenate([k_s[pv, :], k_s[c, :]], axis=0).astype(bf16)
                  for pv, c in zip(prev, cur)]
            vw = [jnp.concatenate([jnp.concatenate([v_s[pv, :], v_s[c, :]], axis=0).astype(bf16),
                                   ones_kv], axis=1) for pv, c in zip(prev, cur)]
            valid = [band & (kj >= jnp.where(i > 0, 0, blk)) for _, i in ri]
            chains = [(j, h) for j in range(SWA_GROUP) for h in range(2)]
            qh = [jnp.where(head0 if h == 0 else ~head0, qb[j], 0.0).astype(bf16) for j, h in chains]
            s = [jnp.where(valid[j], _dot_nt(x, kw[j]), MASKED_SCORE) for (j, h), x in zip(chains, qh)]
            m = [jnp.max(x, axis=-1, keepdims=True) for x in s]
            p = [jnp.exp((x - mx).astype(bf16)) for x, mx in zip(s, m)]
            pv = [_dot(x, vw[j]) for (j, h), x in zip(chains, p)]
            o = [x[:, :LANES] / x[:, LANES:] for x in pv]
            lse = [mx + jnp.log(x[:, LANES:]) for mx, x in zip(m, pv)]
            for j in range(SWA_GROUP):
                op_s[pidx, cur[j], :] = jnp.where(head0, o[2 * j], o[2 * j + 1])
                lse_s[pidx, cur[j], :] = jnp.where(head0, lse[2 * j], lse[2 * j + 1])
            return carry

        lax.fori_loop(0, s_len // (blk * SWA_GROUP), block_group, 0)

    def epilogue(i, carry):
        rows = pl.ds(pl.multiple_of(i * prep_rows, prep_rows), prep_rows)
        l0, l1, l2 = lse_s[0, rows, :], lse_s[1, rows, :], lse_s[2, rows, :]
        m = jnp.maximum(jnp.maximum(l0, l1), l2)
        w0, w1, w2 = jnp.exp(l0 - m), jnp.exp(l1 - m), jnp.exp(l2 - m)
        mix = w0 * op_s[0, rows, :] + w1 * op_s[1, rows, :] + w2 * op_s[2, rows, :]
        o_ref[rows, :] = mix / (w0 + w1 + w2)
        return carry

    lax.fori_loop(0, s_len // prep_rows, epilogue, 0)


def swa_mixer(proj, bsz, s_len, q_norm, k_norm):
    assert all(w // d == SWA_SPAN for w, d in DILATED_PAIRS) and len(DILATED_PAIRS) == 3
    assert s_len % (SWA_SPAN * max(d for _, d in DILATED_PAIRS)) == 0
    n = bsz * s_len
    f32 = jnp.float32
    n_pairs = SWA_WIDTH // LANES
    half = HEAD_DIM // 2
    inv_freq = ROPE_THETA ** (-jnp.arange(0, HEAD_DIM, 2, dtype=f32) / HEAD_DIM)
    ang = jnp.arange(s_len, dtype=f32)[:, None] * inv_freq[None, :]
    cos = jnp.tile(jnp.cos(ang), (1, LANES // half))
    sin = jnp.tile(jnp.concatenate([-jnp.sin(ang), jnp.sin(ang)], axis=1), (1, LANES // HEAD_DIM))
    gain = lambda g: jnp.tile(g, LANES // HEAD_DIM).reshape(1, LANES)
    lane_head = np.arange(LANES) // HEAD_DIM
    bd = jnp.asarray(lane_head[:, None] == lane_head[None, :], jnp.bfloat16)
    q_col0 = 4 * GDN_WIDTH // LANES
    col = lambda which: pl.BlockSpec(
        (s_len, LANES), lambda b, p: (b, q_col0 + which * n_pairs + p))
    seq = pltpu.VMEM((s_len, LANES), f32)
    per_pattern = pltpu.VMEM((len(DILATED_PAIRS), s_len, LANES), f32)
    return pl.pallas_call(
        _swa_kernel,
        grid=(bsz, n_pairs),
        in_specs=[col(0), col(1), col(2), _resident((s_len, LANES)), _resident((s_len, LANES)),
                  _resident((1, LANES)), _resident((1, LANES)), _resident((LANES, LANES))],
        out_specs=pl.BlockSpec((s_len, LANES), lambda b, p: (b, p)),
        out_shape=jax.ShapeDtypeStruct((n, SWA_WIDTH), f32),
        scratch_shapes=[seq, seq, seq, per_pattern, per_pattern],
        compiler_params=_params("parallel", "parallel"),
        name="swa_mixer",
    )(proj, proj, proj, cos, sin, gain(q_norm), gain(k_norm), bd)


def kernel(x, norm_mix, w_in, conv_w, a_log, dt_bias, gdn_norm, q_norm, k_norm, pool_w,
           pool_scale, w_out, norm_ffn, ffn_gate, ffn_up, ffn_down, router_w, router_b,
           exp_gate, exp_up, exp_down):
    bsz, s_len, d = x.shape
    n = bsz * s_len
    depth = w_in.shape[0]
    x2 = x.reshape(n, d)
    for layer in range(depth):
        proj = norm_inproj(x2, norm_mix[layer], w_in, layer)
        ya = gdn_mixer(proj, bsz, s_len, conv_w[layer], a_log[layer], dt_bias[layer],
                       gdn_norm[layer])
        yb = swa_mixer(proj, bsz, s_len, q_norm[layer], k_norm[layer])
        x2 = pool_outproj_residual(x2, ya, yb, proj, bsz, s_len, pool_w[layer],
                                   pool_scale[layer], w_out[layer])
        i = layer // 2
        if layer % 2 == 0:
            x2 = ffn_dense(x2, norm_ffn[layer], ffn_gate[i], ffn_up[i], ffn_down[i])
        else:
            x2 = moe_block(x2, norm_ffn[layer], router_w[i], router_b[i], exp_gate[i],
                           exp_up[i], exp_down[i])
    return x2.reshape(bsz, s_len, d)
```

```python
import functools

import jax
import jax.numpy as jnp
import numpy as np
from jax import lax
from jax.experimental import pallas as pl
from jax.experimental.pallas import tpu as pltpu

D_MODEL = 1024
HEAD_DIM = 64
MIX_WIDTH = D_MODEL
POOL_WIDTH = MIX_WIDTH // 4
POOL_GROUPS = 4
POOL_GROUP_DIM = POOL_WIDTH // POOL_GROUPS
POOL_SIZES = (2, 4, 8, 16)
GDN_HEADS = (MIX_WIDTH - POOL_WIDTH) // (2 * HEAD_DIM)
GDN_WIDTH = GDN_HEADS * HEAD_DIM
SWA_HEADS = (MIX_WIDTH - POOL_WIDTH - GDN_WIDTH) // HEAD_DIM
SWA_WIDTH = SWA_HEADS * HEAD_DIM
CONV_K = 4
GDN_CHUNK = 64
DILATED_PAIRS = ((128, 1), (512, 4), (2048, 16))
ROPE_THETA = 10000.0
RMS_EPS = 1e-6
N_EXPERTS = 8
TOP_K = 2

VMEM_LIMIT_BYTES = 56 * 1024 * 1024
LANES = 128

SMALL_OFF = 7 * GDN_WIDTH
POOL_OFF = SMALL_OFF + LANES
PROJ_WIDTH = POOL_OFF + POOL_WIDTH


def _row_tile(n_rows, want):
    t = min(want, n_rows)
    while n_rows % t:
        t //= 2
    return t


def _params(*sem):
    return pltpu.CompilerParams(dimension_semantics=sem, vmem_limit_bytes=VMEM_LIMIT_BYTES)


def _resident(shape):
    zeros = (0,) * len(shape)
    return pl.BlockSpec(shape, lambda *_: zeros, pipeline_mode=pl.Buffered(1))


def _dot(a, b, precision=None):
    return jnp.dot(a, b, preferred_element_type=jnp.float32, precision=precision)


def _dot_nt(a, b):
    return lax.dot_general(a, b, (((1,), (1,)), ((), ())), preferred_element_type=jnp.float32)


def _bf16_pieces(x, terms):
    pieces, rest = [], x
    for t in range(terms):
        pieces.append(rest.astype(jnp.bfloat16))
        if t + 1 < terms:
            rest = rest - pieces[-1].astype(jnp.float32)
    return pieces


def _select_dot(x, sel_stacked):
    terms = sel_stacked.shape[0] // x.shape[1]
    return _dot(jnp.concatenate(_bf16_pieces(x, terms), axis=1), sel_stacked)


def _prefix_dot(sel, x, terms):
    width = x.shape[1]
    wide = _dot(sel, jnp.concatenate(_bf16_pieces(x, terms), axis=1))
    return sum(wide[:, t * width:(t + 1) * width] for t in range(terms))


def _group_sum_sq(x, group_ones):
    return _dot((x * x).astype(jnp.bfloat16), group_ones)


def _norm_inproj_kernel(x_ref, g_ref, w_ref, o_ref, wb_s):
    @pl.when(pl.program_id(0) == 0)
    def _():
        bf16 = jnp.bfloat16
        a_end = 4 * GDN_WIDTH
        b_start = a_end + 2 * GDN_HEADS
        b_end = b_start + 3 * SWA_WIDTH
        wb_s[:, :a_end] = w_ref[0, :, :a_end].astype(bf16)
        wb_s[:, a_end:SMALL_OFF] = w_ref[0, :, b_start:b_end].astype(bf16)
        lane = lax.broadcasted_iota(jnp.int32, (1, LANES), 1)
        wb_s[:, SMALL_OFF:POOL_OFF] = jnp.where(
            lane < 2 * GDN_HEADS, w_ref[0, :, a_end:a_end + LANES], 0.0).astype(bf16)
        wb_s[:, POOL_OFF:] = w_ref[0, :, b_end:].astype(bf16)

    x = x_ref[...]
    ms = jnp.mean(x * x, axis=-1, keepdims=True)
    h = (x * lax.rsqrt(ms + RMS_EPS) * g_ref[...]).astype(jnp.bfloat16)
    o_ref[...] = _dot(h, wb_s[...])


def norm_inproj(x2, g, w_in, layer):
    n, d = x2.shape
    in_width = w_in.shape[2]
    assert in_width == 4 * GDN_WIDTH + 2 * GDN_HEADS + 3 * SWA_WIDTH + POOL_WIDTH
    tm = _row_tile(n, 512)
    return pl.pallas_call(
        _norm_inproj_kernel,
        grid=(n // tm,),
        in_specs=[pl.BlockSpec((tm, d), lambda i: (i, 0)),
                  _resident((1, d)),
                  pl.BlockSpec((1, d, in_width), lambda i: (layer, 0, 0), pipeline_mode=pl.Buffered(1))],
        out_specs=pl.BlockSpec((tm, PROJ_WIDTH), lambda i: (i, 0)),
        out_shape=jax.ShapeDtypeStruct((n, PROJ_WIDTH), jnp.float32),
        scratch_shapes=[pltpu.VMEM((d, PROJ_WIDTH), jnp.bfloat16)],
        compiler_params=_params("arbitrary"),
        name="norm_inproj",
    )(x2, g.reshape(1, d), w_in)


POOL_HIST = max(POOL_SIZES)


def _pool_outproj_kernel(x_ref, ya_ref, yb_ref, cu_ref, size_ref, pw_ref, ps_ref,
                         w_ref, o_ref, ext_s):
    f32, bf16 = jnp.float32, jnp.bfloat16
    tm = x_ref.shape[0]
    t_blk = pl.program_id(1)

    @pl.when(t_blk == 0)
    def _():
        ext_s[0:POOL_HIST, :] = jnp.zeros((POOL_HIST, POOL_WIDTH), f32)

    cu = cu_ref[...]
    ext_s[POOL_HIST:POOL_HIST + tm, :] = cu
    ext = ext_s[...]
    ext_s[0:POOL_HIST, :] = ext_s[tm:tm + POOL_HIST, :]
    size = size_ref[...]
    win = ext
    total = jnp.zeros_like(ext)
    shift = 1
    while shift < POOL_HIST:
        win = win + pltpu.roll(win, shift, 0)
        shift *= 2
        total = jnp.where(size == shift, win, total)
    total = total[POOL_HIST:, :]
    pos = t_blk * tm + lax.broadcasted_iota(jnp.int32, (tm, 1), 0)
    count = jnp.minimum(pos + 1, size).astype(f32)
    pooled = total / count - cu
    yc = _dot(pooled.astype(bf16), pw_ref[...]) * ps_ref[...]

    mixed = jnp.concatenate([ya_ref[...].astype(bf16), yb_ref[...].astype(bf16), yc.astype(bf16)], axis=1)
    o_ref[...] = x_ref[...] + _dot(mixed, w_ref[...])


def pool_outproj_residual(x2, ya, yb, proj, bsz, s_len, pool_w, pool_scale, w_out):
    n, d = x2.shape
    tm = _row_tile(s_len, 1024)
    nt = s_len // tm
    bf16 = jnp.bfloat16
    size = jnp.asarray(np.repeat(np.asarray(POOL_SIZES, np.int32), POOL_GROUP_DIM).reshape(1, POOL_WIDTH))
    pw = jax.scipy.linalg.block_diag(*[pool_w[g] for g in range(POOL_GROUPS)]).astype(bf16)
    row = lambda width, cb=0: pl.BlockSpec((tm, width), lambda b, t: (b * nt + t, cb))
    return pl.pallas_call(
        _pool_outproj_kernel,
        grid=(bsz, nt),
        in_specs=[row(d), row(GDN_WIDTH), row(SWA_WIDTH), row(POOL_WIDTH, POOL_OFF // POOL_WIDTH),
                  _resident((1, POOL_WIDTH)), _resident(pw.shape), _resident((1, POOL_WIDTH)),
                  _resident(w_out.shape)],
        out_specs=row(d),
        out_shape=jax.ShapeDtypeStruct((n, d), jnp.float32),
        scratch_shapes=[pltpu.VMEM((tm + POOL_HIST, POOL_WIDTH), jnp.float32)],
        compiler_params=_params("parallel", "arbitrary"),
        name="pool_outproj_residual",
    )(x2, ya, yb, proj, size, pw, pool_scale.reshape(1, POOL_WIDTH), w_out.astype(bf16))


MXU_DIM = 256
HIDDEN_SLICE = 512


def _hidden_slices(width):
    assert width % MXU_DIM == 0
    return [(s, min(HIDDEN_SLICE, width - s)) for s in range(0, width, HIDDEN_SLICE)]


def _swiglu_slice(h, wg, wu, wd):
    gate = _dot(h, wg)
    up = _dot(h, wu)
    return _dot((gate * jax.nn.sigmoid(gate) * up).astype(jnp.bfloat16), wd)


def _ffn_kernel(x_ref, g_ref, wg_ref, wu_ref, wd_ref, o_ref):
    x = x_ref[...]
    ms = jnp.mean(x * x, axis=-1, keepdims=True)
    h = (x * lax.rsqrt(ms + RMS_EPS) * g_ref[...]).astype(jnp.bfloat16)
    o_ref[...] = x
    for s, w in _hidden_slices(wg_ref.shape[1]):
        o_ref[...] += _swiglu_slice(h, wg_ref[:, s:s + w], wu_ref[:, s:s + w], wd_ref[s:s + w, :])


def ffn_dense(x2, g, w_gate, w_up, w_down):
    n, d = x2.shape
    d_ff = w_gate.shape[1]
    tm = _row_tile(n, 512)
    row = pl.BlockSpec((tm, d), lambda i: (i, 0))
    return pl.pallas_call(
        _ffn_kernel,
        grid=(n // tm,),
        in_specs=[row, _resident((1, d)), _resident((d, d_ff)), _resident((d, d_ff)),
                  _resident((d_ff, d))],
        out_specs=row,
        out_shape=jax.ShapeDtypeStruct((n, d), jnp.float32),
        compiler_params=_params("parallel"),
        name="ffn_dense",
    )(x2, g.reshape(1, d), w_gate.astype(jnp.bfloat16), w_up.astype(jnp.bfloat16),
      w_down.astype(jnp.bfloat16))


NO_EXPERT = -1e30
ROUTE_EXPERT, ROUTE_GATE, ROUTE_RANK = 0, TOP_K, 2 * TOP_K


def _router_kernel(x_ref, g_ref, wr_ref, br_ref, ltri_ref, h_ref, route_ref, count_ref, count_s):
    f32 = jnp.float32
    tm = x_ref.shape[0]

    @pl.when(pl.program_id(0) == 0)
    def _():
        count_s[...] = jnp.zeros_like(count_s)

    x = x_ref[...]
    ms = jnp.mean(x * x, axis=-1, keepdims=True)
    h = x * lax.rsqrt(ms + RMS_EPS) * g_ref[...]
    h_hi, h_lo = _bf16_pieces(h, 2)
    h_ref[...] = h_hi
    logits = _dot(jnp.concatenate([h_hi, h_lo, h_hi], axis=1), wr_ref[...]) + br_ref[...]
    lane = lax.broadcasted_iota(jnp.int32, (tm, LANES), 1)

    def top(vals):
        best = jnp.max(vals, axis=-1, keepdims=True)
        return best, jnp.min(jnp.where(vals == best, lane, LANES), axis=-1, keepdims=True)

    m1, e1 = top(logits)
    m2, e2 = top(jnp.where(lane == e1, 2 * NO_EXPERT, logits))
    t = jnp.exp(m2 - m1)
    gate1 = 1.0 / (1.0 + t)
    gate2 = t / (1.0 + t)
    sel1, sel2 = lane == e1, lane == e2
    chosen = (sel1 | sel2).astype(f32)
    before = _dot(ltri_ref[...], chosen.astype(jnp.bfloat16)) + count_s[...]
    rank1 = jnp.sum(jnp.where(sel1, before, 0.0), axis=-1, keepdims=True)
    rank2 = jnp.sum(jnp.where(sel2, before, 0.0), axis=-1, keepdims=True)
    count_s[...] += jnp.sum(chosen, axis=0, keepdims=True)
    count_ref[...] = count_s[...]
    record = jnp.zeros((tm, LANES), f32)
    for pos, val in ((ROUTE_EXPERT, e1.astype(f32)), (ROUTE_EXPERT + 1, e2.astype(f32)),
                     (ROUTE_GATE, gate1), (ROUTE_GATE + 1, gate2),
                     (ROUTE_RANK, rank1), (ROUTE_RANK + 1, rank2)):
        record = jnp.where(lane == pos, val, record)
    route_ref[...] = record


def norm_router(x2, g, router_w, router_b):
    n, d = x2.shape
    tm = _row_tile(n, 1024)
    w_hi, w_lo = _bf16_pieces(jnp.pad(router_w, ((0, 0), (0, LANES - N_EXPERTS))), 2)
    wr = jnp.concatenate([w_hi, w_hi, w_lo], axis=0)
    br = jnp.pad(router_b, (0, LANES - N_EXPERTS), constant_values=NO_EXPERT).reshape(1, LANES)
    ti = np.arange(tm)
    ltri = jnp.asarray(ti[None, :] < ti[:, None], jnp.bfloat16)
    return pl.pallas_call(
        _router_kernel,
        grid=(n // tm,),
        in_specs=[pl.BlockSpec((tm, d), lambda i: (i, 0)), _resident((1, d)),
                  _resident(wr.shape), _resident((1, LANES)), _resident((tm, tm))],
        out_specs=[pl.BlockSpec((tm, d), lambda i: (i, 0)),
                   pl.BlockSpec((tm, LANES), lambda i: (i, 0)),
                   pl.BlockSpec((1, LANES), lambda i: (0, 0))],
        out_shape=[jax.ShapeDtypeStruct((n, d), jnp.bfloat16),
                   jax.ShapeDtypeStruct((n, LANES), jnp.float32),
                   jax.ShapeDtypeStruct((1, LANES), jnp.float32)],
        scratch_shapes=[pltpu.VMEM((1, LANES), jnp.float32)],
        compiler_params=_params("arbitrary"),
        name="norm_router",
    )(x2, g.reshape(1, d), wr, br, ltri)


def _moe_kernel(first, be_ref, nb_ref, xs_ref, wg_ref, wu_ref, wd_ref, *refs):
    o_ref, acc_s = refs[-2:]
    j = first + pl.program_id(0)
    c = pl.program_id(1)
    last = pl.num_programs(1) - 1

    @pl.when(j < nb_ref[0])
    def _():
        h = xs_ref[...]

        @pl.when(c == 0)
        def _():
            acc_s[...] = jnp.zeros_like(acc_s)

        for s, w in _hidden_slices(wg_ref.shape[2]):
            acc_s[...] += _swiglu_slice(h, wg_ref[0, :, s:s + w], wu_ref[0, :, s:s + w],
                                        wd_ref[0, s:s + w, :])

        @pl.when(c == last)
        def _():
            o_ref[...] = acc_s[...].astype(o_ref.dtype)

    @pl.when(jnp.logical_and(j >= nb_ref[0], c == last))
    def _():
        o_ref[...] = jnp.zeros_like(o_ref)


def moe_experts(xs, first, n_blocks_total, y_prev, block_expert, n_used, w_gate, w_up, w_down, tm, tc):
    d = xs.shape[1]
    n_here = xs.shape[0] // tm
    d_e = w_gate.shape[2]
    n_c = d_e // tc

    def live(j, nb):
        return jnp.clip(jnp.minimum(first + j, nb[0] - 1), first, first + n_here - 1)

    def chunk(j, c, nb):
        return jnp.where(first + j < nb[0], c, n_c - 1)

    in_specs = [
        pl.BlockSpec((tm, d), lambda j, c, be, nb: (live(j, nb) - first, 0)),
        pl.BlockSpec((1, d, tc), lambda j, c, be, nb: (be[live(j, nb)], 0, chunk(j, c, nb))),
        pl.BlockSpec((1, d, tc), lambda j, c, be, nb: (be[live(j, nb)], 0, chunk(j, c, nb))),
        pl.BlockSpec((1, tc, d), lambda j, c, be, nb: (be[live(j, nb)], chunk(j, c, nb), 0)),
    ]
    operands = [block_expert, n_used, xs, w_gate, w_up, w_down]
    aliases = {}
    if y_prev is not None:
        in_specs.append(pl.BlockSpec(memory_space=pl.ANY))
        aliases = {len(operands): 0}
        operands.append(y_prev)
    return pl.pallas_call(
        functools.partial(_moe_kernel, first),
        grid_spec=pltpu.PrefetchScalarGridSpec(
            num_scalar_prefetch=2,
            grid=(n_here, n_c),
            in_specs=in_specs,
            out_specs=pl.BlockSpec((tm, d), lambda j, c, be, nb: (first + j, 0)),
            scratch_shapes=[pltpu.VMEM((tm, d), jnp.float32)],
        ),
        out_shape=jax.ShapeDtypeStruct((n_blocks_total * tm, d), jnp.bfloat16),
        input_output_aliases=aliases,
        compiler_params=_params("arbitrary", "arbitrary"),
        name="moe_experts",
    )(*operands)


MOE_STAGES = 4
CAST_BLOCK_BYTES = 4 * 1024 * 1024


def _cast_kernel(x_ref, o_ref):
    o_ref[...] = x_ref[...].astype(o_ref.dtype)


def cast_bf16(w):
    e, rows, cols = w.shape
    want = max(16, CAST_BLOCK_BYTES // (4 * cols))
    blk = _row_tile(rows, 1 << (want.bit_length() - 1))
    spec = pl.BlockSpec((1, blk, cols), lambda i, j: (i, j, 0))
    return pl.pallas_call(
        _cast_kernel,
        grid=(e, rows // blk),
        in_specs=[spec],
        out_specs=spec,
        out_shape=jax.ShapeDtypeStruct(w.shape, jnp.bfloat16),
        compiler_params=_params("parallel", "parallel"),
        name="cast_bf16",
    )(w)


def moe_block(x2, g, router_w, router_b, w_gate, w_up, w_down):
    n, d = x2.shape
    tm = _row_tile(n, 1024)
    h, route, count = norm_router(x2, g, router_w, router_b)
    expert = route[:, ROUTE_EXPERT:ROUTE_EXPERT + TOP_K].astype(jnp.int32)
    gates = route[:, ROUTE_GATE:ROUTE_GATE + TOP_K]
    rank = route[:, ROUTE_RANK:ROUTE_RANK + TOP_K].astype(jnp.int32)
    counts = count[0, :N_EXPERTS].astype(jnp.int32)
    padded = (counts + tm - 1) // tm * tm
    padded_end = jnp.cumsum(padded)
    padded_start = padded_end - padded
    slot = jnp.take(padded_start, expert) + rank
    n_blocks = -(-(n * TOP_K) // tm) + N_EXPERTS
    n_slots = n_blocks * tm
    token = jnp.broadcast_to(jnp.arange(n, dtype=jnp.int32)[:, None], (n, TOP_K))
    slot_token = jnp.zeros((n_slots,), jnp.int32).at[slot.reshape(-1)].set(
        token.reshape(-1), unique_indices=True, mode="promise_in_bounds")
    block_start = jnp.arange(n_blocks) * tm
    block_expert = jnp.minimum(
        jnp.sum(block_start[:, None] >= padded_end[None, :], axis=1), N_EXPERTS - 1).astype(jnp.int32)
    n_used = (padded_end[-1] // tm).astype(jnp.int32).reshape(1)
    rows_at = lambda table, idx: table.at[idx].get(mode="promise_in_bounds")
    d_e = w_gate.shape[2]
    tc = d_e // 2 if d_e % (2 * MXU_DIM) == 0 else d_e
    weights = [cast_bf16(w) for w in (w_gate, w_up, w_down)]
    n_stages = MOE_STAGES if n_blocks % MOE_STAGES == 0 else 1
    per_stage = n_blocks // n_stages
    y_slots = None
    for stage in range(n_stages):
        first = stage * per_stage
        xs = rows_at(h, slot_token[first * tm:(first + per_stage) * tm])
        y_slots = moe_experts(xs, first, n_blocks, y_slots, block_expert, n_used, *weights, tm, tc)
    out = x2
    for j in range(TOP_K):
        out = out + gates[:, j:j + 1] * rows_at(y_slots, slot[:, j]).astype(jnp.float32)
    return out


GDN_BLOCK = 512
GDN_GROUP = 4
CONV_PAD = 8
GDN_PAIRS = GDN_HEADS // 2
SELECT_TERMS = 3
PAIR_ROWS = 2 * GDN_CHUNK


def _gdn_kernel(aqkv_ref, az_ref, small_ref, convw_ref, aneg_ref, bias_ref, gnorm_ref,
                eeven_ref, eodd_ref, ebeta_ref, bd_ref, ltri_ref, o_ref,
                ext_s, state_s, q_s, k_s, kb_s, qd_s, rhs_s, attn_s, w_s, kdt_s,
                gc_s, kd_s, u_s, oh_s):
    f32, bf16 = jnp.float32, jnp.bfloat16
    c = GDN_CHUNK
    tb = aqkv_ref.shape[0]
    gw = GDN_WIDTH
    nc = tb // c

    @pl.when(pl.program_id(1) == 0)
    def _():
        ext_s[0:CONV_PAD, :] = jnp.zeros((CONV_PAD, 3 * gw), f32)
        state_s[...] = jnp.zeros_like(state_s)

    ext_s[CONV_PAD:CONV_PAD + tb, :] = aqkv_ref[...]
    conv = convw_ref[0:1, :] * ext_s[pl.ds(CONV_PAD - CONV_K + 1, tb), :]
    for j in range(1, CONV_K):
        conv += convw_ref[j:j + 1, :] * ext_s[pl.ds(CONV_PAD - CONV_K + 1 + j, tb), :]
    ext_s[0:CONV_PAD, :] = ext_s[tb:tb + CONV_PAD, :]
    qkv = conv * jax.nn.sigmoid(conv)
    q, k, v = qkv[:, :gw], qkv[:, gw:2 * gw], qkv[:, 2 * gw:]

    bd = bd_ref[...]
    q = q * lax.rsqrt(_group_sum_sq(q, bd) + RMS_EPS) * (HEAD_DIM ** -0.5)
    k = k * lax.rsqrt(_group_sum_sq(k, bd) + RMS_EPS)

    sm = small_ref[...]
    pre = sm + bias_ref[...]
    softplus = jnp.maximum(pre, 0.0) + jnp.log(1.0 + jnp.exp(-jnp.abs(pre)))
    g = _prefix_dot(ltri_ref[...], aneg_ref[...] * softplus, SELECT_TERMS)
    g_even = _select_dot(g, eeven_ref[...])
    g_odd = _select_dot(g, eodd_ref[...])
    bb = _select_dot(jax.nn.sigmoid(sm), ebeta_ref[...])
    lane = lax.broadcasted_iota(jnp.int32, (1, gw), 1)
    even_lane = (lane // HEAD_DIM) % 2 == 0
    gb = jnp.where(even_lane, g_even, g_odd)
    gb3 = gb.reshape(nc, c, gw)
    k_dec = (k.reshape(nc, c, gw) * jnp.exp(gb3[:, c - 1:c, :] - gb3)).reshape(tb, gw)
    eg = jnp.exp(gb)
    kb = k * bb

    def stack(x):
        x3 = x.reshape(nc, c, gw)
        zero = jnp.zeros_like(x3)
        return jnp.concatenate([jnp.where(even_lane, x3, zero), jnp.where(even_lane, zero, x3)], axis=1)

    q_s[...] = stack(q.astype(bf16))
    k_s[...] = stack(k.astype(bf16))
    kb_s[...] = stack(kb.astype(bf16))
    qd_s[...] = stack((q * eg).astype(bf16))
    kd_s[...] = stack(k_dec)
    vb_st = stack((v * bb).astype(bf16))
    kbe_st = stack((kb * eg).astype(bf16))
    for p in range(GDN_PAIRS):
        ps = slice(p * LANES, (p + 1) * LANES)
        rhs_s[:, :, 2 * p * LANES:(2 * p + 1) * LANES] = vb_st[:, :, ps]
        rhs_s[:, :, (2 * p + 1) * LANES:(2 * p + 2) * LANES] = kbe_st[:, :, ps]
    gc_s[...] = jnp.concatenate([g_even.reshape(nc, c, gw), g_odd.reshape(nc, c, gw)], axis=1)

    row = lax.broadcasted_iota(jnp.int32, (PAIR_ROWS, PAIR_ROWS), 0)
    col = lax.broadcasted_iota(jnp.int32, (PAIR_ROWS, PAIR_ROWS), 1)
    causal = (row // c == col // c) & (row >= col)
    strict = causal & (row > col)
    eye = (row == col).astype(f32)

    def phase_a(gi):
        units = [(gi * GDN_GROUP + ci, p) for ci in range(GDN_GROUP) for p in range(GDN_PAIRS)]
        at = lambda ref, n, p: ref[n, :, p * LANES:(p + 1) * LANES]
        k_u = [at(k_s, n, p) for n, p in units]
        kk = [_dot_nt(at(kb_s, n, p), kx) for (n, p), kx in zip(units, k_u)]
        qk = [_dot_nt(at(q_s, n, p), kx) for (n, p), kx in zip(units, k_u)]
        gc = [at(gc_s, n, p) for n, p in units]
        dm = [jnp.where(causal, jnp.exp(jnp.minimum(gx - gx.T, 0.0)), 0.0) for gx in gc]
        a = [jnp.where(strict, x * d, 0.0) for x, d in zip(kk, dm)]
        for (n, p), x, d in zip(units, qk, dm):
            attn_s[n, :, p * LANES:(p + 1) * LANES] = (x * d).astype(bf16)
        for (n, p) in units:
            kdt_s[n, :, p * LANES:(p + 1) * LANES] = at(kd_s, n, p).T.astype(bf16)
        ab = [x.astype(bf16) for x in a]
        sq = [_dot(x, x) for x in ab]
        inv = [eye - x for x in a]
        for _ in range(4):
            sqb = [x.astype(bf16) for x in sq]
            both = [_dot(x, jnp.concatenate([x, t.astype(bf16)], axis=1)) for x, t in zip(sqb, inv)]
            sq = [x[:, :PAIR_ROWS] for x in both]
            inv = [t + x[:, PAIR_ROWS:] for t, x in zip(inv, both)]
        inv = [t + _dot(x.astype(bf16), t.astype(bf16)) for t, x in zip(inv, sq)]
        for (n, p), t in zip(units, inv):
            uw = _dot(t.astype(bf16), rhs_s[n, :, 2 * p * LANES:(2 * p + 2) * LANES])
            u_s[n, :, p * LANES:(p + 1) * LANES] = uw[:, :LANES]
            w_s[n, :, p * LANES:(p + 1) * LANES] = uw[:, LANES:].astype(bf16)

    pairs = range(GDN_PAIRS)

    def phase_b(n, s):
        at = lambda ref, p: ref[n, :, p * LANES:(p + 1) * LANES]
        sb = [x.astype(bf16) for x in s]
        both = [_dot(jnp.concatenate([at(w_s, p), at(qd_s, p)], axis=0), sb[p]) for p in pairs]
        vn = [(at(u_s, p) - both[p][:PAIR_ROWS]).astype(bf16) for p in pairs]
        o_st = [both[p][PAIR_ROWS:] + _dot(at(attn_s, p), vn[p]) for p in pairs]
        upd = [_dot(at(kdt_s, p), vn[p]) for p in pairs]
        new = []
        for p in pairs:
            ps = slice(p * LANES, (p + 1) * LANES)
            g_last = jnp.where(even_lane[:, :LANES], gc_s[n, c - 1:c, ps], gc_s[n, 2 * c - 1:2 * c, ps])
            new.append(s[p] * jnp.exp(g_last) + upd[p])
            oh_s[n * c:(n + 1) * c, ps] = o_st[p][:c] + o_st[p][c:]
        return new

    n_groups = nc // GDN_GROUP
    state = [state_s[p] for p in pairs]
    phase_a(0)
    for gi in range(1, n_groups + 1):
        if gi < n_groups:
            phase_a(gi)
        for n in range((gi - 1) * GDN_GROUP, gi * GDN_GROUP):
            state = phase_b(n, state)
    for p in pairs:
        state_s[p] = state[p]

    o = oh_s[...]
    ms = _group_sum_sq(o, bd) * (1.0 / HEAD_DIM)
    z = az_ref[...]
    o_ref[...] = o * lax.rsqrt(ms + RMS_EPS) * gnorm_ref[...] * (z * jax.nn.sigmoid(z))


def gdn_mixer(proj, bsz, s_len, conv_w, a_log, dt_bias, gdn_norm):
    n = bsz * s_len
    gw, nh, c = GDN_WIDTH, GDN_HEADS, GDN_CHUNK
    tb = _row_tile(s_len, GDN_BLOCK)
    nt = s_len // tb
    nc = tb // c
    assert nc % GDN_GROUP == 0 and nh % 2 == 0
    f32, bf16 = jnp.float32, jnp.bfloat16
    aneg = jnp.zeros((1, LANES), f32).at[0, :nh].set(-jnp.exp(a_log))
    bias = jnp.zeros((1, LANES), f32).at[0, :nh].set(dt_bias)
    gnorm = jnp.tile(gdn_norm, nh).reshape(1, gw)
    lane_head = np.arange(gw) // HEAD_DIM
    src = np.arange(LANES)[:, None]
    stacked_sel = lambda m: jnp.asarray(np.tile(m, (SELECT_TERMS, 1)), bf16)
    eeven = stacked_sel(src == (lane_head // 2 * 2)[None, :])
    eodd = stacked_sel(src == (lane_head // 2 * 2 + 1)[None, :])
    ebeta = stacked_sel(src == nh + lane_head[None, :])
    bd = jnp.asarray(lane_head[:, None] == lane_head[None, :], bf16)
    ti = np.arange(tb)
    ltri = jnp.asarray((ti[:, None] // c == ti[None, :] // c) & (ti[None, :] <= ti[:, None]), bf16)
    rows = lambda width, cb: pl.BlockSpec((tb, width), lambda b, t: (b * nt + t, cb))
    stacked = lambda dt, width=gw: pltpu.VMEM((nc, PAIR_ROWS, width), dt)
    return pl.pallas_call(
        _gdn_kernel,
        grid=(bsz, nt),
        in_specs=[rows(3 * gw, 0), rows(gw, 3), rows(LANES, SMALL_OFF // LANES),
                  _resident((CONV_K, 3 * gw)), _resident((1, LANES)), _resident((1, LANES)),
                  _resident((1, gw)), _resident(eeven.shape), _resident(eodd.shape),
                  _resident(ebeta.shape), _resident((gw, gw)), _resident((tb, tb))],
        out_specs=pl.BlockSpec((tb, gw), lambda b, t: (b * nt + t, 0)),
        out_shape=jax.ShapeDtypeStruct((n, gw), f32),
        scratch_shapes=[
            pltpu.VMEM((tb + CONV_PAD, 3 * gw), f32),
            pltpu.VMEM((GDN_PAIRS, LANES, LANES), f32),
            stacked(bf16), stacked(bf16), stacked(bf16), stacked(bf16),
            stacked(bf16, 2 * gw),
            stacked(bf16), stacked(bf16), stacked(bf16),
            stacked(f32), stacked(f32), stacked(f32),
            pltpu.VMEM((tb, gw), f32)],
        compiler_params=_params("parallel", "arbitrary"),
        name="gdn_mixer",
    )(proj, proj, proj, conv_w, aneg, bias, gnorm, eeven, eodd, ebeta, bd, ltri)


SWA_SPAN = 128
SWA_PREP_ROWS = 512
SWA_GROUP = 8
MASKED_SCORE = -1e30


def _swa_kernel(q_ref, k_ref, v_ref, cos_ref, sin_ref, qn_ref, kn_ref, bd_ref, o_ref,
                q_s, k_s, v_s, op_s, lse_s):
    f32, bf16 = jnp.float32, jnp.bfloat16
    s_len = q_ref.shape[0]
    blk = SWA_SPAN
    lane = lax.broadcasted_iota(jnp.int32, (1, LANES), 1)
    first_half = (lane % HEAD_DIM) < (HEAD_DIM // 2)
    head0 = lane < HEAD_DIM
    bd = bd_ref[...]
    prep_rows = min(SWA_PREP_ROWS, s_len)

    def normed_rotary(x, gain, cos, sin, scale):
        ms = _group_sum_sq(x, bd) * (1.0 / HEAD_DIM)
        y = x * lax.rsqrt(ms + RMS_EPS) * gain
        partner = jnp.where(first_half, pltpu.roll(y, LANES - HEAD_DIM // 2, 1),
                            pltpu.roll(y, HEAD_DIM // 2, 1))
        return (y * cos + partner * sin) * scale

    def prologue(i, carry):
        rows = pl.ds(pl.multiple_of(i * prep_rows, prep_rows), prep_rows)
        cos, sin = cos_ref[rows, :], sin_ref[rows, :]
        q_s[rows, :] = normed_rotary(q_ref[rows, :], qn_ref[...], cos, sin, HEAD_DIM ** -0.5)
        k_s[rows, :] = normed_rotary(k_ref[rows, :], kn_ref[...], cos, sin, 1.0)
        v_s[rows, :] = v_ref[rows, :]
        return carry

    lax.fori_loop(0, s_len // prep_rows, prologue, 0)

    qi = lax.broadcasted_iota(jnp.int32, (blk, 2 * blk), 0)
    kj = lax.broadcasted_iota(jnp.int32, (blk, 2 * blk), 1)
    dist = blk + qi - kj
    band = (dist >= 0) & (dist <= SWA_SPAN)
    ones_kv = jnp.ones((2 * blk, LANES), bf16)

    for pidx, (window, dil) in enumerate(DILATED_PAIRS):
        n_blk = s_len // (dil * blk)

        def rows_of(r, i):
            start = r + i * (blk * dil)
            if dil == 1:
                return pl.ds(start, blk)
            return pl.ds(start, blk, stride=dil)

        def block_group(gi, carry):
            units = [gi * SWA_GROUP + j for j in range(SWA_GROUP)]
            ri = [(u // n_blk, u % n_blk) for u in units]
            cur = [rows_of(r, i) for r, i in ri]
            prev = [rows_of(r, jnp.maximum(i - 1, 0)) for r, i in ri]
            qb = [q_s[c, :] for c in cur]
            kw = [jnp.concatenate([k_s[pv, :], k_s[c, :]], axis=0).astype(bf16)
                  for pv, c in zip(prev, cur)]
            vw = [jnp.concatenate([jnp.concatenate([v_s[pv, :], v_s[c, :]], axis=0).astype(bf16),
                                   ones_kv], axis=1) for pv, c in zip(prev, cur)]
            valid = [band & (kj >= jnp.where(i > 0, 0, blk)) for _, i in ri]
            chains = [(j, h) for j in range(SWA_GROUP) for h in range(2)]
            qh = [jnp.where(head0 if h == 0 else ~head0, qb[j], 0.0).astype(bf16) for j, h in chains]
            s = [jnp.where(valid[j], _dot_nt(x, kw[j]), MASKED_SCORE) for (j, h), x in zip(chains, qh)]
            m = [jnp.max(x, axis=-1, keepdims=True) for x in s]
            p = [jnp.exp((x - mx).astype(bf16)) for x, mx in zip(s, m)]
            pv = [_dot(x, vw[j]) for (j, h), x in zip(chains, p)]
            o = [x[:, :LANES] / x[:, LANES:] for x in pv]
            lse = [mx + jnp.log(x[:, LANES:]) for mx, x in zip(m, pv)]
            for j in range(SWA_GROUP):
                op_s[pidx, cur[j], :] = jnp.where(head0, o[2 * j], o[2 * j + 1])
                lse_s[pidx, cur[j], :] = jnp.where(head0, lse[2 * j], lse[2 * j + 1])
            return carry

        lax.fori_loop(0, s_len // (blk * SWA_GROUP), block_group, 0)

    def epilogue(i, carry):
        rows = pl.ds(pl.multiple_of(i * prep_rows, prep_rows), prep_rows)
        l0, l1, l2 = lse_s[0, rows, :], lse_s[1, rows, :], lse_s[2, rows, :]
        m = jnp.maximum(jnp.maximum(l0, l1), l2)
        w0, w1, w2 = jnp.exp(l0 - m), jnp.exp(l1 - m), jnp.exp(l2 - m)
        mix = w0 * op_s[0, rows, :] + w1 * op_s[1, rows, :] + w2 * op_s[2, rows, :]
        o_ref[rows, :] = mix / (w0 + w1 + w2)
        return carry

    lax.fori_loop(0, s_len // prep_rows, epilogue, 0)


def swa_mixer(proj, bsz, s_len, q_norm, k_norm):
    assert all(w // d == SWA_SPAN for w, d in DILATED_PAIRS) and len(DILATED_PAIRS) == 3
    assert s_len % (SWA_SPAN * max(d for _, d in DILATED_PAIRS)) == 0
    n = bsz * s_len
    f32 = jnp.float32
    n_pairs = SWA_WIDTH // LANES
    half = HEAD_DIM // 2
    inv_freq = ROPE_THETA ** (-jnp.arange(0, HEAD_DIM, 2, dtype=f32) / HEAD_DIM)
    ang = jnp.arange(s_len, dtype=f32)[:, None] * inv_freq[None, :]
    cos = jnp.tile(jnp.cos(ang), (1, LANES // half))
    sin = jnp.tile(jnp.concatenate([-jnp.sin(ang), jnp.sin(ang)], axis=1), (1, LANES // HEAD_DIM))
    gain = lambda g: jnp.tile(g, LANES // HEAD_DIM).reshape(1, LANES)
    lane_head = np.arange(LANES) // HEAD_DIM
    bd = jnp.asarray(lane_head[:, None] == lane_head[None, :], jnp.bfloat16)
    q_col0 = 4 * GDN_WIDTH // LANES
    col = lambda which: pl.BlockSpec(
        (s_len, LANES), lambda b, p: (b, q_col0 + which * n_pairs + p))
    seq = pltpu.VMEM((s_len, LANES), f32)
    per_pattern = pltpu.VMEM((len(DILATED_PAIRS), s_len, LANES), f32)
    return pl.pallas_call(
        _swa_kernel,
        grid=(bsz, n_pairs),
        in_specs=[col(0), col(1), col(2), _resident((s_len, LANES)), _resident((s_len, LANES)),
                  _resident((1, LANES)), _resident((1, LANES)), _resident((LANES, LANES))],
        out_specs=pl.BlockSpec((s_len, LANES), lambda b, p: (b, p)),
        out_shape=jax.ShapeDtypeStruct((n, SWA_WIDTH), f32),
        scratch_shapes=[seq, seq, seq, per_pattern, per_pattern],
        compiler_params=_params("parallel", "parallel"),
        name="swa_mixer",
    )(proj, proj, proj, cos, sin, gain(q_norm), gain(k_norm), bd)


def kernel(x, norm_mix, w_in, conv_w, a_log, dt_bias, gdn_norm, q_norm, k_norm, pool_w,
           pool_scale, w_out, norm_ffn, ffn_gate, ffn_up, ffn_down, router_w, router_b,
           exp_gate, exp_up, exp_down):
    bsz, s_len, d = x.shape
    n = bsz * s_len
    depth = w_in.shape[0]
    x2 = x.reshape(n, d)
    for layer in range(depth):
        proj = norm_inproj(x2, norm_mix[layer], w_in, layer)
        ya = gdn_mixer(proj, bsz, s_len, conv_w[layer], a_log[layer], dt_bias[layer],
                       gdn_norm[layer])
        yb = swa_mixer(proj, bsz, s_len, q_norm[layer], k_norm[layer])
        x2 = pool_outproj_residual(x2, ya, yb, proj, bsz, s_len, pool_w[layer],
                                   pool_scale[layer], w_out[layer])
        i = layer // 2
        if layer % 2 == 0:
            x2 = ffn_dense(x2, norm_ffn[layer], ffn_gate[i], ffn_up[i], ffn_down[i])
        else:
            x2 = moe_block(x2, norm_ffn[layer], router_w[i], router_b[i], exp_gate[i],
                           exp_up[i], exp_down[i])
    return x2.reshape(bsz, s_len, d)
```

```python
import functools

import jax
import jax.numpy as jnp
import numpy as np
from jax import lax
from jax.experimental import pallas as pl
from jax.experimental.pallas import tpu as pltpu

D_MODEL = 1024
HEAD_DIM = 64
MIX_WIDTH = D_MODEL
POOL_WIDTH = MIX_WIDTH // 4
POOL_GROUPS = 4
POOL_GROUP_DIM = POOL_WIDTH // POOL_GROUPS
POOL_SIZES = (2, 4, 8, 16)
GDN_HEADS = (MIX_WIDTH - POOL_WIDTH) // (2 * HEAD_DIM)
GDN_WIDTH = GDN_HEADS * HEAD_DIM
SWA_HEADS = (MIX_WIDTH - POOL_WIDTH - GDN_WIDTH) // HEAD_DIM
SWA_WIDTH = SWA_HEADS * HEAD_DIM
CONV_K = 4
GDN_CHUNK = 64
DILATED_PAIRS = ((128, 1), (512, 4), (2048, 16))
ROPE_THETA = 10000.0
RMS_EPS = 1e-6
N_EXPERTS = 8
TOP_K = 2

VMEM_LIMIT_BYTES = 56 * 1024 * 1024
LANES = 128
MATMUL_ROWS = 512
ROUTED_ROWS = 1024

SMALL_OFF = 7 * GDN_WIDTH
POOL_OFF = SMALL_OFF + LANES
PROJ_WIDTH = POOL_OFF + POOL_WIDTH


def _row_tile(n_rows, want):
    t = min(want, n_rows)
    while n_rows % t:
        t //= 2
    return t


def _params(*sem):
    return pltpu.CompilerParams(dimension_semantics=sem, vmem_limit_bytes=VMEM_LIMIT_BYTES)


def _resident(shape):
    zeros = (0,) * len(shape)
    return pl.BlockSpec(shape, lambda *_: zeros, pipeline_mode=pl.Buffered(1))


def _dot(a, b):
    return jnp.dot(a, b, preferred_element_type=jnp.float32)


def _dot_nt(a, b):
    return lax.dot_general(a, b, (((1,), (1,)), ((), ())), preferred_element_type=jnp.float32)


def _bf16_pieces(x, terms):
    pieces, rest = [], x
    for t in range(terms):
        pieces.append(rest.astype(jnp.bfloat16))
        if t + 1 < terms:
            rest = rest - pieces[-1].astype(jnp.float32)
    return pieces


def _select_dot(x, sel_stacked):
    terms = sel_stacked.shape[0] // x.shape[1]
    return _dot(jnp.concatenate(_bf16_pieces(x, terms), axis=1), sel_stacked)


def _prefix_dot(sel, x, terms):
    width = x.shape[1]
    wide = _dot(sel, jnp.concatenate(_bf16_pieces(x, terms), axis=1))
    return sum(wide[:, t * width:(t + 1) * width] for t in range(terms))


def _group_sum_sq(x, group_ones):
    return _dot((x * x).astype(jnp.bfloat16), group_ones)


def _norm_inproj_kernel(x_ref, g_ref, w_ref, o_ref, wb_s):
    @pl.when(pl.program_id(0) == 0)
    def _():
        bf16 = jnp.bfloat16
        a_end = 4 * GDN_WIDTH
        b_start = a_end + 2 * GDN_HEADS
        b_end = b_start + 3 * SWA_WIDTH
        wb_s[:, :a_end] = w_ref[0, :, :a_end].astype(bf16)
        wb_s[:, a_end:SMALL_OFF] = w_ref[0, :, b_start:b_end].astype(bf16)
        lane = lax.broadcasted_iota(jnp.int32, (1, LANES), 1)
        wb_s[:, SMALL_OFF:POOL_OFF] = jnp.where(
            lane < 2 * GDN_HEADS, w_ref[0, :, a_end:a_end + LANES], 0.0).astype(bf16)
        wb_s[:, POOL_OFF:] = w_ref[0, :, b_end:].astype(bf16)

    x = x_ref[...]
    ms = jnp.mean(x * x, axis=-1, keepdims=True)
    h = (x * lax.rsqrt(ms + RMS_EPS) * g_ref[...]).astype(jnp.bfloat16)
    o_ref[...] = _dot(h, wb_s[...])


def norm_inproj(x2, g, w_in, layer):
    n, d = x2.shape
    in_width = w_in.shape[2]
    assert in_width == 4 * GDN_WIDTH + 2 * GDN_HEADS + 3 * SWA_WIDTH + POOL_WIDTH
    tm = _row_tile(n, MATMUL_ROWS)
    return pl.pallas_call(
        _norm_inproj_kernel,
        grid=(n // tm,),
        in_specs=[pl.BlockSpec((tm, d), lambda i: (i, 0)),
                  _resident((1, d)),
                  pl.BlockSpec((1, d, in_width), lambda i: (layer, 0, 0), pipeline_mode=pl.Buffered(1))],
        out_specs=pl.BlockSpec((tm, PROJ_WIDTH), lambda i: (i, 0)),
        out_shape=jax.ShapeDtypeStruct((n, PROJ_WIDTH), jnp.float32),
        scratch_shapes=[pltpu.VMEM((d, PROJ_WIDTH), jnp.bfloat16)],
        compiler_params=_params("arbitrary"),
        name="norm_inproj",
    )(x2, g.reshape(1, d), w_in)


POOL_HIST = max(POOL_SIZES)


def _pool_outproj_kernel(x_ref, ya_ref, yb_ref, cu_ref, size_ref, pw_ref, ps_ref,
                         w_ref, o_ref, ext_s):
    f32, bf16 = jnp.float32, jnp.bfloat16
    tm = x_ref.shape[0]
    t_blk = pl.program_id(1)

    @pl.when(t_blk == 0)
    def _():
        ext_s[0:POOL_HIST, :] = jnp.zeros((POOL_HIST, POOL_WIDTH), f32)

    cu = cu_ref[...]
    ext_s[POOL_HIST:POOL_HIST + tm, :] = cu
    ext = ext_s[...]
    ext_s[0:POOL_HIST, :] = ext_s[tm:tm + POOL_HIST, :]
    size = size_ref[...]
    win = ext
    total = jnp.zeros_like(ext)
    shift = 1
    while shift < POOL_HIST:
        win = win + pltpu.roll(win, shift, 0)
        shift *= 2
        total = jnp.where(size == shift, win, total)
    total = total[POOL_HIST:, :]
    pos = t_blk * tm + lax.broadcasted_iota(jnp.int32, (tm, 1), 0)
    count = jnp.minimum(pos + 1, size).astype(f32)
    pooled = total / count - cu
    yc = _dot(pooled.astype(bf16), pw_ref[...]) * ps_ref[...]

    mixed = jnp.concatenate([ya_ref[...].astype(bf16), yb_ref[...].astype(bf16), yc.astype(bf16)], axis=1)
    o_ref[...] = x_ref[...] + _dot(mixed, w_ref[...])


def pool_outproj_residual(x2, ya, yb, proj, bsz, s_len, pool_w, pool_scale, w_out):
    n, d = x2.shape
    tm = _row_tile(s_len, ROUTED_ROWS)
    nt = s_len // tm
    bf16 = jnp.bfloat16
    size = jnp.asarray(np.repeat(np.asarray(POOL_SIZES, np.int32), POOL_GROUP_DIM).reshape(1, POOL_WIDTH))
    pw = jax.scipy.linalg.block_diag(*[pool_w[g] for g in range(POOL_GROUPS)]).astype(bf16)
    row = lambda width, cb=0: pl.BlockSpec((tm, width), lambda b, t: (b * nt + t, cb))
    return pl.pallas_call(
        _pool_outproj_kernel,
        grid=(bsz, nt),
        in_specs=[row(d), row(GDN_WIDTH), row(SWA_WIDTH), row(POOL_WIDTH, POOL_OFF // POOL_WIDTH),
                  _resident((1, POOL_WIDTH)), _resident(pw.shape), _resident((1, POOL_WIDTH)),
                  _resident(w_out.shape)],
        out_specs=row(d),
        out_shape=jax.ShapeDtypeStruct((n, d), jnp.float32),
        scratch_shapes=[pltpu.VMEM((tm + POOL_HIST, POOL_WIDTH), jnp.float32)],
        compiler_params=_params("parallel", "arbitrary"),
        name="pool_outproj_residual",
    )(x2, ya, yb, proj, size, pw, pool_scale.reshape(1, POOL_WIDTH), w_out.astype(bf16))


MXU_DIM = 256
HIDDEN_SLICE = 1024


def _hidden_slices(width):
    assert width % MXU_DIM == 0
    return [(s, min(HIDDEN_SLICE, width - s)) for s in range(0, width, HIDDEN_SLICE)]


def _swiglu_slice(h, wg, wu, wd):
    gate = _dot(h, wg)
    up = _dot(h, wu)
    return _dot((gate * jax.nn.sigmoid(gate) * up).astype(jnp.bfloat16), wd)


def _ffn_kernel(x_ref, g_ref, wg_ref, wu_ref, wd_ref, o_ref):
    x = x_ref[...]
    ms = jnp.mean(x * x, axis=-1, keepdims=True)
    h = (x * lax.rsqrt(ms + RMS_EPS) * g_ref[...]).astype(jnp.bfloat16)
    o_ref[...] = x
    for s, w in _hidden_slices(wg_ref.shape[1]):
        o_ref[...] += _swiglu_slice(h, wg_ref[:, s:s + w], wu_ref[:, s:s + w], wd_ref[s:s + w, :])


def ffn_dense(x2, g, w_gate, w_up, w_down):
    n, d = x2.shape
    d_ff = w_gate.shape[1]
    tm = _row_tile(n, MATMUL_ROWS)
    row = pl.BlockSpec((tm, d), lambda i: (i, 0))
    return pl.pallas_call(
        _ffn_kernel,
        grid=(n // tm,),
        in_specs=[row, _resident((1, d)), _resident((d, d_ff)), _resident((d, d_ff)),
                  _resident((d_ff, d))],
        out_specs=row,
        out_shape=jax.ShapeDtypeStruct((n, d), jnp.float32),
        compiler_params=_params("parallel"),
        name="ffn_dense",
    )(x2, g.reshape(1, d), w_gate.astype(jnp.bfloat16), w_up.astype(jnp.bfloat16),
      w_down.astype(jnp.bfloat16))


NO_EXPERT = -1e30
ROUTE_EXPERT, ROUTE_GATE, ROUTE_RANK = 0, TOP_K, 2 * TOP_K


def _router_kernel(x_ref, g_ref, wr_ref, br_ref, ltri_ref, h_ref, route_ref, count_ref, count_s):
    f32 = jnp.float32
    tm = x_ref.shape[0]

    @pl.when(pl.program_id(0) == 0)
    def _():
        count_s[...] = jnp.zeros_like(count_s)

    x = x_ref[...]
    ms = jnp.mean(x * x, axis=-1, keepdims=True)
    h = x * lax.rsqrt(ms + RMS_EPS) * g_ref[...]
    h_hi, h_lo = _bf16_pieces(h, 2)
    h_ref[...] = h_hi
    logits = _dot(jnp.concatenate([h_hi, h_lo, h_hi], axis=1), wr_ref[...]) + br_ref[...]
    lane = lax.broadcasted_iota(jnp.int32, (tm, LANES), 1)

    def top(vals):
        best = jnp.max(vals, axis=-1, keepdims=True)
        return best, jnp.min(jnp.where(vals == best, lane, LANES), axis=-1, keepdims=True)

    m1, e1 = top(logits)
    m2, e2 = top(jnp.where(lane == e1, 2 * NO_EXPERT, logits))
    t = jnp.exp(m2 - m1)
    gate1 = 1.0 / (1.0 + t)
    gate2 = t / (1.0 + t)
    sel1, sel2 = lane == e1, lane == e2
    chosen = (sel1 | sel2).astype(f32)
    before = _dot(ltri_ref[...], chosen.astype(jnp.bfloat16)) + count_s[...]
    rank1 = jnp.sum(jnp.where(sel1, before, 0.0), axis=-1, keepdims=True)
    rank2 = jnp.sum(jnp.where(sel2, before, 0.0), axis=-1, keepdims=True)
    count_s[...] += jnp.sum(chosen, axis=0, keepdims=True)
    count_ref[...] = count_s[...]
    record = jnp.zeros((tm, LANES), f32)
    for pos, val in ((ROUTE_EXPERT, e1.astype(f32)), (ROUTE_EXPERT + 1, e2.astype(f32)),
                     (ROUTE_GATE, gate1), (ROUTE_GATE + 1, gate2),
                     (ROUTE_RANK, rank1), (ROUTE_RANK + 1, rank2)):
        record = jnp.where(lane == pos, val, record)
    route_ref[...] = record


def norm_router(x2, g, router_w, router_b):
    n, d = x2.shape
    tm = _row_tile(n, ROUTED_ROWS)
    w_hi, w_lo = _bf16_pieces(jnp.pad(router_w, ((0, 0), (0, LANES - N_EXPERTS))), 2)
    wr = jnp.concatenate([w_hi, w_hi, w_lo], axis=0)
    br = jnp.pad(router_b, (0, LANES - N_EXPERTS), constant_values=NO_EXPERT).reshape(1, LANES)
    ti = np.arange(tm)
    ltri = jnp.asarray(ti[None, :] < ti[:, None], jnp.bfloat16)
    return pl.pallas_call(
        _router_kernel,
        grid=(n // tm,),
        in_specs=[pl.BlockSpec((tm, d), lambda i: (i, 0)), _resident((1, d)),
                  _resident(wr.shape), _resident((1, LANES)), _resident((tm, tm))],
        out_specs=[pl.BlockSpec((tm, d), lambda i: (i, 0)),
                   pl.BlockSpec((tm, LANES), lambda i: (i, 0)),
                   pl.BlockSpec((1, LANES), lambda i: (0, 0))],
        out_shape=[jax.ShapeDtypeStruct((n, d), jnp.bfloat16),
                   jax.ShapeDtypeStruct((n, LANES), jnp.float32),
                   jax.ShapeDtypeStruct((1, LANES), jnp.float32)],
        scratch_shapes=[pltpu.VMEM((1, LANES), jnp.float32)],
        compiler_params=_params("arbitrary"),
        name="norm_router",
    )(x2, g.reshape(1, d), wr, br, ltri)


def _moe_kernel(first, be_ref, nb_ref, xs_ref, wg_ref, wu_ref, wd_ref, *refs):
    o_ref, acc_s = refs[-2:]
    j = first + pl.program_id(0)
    c = pl.program_id(1)
    last = pl.num_programs(1) - 1

    @pl.when(j < nb_ref[0])
    def _():
        h = xs_ref[...]

        @pl.when(c == 0)
        def _():
            acc_s[...] = jnp.zeros_like(acc_s)

        for s, w in _hidden_slices(wg_ref.shape[2]):
            acc_s[...] += _swiglu_slice(h, wg_ref[0, :, s:s + w], wu_ref[0, :, s:s + w],
                                        wd_ref[0, s:s + w, :])

        @pl.when(c == last)
        def _():
            o_ref[...] = acc_s[...].astype(o_ref.dtype)

    @pl.when(jnp.logical_and(j >= nb_ref[0], c == last))
    def _():
        o_ref[...] = jnp.zeros_like(o_ref)


def moe_experts(xs, first, n_blocks_total, y_prev, block_expert, n_used, w_gate, w_up, w_down, tm, tc):
    d = xs.shape[1]
    n_here = xs.shape[0] // tm
    d_e = w_gate.shape[2]
    n_c = d_e // tc

    def live(j, nb):
        return jnp.clip(jnp.minimum(first + j, nb[0] - 1), first, first + n_here - 1)

    def chunk(j, c, nb):
        return jnp.where(first + j < nb[0], c, n_c - 1)

    in_specs = [
        pl.BlockSpec((tm, d), lambda j, c, be, nb: (live(j, nb) - first, 0)),
        pl.BlockSpec((1, d, tc), lambda j, c, be, nb: (be[live(j, nb)], 0, chunk(j, c, nb))),
        pl.BlockSpec((1, d, tc), lambda j, c, be, nb: (be[live(j, nb)], 0, chunk(j, c, nb))),
        pl.BlockSpec((1, tc, d), lambda j, c, be, nb: (be[live(j, nb)], chunk(j, c, nb), 0)),
    ]
    operands = [block_expert, n_used, xs, w_gate, w_up, w_down]
    aliases = {}
    if y_prev is not None:
        in_specs.append(pl.BlockSpec(memory_space=pl.ANY))
        aliases = {len(operands): 0}
        operands.append(y_prev)
    return pl.pallas_call(
        functools.partial(_moe_kernel, first),
        grid_spec=pltpu.PrefetchScalarGridSpec(
            num_scalar_prefetch=2,
            grid=(n_here, n_c),
            in_specs=in_specs,
            out_specs=pl.BlockSpec((tm, d), lambda j, c, be, nb: (first + j, 0)),
            scratch_shapes=[pltpu.VMEM((tm, d), jnp.float32)],
        ),
        out_shape=jax.ShapeDtypeStruct((n_blocks_total * tm, d), jnp.bfloat16),
        input_output_aliases=aliases,
        compiler_params=_params("arbitrary", "arbitrary"),
        name="moe_experts",
    )(*operands)


MOE_STAGES = 4
CAST_BLOCK_BYTES = 4 * 1024 * 1024


def _cast_kernel(x_ref, o_ref):
    o_ref[...] = x_ref[...].astype(o_ref.dtype)


def cast_bf16(w):
    e, rows, cols = w.shape
    want = max(16, CAST_BLOCK_BYTES // (4 * cols))
    blk = _row_tile(rows, 1 << (want.bit_length() - 1))
    spec = pl.BlockSpec((1, blk, cols), lambda i, j: (i, j, 0))
    return pl.pallas_call(
        _cast_kernel,
        grid=(e, rows // blk),
        in_specs=[spec],
        out_specs=spec,
        out_shape=jax.ShapeDtypeStruct(w.shape, jnp.bfloat16),
        compiler_params=_params("parallel", "parallel"),
        name="cast_bf16",
    )(w)


def moe_block(x2, g, router_w, router_b, w_gate, w_up, w_down):
    n, d = x2.shape
    tm = _row_tile(n, ROUTED_ROWS)
    h, route, count = norm_router(x2, g, router_w, router_b)
    expert = route[:, ROUTE_EXPERT:ROUTE_EXPERT + TOP_K].astype(jnp.int32)
    gates = route[:, ROUTE_GATE:ROUTE_GATE + TOP_K]
    rank = route[:, ROUTE_RANK:ROUTE_RANK + TOP_K].astype(jnp.int32)
    counts = count[0, :N_EXPERTS].astype(jnp.int32)
    padded = (counts + tm - 1) // tm * tm
    padded_end = jnp.cumsum(padded)
    padded_start = padded_end - padded
    slot = jnp.take(padded_start, expert) + rank
    n_blocks = -(-(n * TOP_K) // tm) + N_EXPERTS
    n_slots = n_blocks * tm
    token = jnp.broadcast_to(jnp.arange(n, dtype=jnp.int32)[:, None], (n, TOP_K))
    slot_token = jnp.zeros((n_slots,), jnp.int32).at[slot.reshape(-1)].set(
        token.reshape(-1), unique_indices=True, mode="promise_in_bounds")
    block_start = jnp.arange(n_blocks) * tm
    block_expert = jnp.minimum(
        jnp.sum(block_start[:, None] >= padded_end[None, :], axis=1), N_EXPERTS - 1).astype(jnp.int32)
    n_used = (padded_end[-1] // tm).astype(jnp.int32).reshape(1)
    rows_at = lambda table, idx: table.at[idx].get(mode="promise_in_bounds")
    d_e = w_gate.shape[2]
    tc = d_e // 2 if d_e % (2 * MXU_DIM) == 0 else d_e
    weights = [cast_bf16(w) for w in (w_gate, w_up, w_down)]
    n_stages = MOE_STAGES if n_blocks % MOE_STAGES == 0 else 1
    per_stage = n_blocks // n_stages
    y_slots = None
    for stage in range(n_stages):
        first = stage * per_stage
        xs = rows_at(h, slot_token[first * tm:(first + per_stage) * tm])
        y_slots = moe_experts(xs, first, n_blocks, y_slots, block_expert, n_used, *weights, tm, tc)
    out = x2
    for j in range(TOP_K):
        out = out + gates[:, j:j + 1] * rows_at(y_slots, slot[:, j]).astype(jnp.float32)
    return out


GDN_BLOCK = 512
GDN_GROUP = 4
CONV_PAD = 8
GDN_PAIRS = GDN_HEADS // 2
SELECT_TERMS = 3
PAIR_ROWS = 2 * GDN_CHUNK


def _gdn_kernel(aqkv_ref, az_ref, small_ref, convw_ref, aneg_ref, bias_ref, gnorm_ref,
                eeven_ref, eodd_ref, ebeta_ref, bd_ref, ltri_ref, o_ref,
                ext_s, state_s, q_s, k_s, kb_s, qd_s, rhs_s, attn_s, w_s, kdt_s,
                gc_s, kd_s, u_s, oh_s):
    f32, bf16 = jnp.float32, jnp.bfloat16
    c = GDN_CHUNK
    tb = aqkv_ref.shape[0]
    gw = GDN_WIDTH
    nc = tb // c

    @pl.when(pl.program_id(1) == 0)
    def _():
        ext_s[0:CONV_PAD, :] = jnp.zeros((CONV_PAD, 3 * gw), f32)
        state_s[...] = jnp.zeros_like(state_s)

    ext_s[CONV_PAD:CONV_PAD + tb, :] = aqkv_ref[...]
    conv = convw_ref[0:1, :] * ext_s[pl.ds(CONV_PAD - CONV_K + 1, tb), :]
    for j in range(1, CONV_K):
        conv += convw_ref[j:j + 1, :] * ext_s[pl.ds(CONV_PAD - CONV_K + 1 + j, tb), :]
    ext_s[0:CONV_PAD, :] = ext_s[tb:tb + CONV_PAD, :]
    qkv = conv * jax.nn.sigmoid(conv)
    q, k, v = qkv[:, :gw], qkv[:, gw:2 * gw], qkv[:, 2 * gw:]

    bd = bd_ref[...]
    q = q * lax.rsqrt(_group_sum_sq(q, bd) + RMS_EPS) * (HEAD_DIM ** -0.5)
    k = k * lax.rsqrt(_group_sum_sq(k, bd) + RMS_EPS)

    sm = small_ref[...]
    pre = sm + bias_ref[...]
    softplus = jnp.maximum(pre, 0.0) + jnp.log(1.0 + jnp.exp(-jnp.abs(pre)))
    g = _prefix_dot(ltri_ref[...], aneg_ref[...] * softplus, SELECT_TERMS)
    g_even = _select_dot(g, eeven_ref[...])
    g_odd = _select_dot(g, eodd_ref[...])
    bb = _select_dot(jax.nn.sigmoid(sm), ebeta_ref[...])
    lane = lax.broadcasted_iota(jnp.int32, (1, gw), 1)
    even_lane = (lane // HEAD_DIM) % 2 == 0
    gb = jnp.where(even_lane, g_even, g_odd)
    gb3 = gb.reshape(nc, c, gw)
    k_dec = (k.reshape(nc, c, gw) * jnp.exp(gb3[:, c - 1:c, :] - gb3)).reshape(tb, gw)
    eg = jnp.exp(gb)
    kb = k * bb

    def stack(x):
        x3 = x.reshape(nc, c, gw)
        zero = jnp.zeros_like(x3)
        return jnp.concatenate([jnp.where(even_lane, x3, zero), jnp.where(even_lane, zero, x3)], axis=1)

    q_s[...] = stack(q.astype(bf16))
    k_s[...] = stack(k.astype(bf16))
    kb_s[...] = stack(kb.astype(bf16))
    qd_s[...] = stack((q * eg).astype(bf16))
    kd_s[...] = stack(k_dec)
    vb_st = stack((v * bb).astype(bf16))
    kbe_st = stack((kb * eg).astype(bf16))
    for p in range(GDN_PAIRS):
        ps = slice(p * LANES, (p + 1) * LANES)
        rhs_s[:, :, 2 * p * LANES:(2 * p + 1) * LANES] = vb_st[:, :, ps]
        rhs_s[:, :, (2 * p + 1) * LANES:(2 * p + 2) * LANES] = kbe_st[:, :, ps]
    gc_s[...] = jnp.concatenate([g_even.reshape(nc, c, gw), g_odd.reshape(nc, c, gw)], axis=1)

    row = lax.broadcasted_iota(jnp.int32, (PAIR_ROWS, PAIR_ROWS), 0)
    col = lax.broadcasted_iota(jnp.int32, (PAIR_ROWS, PAIR_ROWS), 1)
    causal = (row // c == col // c) & (row >= col)
    strict = causal & (row > col)
    eye = (row == col).astype(f32)

    def phase_a(gi):
        units = [(gi * GDN_GROUP + ci, p) for ci in range(GDN_GROUP) for p in range(GDN_PAIRS)]
        at = lambda ref, n, p: ref[n, :, p * LANES:(p + 1) * LANES]
        k_u = [at(k_s, n, p) for n, p in units]
        kk = [_dot_nt(at(kb_s, n, p), kx) for (n, p), kx in zip(units, k_u)]
        qk = [_dot_nt(at(q_s, n, p), kx) for (n, p), kx in zip(units, k_u)]
        gc = [at(gc_s, n, p) for n, p in units]
        dm = [jnp.where(causal, jnp.exp(jnp.minimum(gx - gx.T, 0.0)), 0.0) for gx in gc]
        a = [jnp.where(strict, x * d, 0.0) for x, d in zip(kk, dm)]
        for (n, p), x, d in zip(units, qk, dm):
            attn_s[n, :, p * LANES:(p + 1) * LANES] = (x * d).astype(bf16)
        for (n, p) in units:
            kdt_s[n, :, p * LANES:(p + 1) * LANES] = at(kd_s, n, p).T.astype(bf16)
        ab = [x.astype(bf16) for x in a]
        sq = [_dot(x, x) for x in ab]
        inv = [eye - x for x in a]
        for _ in range(4):
            sqb = [x.astype(bf16) for x in sq]
            both = [_dot(x, jnp.concatenate([x, t.astype(bf16)], axis=1)) for x, t in zip(sqb, inv)]
            sq = [x[:, :PAIR_ROWS] for x in both]
            inv = [t + x[:, PAIR_ROWS:] for t, x in zip(inv, both)]
        inv = [t + _dot(x.astype(bf16), t.astype(bf16)) for t, x in zip(inv, sq)]
        for (n, p), t in zip(units, inv):
            uw = _dot(t.astype(bf16), rhs_s[n, :, 2 * p * LANES:(2 * p + 2) * LANES])
            u_s[n, :, p * LANES:(p + 1) * LANES] = uw[:, :LANES]
            w_s[n, :, p * LANES:(p + 1) * LANES] = uw[:, LANES:].astype(bf16)

    pairs = range(GDN_PAIRS)

    def phase_b(n, s):
        at = lambda ref, p: ref[n, :, p * LANES:(p + 1) * LANES]
        sb = [x.astype(bf16) for x in s]
        both = [_dot(jnp.concatenate([at(w_s, p), at(qd_s, p)], axis=0), sb[p]) for p in pairs]
        vn = [(at(u_s, p) - both[p][:PAIR_ROWS]).astype(bf16) for p in pairs]
        o_st = [both[p][PAIR_ROWS:] + _dot(at(attn_s, p), vn[p]) for p in pairs]
        upd = [_dot(at(kdt_s, p), vn[p]) for p in pairs]
        new = []
        for p in pairs:
            ps = slice(p * LANES, (p + 1) * LANES)
            g_last = jnp.where(even_lane[:, :LANES], gc_s[n, c - 1:c, ps], gc_s[n, 2 * c - 1:2 * c, ps])
            new.append(s[p] * jnp.exp(g_last) + upd[p])
            oh_s[n * c:(n + 1) * c, ps] = o_st[p][:c] + o_st[p][c:]
        return new

    n_groups = nc // GDN_GROUP
    state = [state_s[p] for p in pairs]
    phase_a(0)
    for gi in range(1, n_groups + 1):
        if gi < n_groups:
            phase_a(gi)
        for n in range((gi - 1) * GDN_GROUP, gi * GDN_GROUP):
            state = phase_b(n, state)
    for p in pairs:
        state_s[p] = state[p]

    o = oh_s[...]
    ms = _group_sum_sq(o, bd) * (1.0 / HEAD_DIM)
    z = az_ref[...]
    o_ref[...] = o * lax.rsqrt(ms + RMS_EPS) * gnorm_ref[...] * (z * jax.nn.sigmoid(z))


def gdn_mixer(proj, bsz, s_len, conv_w, a_log, dt_bias, gdn_norm):
    n = bsz * s_len
    gw, nh, c = GDN_WIDTH, GDN_HEADS, GDN_CHUNK
    tb = _row_tile(s_len, GDN_BLOCK)
    nt = s_len // tb
    nc = tb // c
    assert nc % GDN_GROUP == 0 and nh % 2 == 0
    f32, bf16 = jnp.float32, jnp.bfloat16
    aneg = jnp.zeros((1, LANES), f32).at[0, :nh].set(-jnp.exp(a_log))
    bias = jnp.zeros((1, LANES), f32).at[0, :nh].set(dt_bias)
    gnorm = jnp.tile(gdn_norm, nh).reshape(1, gw)
    lane_head = np.arange(gw) // HEAD_DIM
    src = np.arange(LANES)[:, None]
    stacked_sel = lambda m: jnp.asarray(np.tile(m, (SELECT_TERMS, 1)), bf16)
    eeven = stacked_sel(src == (lane_head // 2 * 2)[None, :])
    eodd = stacked_sel(src == (lane_head // 2 * 2 + 1)[None, :])
    ebeta = stacked_sel(src == nh + lane_head[None, :])
    bd = jnp.asarray(lane_head[:, None] == lane_head[None, :], bf16)
    ti = np.arange(tb)
    ltri = jnp.asarray((ti[:, None] // c == ti[None, :] // c) & (ti[None, :] <= ti[:, None]), bf16)
    rows = lambda width, cb: pl.BlockSpec((tb, width), lambda b, t: (b * nt + t, cb))
    stacked = lambda dt, width=gw: pltpu.VMEM((nc, PAIR_ROWS, width), dt)
    return pl.pallas_call(
        _gdn_kernel,
        grid=(bsz, nt),
        in_specs=[rows(3 * gw, 0), rows(gw, 3), rows(LANES, SMALL_OFF // LANES),
                  _resident((CONV_K, 3 * gw)), _resident((1, LANES)), _resident((1, LANES)),
                  _resident((1, gw)), _resident(eeven.shape), _resident(eodd.shape),
                  _resident(ebeta.shape), _resident((gw, gw)), _resident((tb, tb))],
        out_specs=pl.BlockSpec((tb, gw), lambda b, t: (b * nt + t, 0)),
        out_shape=jax.ShapeDtypeStruct((n, gw), f32),
        scratch_shapes=[
            pltpu.VMEM((tb + CONV_PAD, 3 * gw), f32),
            pltpu.VMEM((GDN_PAIRS, LANES, LANES), f32),
            stacked(bf16), stacked(bf16), stacked(bf16), stacked(bf16),
            stacked(bf16, 2 * gw),
            stacked(bf16), stacked(bf16), stacked(bf16),
            stacked(f32), stacked(f32), stacked(f32),
            pltpu.VMEM((tb, gw), f32)],
        compiler_params=_params("parallel", "arbitrary"),
        name="gdn_mixer",
    )(proj, proj, proj, conv_w, aneg, bias, gnorm, eeven, eodd, ebeta, bd, ltri)


SWA_SPAN = 128
SWA_PREP_ROWS = 512
SWA_GROUP = 8
MASKED_SCORE = -1e30


def _swa_kernel(q_ref, k_ref, v_ref, cos_ref, sin_ref, qn_ref, kn_ref, bd_ref, o_ref,
                q_s, k_s, v_s, op_s, lse_s):
    f32, bf16 = jnp.float32, jnp.bfloat16
    s_len = q_ref.shape[0]
    blk = SWA_SPAN
    lane = lax.broadcasted_iota(jnp.int32, (1, LANES), 1)
    first_half = (lane % HEAD_DIM) < (HEAD_DIM // 2)
    head0 = lane < HEAD_DIM
    bd = bd_ref[...]
    prep_rows = min(SWA_PREP_ROWS, s_len)

    def normed_rotary(x, gain, cos, sin, scale):
        ms = _group_sum_sq(x, bd) * (1.0 / HEAD_DIM)
        y = x * lax.rsqrt(ms + RMS_EPS) * gain
        partner = jnp.where(first_half, pltpu.roll(y, LANES - HEAD_DIM // 2, 1),
                            pltpu.roll(y, HEAD_DIM // 2, 1))
        return (y * cos + partner * sin) * scale

    def prologue(i, carry):
        rows = pl.ds(pl.multiple_of(i * prep_rows, prep_rows), prep_rows)
        cos, sin = cos_ref[rows, :], sin_ref[rows, :]
        q_s[rows, :] = normed_rotary(q_ref[rows, :], qn_ref[...], cos, sin, HEAD_DIM ** -0.5)
        k_s[rows, :] = normed_rotary(k_ref[rows, :], kn_ref[...], cos, sin, 1.0)
        v_s[rows, :] = v_ref[rows, :]
        return carry

    lax.fori_loop(0, s_len // prep_rows, prologue, 0)

    qi = lax.broadcasted_iota(jnp.int32, (blk, 2 * blk), 0)
    kj = lax.broadcasted_iota(jnp.int32, (blk, 2 * blk), 1)
    dist = blk + qi - kj
    band = (dist >= 0) & (dist <= SWA_SPAN)
    ones_kv = jnp.ones((2 * blk, LANES), bf16)

    for pidx, (window, dil) in enumerate(DILATED_PAIRS):
        n_blk = s_len // (dil * blk)

        def rows_of(r, i):
            start = r + i * (blk * dil)
            if dil == 1:
                return pl.ds(start, blk)
            return pl.ds(start, blk, stride=dil)

        def block_group(gi, carry):
            units = [gi * SWA_GROUP + j for j in range(SWA_GROUP)]
            ri = [(u // n_blk, u % n_blk) for u in units]
            cur = [rows_of(r, i) for r, i in ri]
            prev = [rows_of(r, jnp.maximum(i - 1, 0)) for r, i in ri]
            qb = [q_s[c, :] for c in cur]
            kw = [jnp.concatenate([k_s[pv, :], k_s[c, :]], axis=0).astype(bf16)
                  for pv, c in zip(prev, cur)]
            vw = [jnp.concatenate([jnp.concatenate([v_s[pv, :], v_s[c, :]], axis=0).astype(bf16),
                                   ones_kv], axis=1) for pv, c in zip(prev, cur)]
            valid = [band & (kj >= jnp.where(i > 0, 0, blk)) for _, i in ri]
            chains = [(j, h) for j in range(SWA_GROUP) for h in range(2)]
            qh = [jnp.where(head0 if h == 0 else ~head0, qb[j], 0.0).astype(bf16) for j, h in chains]
            s = [jnp.where(valid[j], _dot_nt(x, kw[j]), MASKED_SCORE) for (j, h), x in zip(chains, qh)]
            m = [jnp.max(x, axis=-1, keepdims=True) for x in s]
            p = [jnp.exp((x - mx).astype(bf16)) for x, mx in zip(s, m)]
            pv = [_dot(x, vw[j]) for (j, h), x in zip(chains, p)]
            o = [x[:, :LANES] / x[:, LANES:] for x in pv]
            lse = [mx + jnp.log(x[:, LANES:]) for mx, x in zip(m, pv)]
            for j in range(SWA_GROUP):
                op_s[pidx, cur[j], :] = jnp.where(head0, o[2 * j], o[2 * j + 1])
                lse_s[pidx, cur[j], :] = jnp.where(head0, lse[2 * j], lse[2 * j + 1])
            return carry

        lax.fori_loop(0, s_len // (blk * SWA_GROUP), block_group, 0)

    def epilogue(i, carry):
        rows = pl.ds(pl.multiple_of(i * prep_rows, prep_rows), prep_rows)
        l0, l1, l2 = lse_s[0, rows, :], lse_s[1, rows, :], lse_s[2, rows, :]
        m = jnp.maximum(jnp.maximum(l0, l1), l2)
        w0, w1, w2 = jnp.exp(l0 - m), jnp.exp(l1 - m), jnp.exp(l2 - m)
        mix = w0 * op_s[0, rows, :] + w1 * op_s[1, rows, :] + w2 * op_s[2, rows, :]
        o_ref[rows, :] = mix / (w0 + w1 + w2)
        return carry

    lax.fori_loop(0, s_len // prep_rows, epilogue, 0)


def swa_mixer(proj, bsz, s_len, q_norm, k_norm):
    assert all(w // d == SWA_SPAN for w, d in DILATED_PAIRS) and len(DILATED_PAIRS) == 3
    assert s_len % (SWA_SPAN * max(d for _, d in DILATED_PAIRS)) == 0
    n = bsz * s_len
    f32 = jnp.float32
    n_pairs = SWA_WIDTH // LANES
    half = HEAD_DIM // 2
    inv_freq = ROPE_THETA ** (-jnp.arange(0, HEAD_DIM, 2, dtype=f32) / HEAD_DIM)
    ang = jnp.arange(s_len, dtype=f32)[:, None] * inv_freq[None, :]
    cos = jnp.tile(jnp.cos(ang), (1, LANES // half))
    sin = jnp.tile(jnp.concatenate([-jnp.sin(ang), jnp.sin(ang)], axis=1), (1, LANES // HEAD_DIM))
    gain = lambda g: jnp.tile(g, LANES // HEAD_DIM).reshape(1, LANES)
    lane_head = np.arange(LANES) // HEAD_DIM
    bd = jnp.asarray(lane_head[:, None] == lane_head[None, :], jnp.bfloat16)
    q_col0 = 4 * GDN_WIDTH // LANES
    col = lambda which: pl.BlockSpec(
        (s_len, LANES), lambda b, p: (b, q_col0 + which * n_pairs + p))
    seq = pltpu.VMEM((s_len, LANES), f32)
    per_pattern = pltpu.VMEM((len(DILATED_PAIRS), s_len, LANES), f32)
    return pl.pallas_call(
        _swa_kernel,
        grid=(bsz, n_pairs),
        in_specs=[col(0), col(1), col(2), _resident((s_len, LANES)), _resident((s_len, LANES)),
                  _resident((1, LANES)), _resident((1, LANES)), _resident((LANES, LANES))],
        out_specs=pl.BlockSpec((s_len, LANES), lambda b, p: (b, p)),
        out_shape=jax.ShapeDtypeStruct((n, SWA_WIDTH), f32),
        scratch_shapes=[seq, seq, seq, per_pattern, per_pattern],
        compiler_params=_params("parallel", "parallel"),
        name="swa_mixer",
    )(proj, proj, proj, cos, sin, gain(q_norm), gain(k_norm), bd)


def kernel(x, norm_mix, w_in, conv_w, a_log, dt_bias, gdn_norm, q_norm, k_norm, pool_w,
           pool_scale, w_out, norm_ffn, ffn_gate, ffn_up, ffn_down, router_w, router_b,
           exp_gate, exp_up, exp_down):
    bsz, s_len, d = x.shape
    n = bsz * s_len
    depth = w_in.shape[0]
    x2 = x.reshape(n, d)
    for layer in range(depth):
        proj = norm_inproj(x2, norm_mix[layer], w_in, layer)
        ya = gdn_mixer(proj, bsz, s_len, conv_w[layer], a_log[layer], dt_bias[layer],
                       gdn_norm[layer])
        yb = swa_mixer(proj, bsz, s_len, q_norm[layer], k_norm[layer])
        x2 = pool_outproj_residual(x2, ya, yb, proj, bsz, s_len, pool_w[layer],
                                   pool_scale[layer], w_out[layer])
        i = layer // 2
        if layer % 2 == 0:
            x2 = ffn_dense(x2, norm_ffn[layer], ffn_gate[i], ffn_up[i], ffn_down[i])
        else:
            x2 = moe_block(x2, norm_ffn[layer], router_w[i], router_b[i], exp_gate[i],
                           exp_up[i], exp_down[i])
    return x2.reshape(bsz, s_len, d)
```

```python
import functools

import jax
import jax.numpy as jnp
import numpy as np
from jax import lax
from jax.experimental import pallas as pl
from jax.experimental.pallas import tpu as pltpu

D_MODEL = 1024
HEAD_DIM = 64
MIX_WIDTH = D_MODEL
POOL_WIDTH = MIX_WIDTH // 4
POOL_GROUPS = 4
POOL_GROUP_DIM = POOL_WIDTH // POOL_GROUPS
POOL_SIZES = (2, 4, 8, 16)
GDN_HEADS = (MIX_WIDTH - POOL_WIDTH) // (2 * HEAD_DIM)
GDN_WIDTH = GDN_HEADS * HEAD_DIM
SWA_HEADS = (MIX_WIDTH - POOL_WIDTH - GDN_WIDTH) // HEAD_DIM
SWA_WIDTH = SWA_HEADS * HEAD_DIM
CONV_K = 4
GDN_CHUNK = 64
DILATED_PAIRS = ((128, 1), (512, 4), (2048, 16))
ROPE_THETA = 10000.0
RMS_EPS = 1e-6
N_EXPERTS = 8
TOP_K = 2

VMEM_LIMIT_BYTES = 56 * 1024 * 1024
LANES = 128
MATMUL_ROWS = 512
ROUTED_ROWS = 1024

SMALL_OFF = 7 * GDN_WIDTH
POOL_OFF = SMALL_OFF + LANES
PROJ_WIDTH = POOL_OFF + POOL_WIDTH


def _row_tile(n_rows, want):
    t = min(want, n_rows)
    while n_rows % t:
        t //= 2
    return t


def _params(*sem):
    return pltpu.CompilerParams(dimension_semantics=sem, vmem_limit_bytes=VMEM_LIMIT_BYTES)


def _resident(shape):
    zeros = (0,) * len(shape)
    return pl.BlockSpec(shape, lambda *_: zeros, pipeline_mode=pl.Buffered(1))


def _dot(a, b):
    return jnp.dot(a, b, preferred_element_type=jnp.float32)


def _dot_nt(a, b):
    return lax.dot_general(a, b, (((1,), (1,)), ((), ())), preferred_element_type=jnp.float32)


def _bf16_pieces(x, terms):
    pieces, rest = [], x
    for t in range(terms):
        pieces.append(rest.astype(jnp.bfloat16))
        if t + 1 < terms:
            rest = rest - pieces[-1].astype(jnp.float32)
    return pieces


def _select_dot(x, sel_stacked):
    terms = sel_stacked.shape[0] // x.shape[1]
    return _dot(jnp.concatenate(_bf16_pieces(x, terms), axis=1), sel_stacked)


def _prefix_dot(sel, x, terms):
    width = x.shape[1]
    wide = _dot(sel, jnp.concatenate(_bf16_pieces(x, terms), axis=1))
    return sum(wide[:, t * width:(t + 1) * width] for t in range(terms))


def _group_sum_sq(x, group_ones):
    return _dot((x * x).astype(jnp.bfloat16), group_ones)


def _norm_inproj_kernel(x_ref, g_ref, w_ref, o_ref, wb_s):
    @pl.when(pl.program_id(0) == 0)
    def _():
        bf16 = jnp.bfloat16
        a_end = 4 * GDN_WIDTH
        b_start = a_end + 2 * GDN_HEADS
        b_end = b_start + 3 * SWA_WIDTH
        wb_s[:, :a_end] = w_ref[0, :, :a_end].astype(bf16)
        wb_s[:, a_end:SMALL_OFF] = w_ref[0, :, b_start:b_end].astype(bf16)
        lane = lax.broadcasted_iota(jnp.int32, (1, LANES), 1)
        wb_s[:, SMALL_OFF:POOL_OFF] = jnp.where(
            lane < 2 * GDN_HEADS, w_ref[0, :, a_end:a_end + LANES], 0.0).astype(bf16)
        wb_s[:, POOL_OFF:] = w_ref[0, :, b_end:].astype(bf16)

    x = x_ref[...]
    ms = jnp.mean(x * x, axis=-1, keepdims=True)
    h = (x * lax.rsqrt(ms + RMS_EPS) * g_ref[...]).astype(jnp.bfloat16)
    o_ref[...] = _dot(h, wb_s[...])


def norm_inproj(x2, g, w_in, layer):
    n, d = x2.shape
    in_width = w_in.shape[2]
    assert in_width == 4 * GDN_WIDTH + 2 * GDN_HEADS + 3 * SWA_WIDTH + POOL_WIDTH
    tm = _row_tile(n, MATMUL_ROWS)
    return pl.pallas_call(
        _norm_inproj_kernel,
        grid=(n // tm,),
        in_specs=[pl.BlockSpec((tm, d), lambda i: (i, 0)),
                  _resident((1, d)),
                  pl.BlockSpec((1, d, in_width), lambda i: (layer, 0, 0), pipeline_mode=pl.Buffered(1))],
        out_specs=pl.BlockSpec((tm, PROJ_WIDTH), lambda i: (i, 0)),
        out_shape=jax.ShapeDtypeStruct((n, PROJ_WIDTH), jnp.float32),
        scratch_shapes=[pltpu.VMEM((d, PROJ_WIDTH), jnp.bfloat16)],
        compiler_params=_params("arbitrary"),
        name="norm_inproj",
    )(x2, g.reshape(1, d), w_in)


POOL_HIST = max(POOL_SIZES)


def _pool_outproj_kernel(x_ref, ya_ref, yb_ref, cu_ref, size_ref, pw_ref, ps_ref,
                         w_ref, o_ref, ext_s):
    f32, bf16 = jnp.float32, jnp.bfloat16
    tm = x_ref.shape[0]
    t_blk = pl.program_id(1)

    @pl.when(t_blk == 0)
    def _():
        ext_s[0:POOL_HIST, :] = jnp.zeros((POOL_HIST, POOL_WIDTH), f32)

    cu = cu_ref[...]
    ext_s[POOL_HIST:POOL_HIST + tm, :] = cu
    ext = ext_s[...]
    ext_s[0:POOL_HIST, :] = ext_s[tm:tm + POOL_HIST, :]
    size = size_ref[...]
    win = ext
    total = jnp.zeros_like(ext)
    shift = 1
    while shift < POOL_HIST:
        win = win + pltpu.roll(win, shift, 0)
        shift *= 2
        total = jnp.where(size == shift, win, total)
    total = total[POOL_HIST:, :]
    pos = t_blk * tm + lax.broadcasted_iota(jnp.int32, (tm, 1), 0)
    count = jnp.minimum(pos + 1, size).astype(f32)
    pooled = total / count - cu
    yc = _dot(pooled.astype(bf16), pw_ref[...]) * ps_ref[...]

    mixed = jnp.concatenate([ya_ref[...].astype(bf16), yb_ref[...].astype(bf16), yc.astype(bf16)], axis=1)
    o_ref[...] = x_ref[...] + _dot(mixed, w_ref[...])


def pool_outproj_residual(x2, ya, yb, proj, bsz, s_len, pool_w, pool_scale, w_out):
    n, d = x2.shape
    tm = _row_tile(s_len, ROUTED_ROWS)
    nt = s_len // tm
    bf16 = jnp.bfloat16
    size = jnp.asarray(np.repeat(np.asarray(POOL_SIZES, np.int32), POOL_GROUP_DIM).reshape(1, POOL_WIDTH))
    pw = jax.scipy.linalg.block_diag(*[pool_w[g] for g in range(POOL_GROUPS)]).astype(bf16)
    row = lambda width, cb=0: pl.BlockSpec((tm, width), lambda b, t: (b * nt + t, cb))
    return pl.pallas_call(
        _pool_outproj_kernel,
        grid=(bsz, nt),
        in_specs=[row(d), row(GDN_WIDTH), row(SWA_WIDTH), row(POOL_WIDTH, POOL_OFF // POOL_WIDTH),
                  _resident((1, POOL_WIDTH)), _resident(pw.shape), _resident((1, POOL_WIDTH)),
                  _resident(w_out.shape)],
        out_specs=row(d),
        out_shape=jax.ShapeDtypeStruct((n, d), jnp.float32),
        scratch_shapes=[pltpu.VMEM((tm + POOL_HIST, POOL_WIDTH), jnp.float32)],
        compiler_params=_params("parallel", "arbitrary"),
        name="pool_outproj_residual",
    )(x2, ya, yb, proj, size, pw, pool_scale.reshape(1, POOL_WIDTH), w_out.astype(bf16))


MXU_DIM = 256
HIDDEN_SLICE = 1024


def _hidden_slices(width):
    assert width % MXU_DIM == 0
    return [(s, min(HIDDEN_SLICE, width - s)) for s in range(0, width, HIDDEN_SLICE)]


def _swiglu_slice(h, wg, wu, wd):
    gate = _dot(h, wg)
    up = _dot(h, wu)
    return _dot((gate * jax.nn.sigmoid(gate) * up).astype(jnp.bfloat16), wd)


def _ffn_kernel(x_ref, g_ref, wg_ref, wu_ref, wd_ref, o_ref):
    x = x_ref[...]
    ms = jnp.mean(x * x, axis=-1, keepdims=True)
    h = (x * lax.rsqrt(ms + RMS_EPS) * g_ref[...]).astype(jnp.bfloat16)
    o_ref[...] = x
    for s, w in _hidden_slices(wg_ref.shape[1]):
        o_ref[...] += _swiglu_slice(h, wg_ref[:, s:s + w], wu_ref[:, s:s + w], wd_ref[s:s + w, :])


def ffn_dense(x2, g, w_gate, w_up, w_down):
    n, d = x2.shape
    d_ff = w_gate.shape[1]
    tm = _row_tile(n, MATMUL_ROWS)
    row = pl.BlockSpec((tm, d), lambda i: (i, 0))
    return pl.pallas_call(
        _ffn_kernel,
        grid=(n // tm,),
        in_specs=[row, _resident((1, d)), _resident((d, d_ff)), _resident((d, d_ff)),
                  _resident((d_ff, d))],
        out_specs=row,
        out_shape=jax.ShapeDtypeStruct((n, d), jnp.float32),
        compiler_params=_params("parallel"),
        name="ffn_dense",
    )(x2, g.reshape(1, d), w_gate.astype(jnp.bfloat16), w_up.astype(jnp.bfloat16),
      w_down.astype(jnp.bfloat16))


NO_EXPERT = -1e30
ROUTE_EXPERT, ROUTE_GATE, ROUTE_RANK = 0, TOP_K, 2 * TOP_K


def _router_kernel(x_ref, g_ref, wr_ref, br_ref, ltri_ref, h_ref, route_ref, count_ref, count_s):
    f32 = jnp.float32
    tm = x_ref.shape[0]

    @pl.when(pl.program_id(0) == 0)
    def _():
        count_s[...] = jnp.zeros_like(count_s)

    x = x_ref[...]
    ms = jnp.mean(x * x, axis=-1, keepdims=True)
    h = x * lax.rsqrt(ms + RMS_EPS) * g_ref[...]
    h_hi, h_lo = _bf16_pieces(h, 2)
    h_ref[...] = h_hi
    logits = _dot(jnp.concatenate([h_hi, h_lo, h_hi], axis=1), wr_ref[...]) + br_ref[...]
    lane = lax.broadcasted_iota(jnp.int32, (tm, LANES), 1)

    def top(vals):
        best = jnp.max(vals, axis=-1, keepdims=True)
        return best, jnp.min(jnp.where(vals == best, lane, LANES), axis=-1, keepdims=True)

    m1, e1 = top(logits)
    m2, e2 = top(jnp.where(lane == e1, 2 * NO_EXPERT, logits))
    t = jnp.exp(m2 - m1)
    gate1 = 1.0 / (1.0 + t)
    gate2 = t / (1.0 + t)
    sel1, sel2 = lane == e1, lane == e2
    chosen = (sel1 | sel2).astype(f32)
    before = _dot(ltri_ref[...], chosen.astype(jnp.bfloat16)) + count_s[...]
    rank1 = jnp.sum(jnp.where(sel1, before, 0.0), axis=-1, keepdims=True)
    rank2 = jnp.sum(jnp.where(sel2, before, 0.0), axis=-1, keepdims=True)
    count_s[...] += jnp.sum(chosen, axis=0, keepdims=True)
    count_ref[...] = count_s[...]
    record = jnp.zeros((tm, LANES), f32)
    for pos, val in ((ROUTE_EXPERT, e1.astype(f32)), (ROUTE_EXPERT + 1, e2.astype(f32)),
                     (ROUTE_GATE, gate1), (ROUTE_GATE + 1, gate2),
                     (ROUTE_RANK, rank1), (ROUTE_RANK + 1, rank2)):
        record = jnp.where(lane == pos, val, record)
    route_ref[...] = record


def norm_router(x2, g, router_w, router_b):
    n, d = x2.shape
    tm = _row_tile(n, ROUTED_ROWS)
    w_hi, w_lo = _bf16_pieces(jnp.pad(router_w, ((0, 0), (0, LANES - N_EXPERTS))), 2)
    wr = jnp.concatenate([w_hi, w_hi, w_lo], axis=0)
    br = jnp.pad(router_b, (0, LANES - N_EXPERTS), constant_values=NO_EXPERT).reshape(1, LANES)
    ti = np.arange(tm)
    ltri = jnp.asarray(ti[None, :] < ti[:, None], jnp.bfloat16)
    return pl.pallas_call(
        _router_kernel,
        grid=(n // tm,),
        in_specs=[pl.BlockSpec((tm, d), lambda i: (i, 0)), _resident((1, d)),
                  _resident(wr.shape), _resident((1, LANES)), _resident((tm, tm))],
        out_specs=[pl.BlockSpec((tm, d), lambda i: (i, 0)),
                   pl.BlockSpec((tm, LANES), lambda i: (i, 0)),
                   pl.BlockSpec((1, LANES), lambda i: (0, 0))],
        out_shape=[jax.ShapeDtypeStruct((n, d), jnp.bfloat16),
                   jax.ShapeDtypeStruct((n, LANES), jnp.float32),
                   jax.ShapeDtypeStruct((1, LANES), jnp.float32)],
        scratch_shapes=[pltpu.VMEM((1, LANES), jnp.float32)],
        compiler_params=_params("arbitrary"),
        name="norm_router",
    )(x2, g.reshape(1, d), wr, br, ltri)


def _moe_kernel(first, be_ref, nb_ref, xs_ref, wg_ref, wu_ref, wd_ref, *refs):
    o_ref, acc_s = refs[-2:]
    j = first + pl.program_id(0)
    c = pl.program_id(1)
    last = pl.num_programs(1) - 1

    @pl.when(j < nb_ref[0])
    def _():
        h = xs_ref[...]

        @pl.when(c == 0)
        def _():
            acc_s[...] = jnp.zeros_like(acc_s)

        for s, w in _hidden_slices(wg_ref.shape[2]):
            acc_s[...] += _swiglu_slice(h, wg_ref[0, :, s:s + w], wu_ref[0, :, s:s + w],
                                        wd_ref[0, s:s + w, :])

        @pl.when(c == last)
        def _():
            o_ref[...] = acc_s[...].astype(o_ref.dtype)

    @pl.when(jnp.logical_and(j >= nb_ref[0], c == last))
    def _():
        o_ref[...] = jnp.zeros_like(o_ref)


def moe_experts(xs, first, n_blocks_total, y_prev, block_expert, n_used, w_gate, w_up, w_down, tm, tc):
    d = xs.shape[1]
    n_here = xs.shape[0] // tm
    d_e = w_gate.shape[2]
    n_c = d_e // tc

    def live(j, nb):
        return jnp.clip(jnp.minimum(first + j, nb[0] - 1), first, first + n_here - 1)

    def chunk(j, c, nb):
        return jnp.where(first + j < nb[0], c, n_c - 1)

    in_specs = [
        pl.BlockSpec((tm, d), lambda j, c, be, nb: (live(j, nb) - first, 0)),
        pl.BlockSpec((1, d, tc), lambda j, c, be, nb: (be[live(j, nb)], 0, chunk(j, c, nb))),
        pl.BlockSpec((1, d, tc), lambda j, c, be, nb: (be[live(j, nb)], 0, chunk(j, c, nb))),
        pl.BlockSpec((1, tc, d), lambda j, c, be, nb: (be[live(j, nb)], chunk(j, c, nb), 0)),
    ]
    operands = [block_expert, n_used, xs, w_gate, w_up, w_down]
    aliases = {}
    if y_prev is not None:
        in_specs.append(pl.BlockSpec(memory_space=pl.ANY))
        aliases = {len(operands): 0}
        operands.append(y_prev)
    return pl.pallas_call(
        functools.partial(_moe_kernel, first),
        grid_spec=pltpu.PrefetchScalarGridSpec(
            num_scalar_prefetch=2,
            grid=(n_here, n_c),
            in_specs=in_specs,
            out_specs=pl.BlockSpec((tm, d), lambda j, c, be, nb: (first + j, 0)),
            scratch_shapes=[pltpu.VMEM((tm, d), jnp.float32)],
        ),
        out_shape=jax.ShapeDtypeStruct((n_blocks_total * tm, d), jnp.bfloat16),
        input_output_aliases=aliases,
        compiler_params=_params("arbitrary", "arbitrary"),
        name="moe_experts",
    )(*operands)


MOE_STAGES = 8
CAST_BLOCK_BYTES = 4 * 1024 * 1024


def _cast_kernel(x_ref, o_ref):
    o_ref[...] = x_ref[...].astype(o_ref.dtype)


def cast_bf16(w):
    e, rows, cols = w.shape
    want = max(16, CAST_BLOCK_BYTES // (4 * cols))
    blk = _row_tile(rows, 1 << (want.bit_length() - 1))
    spec = pl.BlockSpec((1, blk, cols), lambda i, j: (i, j, 0))
    return pl.pallas_call(
        _cast_kernel,
        grid=(e, rows // blk),
        in_specs=[spec],
        out_specs=spec,
        out_shape=jax.ShapeDtypeStruct(w.shape, jnp.bfloat16),
        compiler_params=_params("parallel", "parallel"),
        name="cast_bf16",
    )(w)


def moe_block(x2, g, router_w, router_b, w_gate, w_up, w_down):
    n, d = x2.shape
    tm = _row_tile(n, ROUTED_ROWS)
    h, route, count = norm_router(x2, g, router_w, router_b)
    expert = route[:, ROUTE_EXPERT:ROUTE_EXPERT + TOP_K].astype(jnp.int32)
    gates = route[:, ROUTE_GATE:ROUTE_GATE + TOP_K]
    rank = route[:, ROUTE_RANK:ROUTE_RANK + TOP_K].astype(jnp.int32)
    counts = count[0, :N_EXPERTS].astype(jnp.int32)
    padded = (counts + tm - 1) // tm * tm
    padded_end = jnp.cumsum(padded)
    padded_start = padded_end - padded
    slot = jnp.take(padded_start, expert) + rank
    n_blocks = -(-(n * TOP_K) // tm) + N_EXPERTS
    n_slots = n_blocks * tm
    token = jnp.broadcast_to(jnp.arange(n, dtype=jnp.int32)[:, None], (n, TOP_K))
    slot_token = jnp.zeros((n_slots,), jnp.int32).at[slot.reshape(-1)].set(
        token.reshape(-1), unique_indices=True, mode="promise_in_bounds")
    block_start = jnp.arange(n_blocks) * tm
    block_expert = jnp.minimum(
        jnp.sum(block_start[:, None] >= padded_end[None, :], axis=1), N_EXPERTS - 1).astype(jnp.int32)
    n_used = (padded_end[-1] // tm).astype(jnp.int32).reshape(1)
    rows_at = lambda table, idx: table.at[idx].get(mode="promise_in_bounds")
    d_e = w_gate.shape[2]
    tc = d_e // 2 if d_e % (2 * MXU_DIM) == 0 else d_e
    weights = [cast_bf16(w) for w in (w_gate, w_up, w_down)]
    n_stages = MOE_STAGES if n_blocks % MOE_STAGES == 0 else 1
    per_stage = n_blocks // n_stages
    y_slots = None
    for stage in range(n_stages):
        first = stage * per_stage
        xs = rows_at(h, slot_token[first * tm:(first + per_stage) * tm])
        y_slots = moe_experts(xs, first, n_blocks, y_slots, block_expert, n_used, *weights, tm, tc)
    out = x2
    for j in range(TOP_K):
        out = out + gates[:, j:j + 1] * rows_at(y_slots, slot[:, j]).astype(jnp.float32)
    return out


GDN_BLOCK = 512
GDN_GROUP = 4
CONV_PAD = 8
GDN_PAIRS = GDN_HEADS // 2
SELECT_TERMS = 3
PAIR_ROWS = 2 * GDN_CHUNK


def _gdn_kernel(aqkv_ref, az_ref, small_ref, convw_ref, aneg_ref, bias_ref, gnorm_ref,
                eeven_ref, eodd_ref, ebeta_ref, bd_ref, ltri_ref, o_ref,
                ext_s, state_s, q_s, k_s, kb_s, qd_s, rhs_s, attn_s, w_s, kdt_s,
                gc_s, kd_s, u_s, oh_s):
    f32, bf16 = jnp.float32, jnp.bfloat16
    c = GDN_CHUNK
    tb = aqkv_ref.shape[0]
    gw = GDN_WIDTH
    nc = tb // c

    @pl.when(pl.program_id(1) == 0)
    def _():
        ext_s[0:CONV_PAD, :] = jnp.zeros((CONV_PAD, 3 * gw), f32)
        state_s[...] = jnp.zeros_like(state_s)

    ext_s[CONV_PAD:CONV_PAD + tb, :] = aqkv_ref[...]
    conv = convw_ref[0:1, :] * ext_s[pl.ds(CONV_PAD - CONV_K + 1, tb), :]
    for j in range(1, CONV_K):
        conv += convw_ref[j:j + 1, :] * ext_s[pl.ds(CONV_PAD - CONV_K + 1 + j, tb), :]
    ext_s[0:CONV_PAD, :] = ext_s[tb:tb + CONV_PAD, :]
    qkv = conv * jax.nn.sigmoid(conv)
    q, k, v = qkv[:, :gw], qkv[:, gw:2 * gw], qkv[:, 2 * gw:]

    bd = bd_ref[...]
    q = q * lax.rsqrt(_group_sum_sq(q, bd) + RMS_EPS) * (HEAD_DIM ** -0.5)
    k = k * lax.rsqrt(_group_sum_sq(k, bd) + RMS_EPS)

    sm = small_ref[...]
    pre = sm + bias_ref[...]
    softplus = jnp.maximum(pre, 0.0) + jnp.log(1.0 + jnp.exp(-jnp.abs(pre)))
    g = _prefix_dot(ltri_ref[...], aneg_ref[...] * softplus, SELECT_TERMS)
    g_even = _select_dot(g, eeven_ref[...])
    g_odd = _select_dot(g, eodd_ref[...])
    bb = _select_dot(jax.nn.sigmoid(sm), ebeta_ref[...])
    lane = lax.broadcasted_iota(jnp.int32, (1, gw), 1)
    even_lane = (lane // HEAD_DIM) % 2 == 0
    gb = jnp.where(even_lane, g_even, g_odd)
    gb3 = gb.reshape(nc, c, gw)
    k_dec = (k.reshape(nc, c, gw) * jnp.exp(gb3[:, c - 1:c, :] - gb3)).reshape(tb, gw)
    eg = jnp.exp(gb)
    kb = k * bb

    def stack(x):
        x3 = x.reshape(nc, c, gw)
        zero = jnp.zeros_like(x3)
        return jnp.concatenate([jnp.where(even_lane, x3, zero), jnp.where(even_lane, zero, x3)], axis=1)

    q_s[...] = stack(q.astype(bf16))
    k_s[...] = stack(k.astype(bf16))
    kb_s[...] = stack(kb.astype(bf16))
    qd_s[...] = stack((q * eg).astype(bf16))
    kd_s[...] = stack(k_dec)
    vb_st = stack((v * bb).astype(bf16))
    kbe_st = stack((kb * eg).astype(bf16))
    for p in range(GDN_PAIRS):
        ps = slice(p * LANES, (p + 1) * LANES)
        rhs_s[:, :, 2 * p * LANES:(2 * p + 1) * LANES] = vb_st[:, :, ps]
        rhs_s[:, :, (2 * p + 1) * LANES:(2 * p + 2) * LANES] = kbe_st[:, :, ps]
    gc_s[...] = jnp.concatenate([g_even.reshape(nc, c, gw), g_odd.reshape(nc, c, gw)], axis=1)

    row = lax.broadcasted_iota(jnp.int32, (PAIR_ROWS, PAIR_ROWS), 0)
    col = lax.broadcasted_iota(jnp.int32, (PAIR_ROWS, PAIR_ROWS), 1)
    causal = (row // c == col // c) & (row >= col)
    strict = causal & (row > col)
    eye = (row == col).astype(f32)

    def phase_a(gi):
        units = [(gi * GDN_GROUP + ci, p) for ci in range(GDN_GROUP) for p in range(GDN_PAIRS)]
        at = lambda ref, n, p: ref[n, :, p * LANES:(p + 1) * LANES]
        k_u = [at(k_s, n, p) for n, p in units]
        kk = [_dot_nt(at(kb_s, n, p), kx) for (n, p), kx in zip(units, k_u)]
        qk = [_dot_nt(at(q_s, n, p), kx) for (n, p), kx in zip(units, k_u)]
        gc = [at(gc_s, n, p) for n, p in units]
        dm = [jnp.where(causal, jnp.exp(jnp.minimum(gx - gx.T, 0.0)), 0.0) for gx in gc]
        a = [jnp.where(strict, x * d, 0.0) for x, d in zip(kk, dm)]
        for (n, p), x, d in zip(units, qk, dm):
            attn_s[n, :, p * LANES:(p + 1) * LANES] = (x * d).astype(bf16)
        for (n, p) in units:
            kdt_s[n, :, p * LANES:(p + 1) * LANES] = at(kd_s, n, p).T.astype(bf16)
        ab = [x.astype(bf16) for x in a]
        sq = [_dot(x, x) for x in ab]
        inv = [eye - x for x in a]
        for _ in range(4):
            sqb = [x.astype(bf16) for x in sq]
            both = [_dot(x, jnp.concatenate([x, t.astype(bf16)], axis=1)) for x, t in zip(sqb, inv)]
            sq = [x[:, :PAIR_ROWS] for x in both]
            inv = [t + x[:, PAIR_ROWS:] for t, x in zip(inv, both)]
        inv = [t + _dot(x.astype(bf16), t.astype(bf16)) for t, x in zip(inv, sq)]
        for (n, p), t in zip(units, inv):
            uw = _dot(t.astype(bf16), rhs_s[n, :, 2 * p * LANES:(2 * p + 2) * LANES])
            u_s[n, :, p * LANES:(p + 1) * LANES] = uw[:, :LANES]
            w_s[n, :, p * LANES:(p + 1) * LANES] = uw[:, LANES:].astype(bf16)

    pairs = range(GDN_PAIRS)

    def phase_b(n, s):
        at = lambda ref, p: ref[n, :, p * LANES:(p + 1) * LANES]
        sb = [x.astype(bf16) for x in s]
        both = [_dot(jnp.concatenate([at(w_s, p), at(qd_s, p)], axis=0), sb[p]) for p in pairs]
        vn = [(at(u_s, p) - both[p][:PAIR_ROWS]).astype(bf16) for p in pairs]
        o_st = [both[p][PAIR_ROWS:] + _dot(at(attn_s, p), vn[p]) for p in pairs]
        upd = [_dot(at(kdt_s, p), vn[p]) for p in pairs]
        new = []
        for p in pairs:
            ps = slice(p * LANES, (p + 1) * LANES)
            g_last = jnp.where(even_lane[:, :LANES], gc_s[n, c - 1:c, ps], gc_s[n, 2 * c - 1:2 * c, ps])
            new.append(s[p] * jnp.exp(g_last) + upd[p])
            oh_s[n * c:(n + 1) * c, ps] = o_st[p][:c] + o_st[p][c:]
        return new

    n_groups = nc // GDN_GROUP
    state = [state_s[p] for p in pairs]
    phase_a(0)
    for gi in range(1, n_groups + 1):
        if gi < n_groups:
            phase_a(gi)
        for n in range((gi - 1) * GDN_GROUP, gi * GDN_GROUP):
            state = phase_b(n, state)
    for p in pairs:
        state_s[p] = state[p]

    o = oh_s[...]
    ms = _group_sum_sq(o, bd) * (1.0 / HEAD_DIM)
    z = az_ref[...]
    o_ref[...] = o * lax.rsqrt(ms + RMS_EPS) * gnorm_ref[...] * (z * jax.nn.sigmoid(z))


def gdn_mixer(proj, bsz, s_len, conv_w, a_log, dt_bias, gdn_norm):
    n = bsz * s_len
    gw, nh, c = GDN_WIDTH, GDN_HEADS, GDN_CHUNK
    tb = _row_tile(s_len, GDN_BLOCK)
    nt = s_len // tb
    nc = tb // c
    assert nc % GDN_GROUP == 0 and nh % 2 == 0
    f32, bf16 = jnp.float32, jnp.bfloat16
    aneg = jnp.zeros((1, LANES), f32).at[0, :nh].set(-jnp.exp(a_log))
    bias = jnp.zeros((1, LANES), f32).at[0, :nh].set(dt_bias)
    gnorm = jnp.tile(gdn_norm, nh).reshape(1, gw)
    lane_head = np.arange(gw) // HEAD_DIM
    src = np.arange(LANES)[:, None]
    stacked_sel = lambda m: jnp.asarray(np.tile(m, (SELECT_TERMS, 1)), bf16)
    eeven = stacked_sel(src == (lane_head // 2 * 2)[None, :])
    eodd = stacked_sel(src == (lane_head // 2 * 2 + 1)[None, :])
    ebeta = stacked_sel(src == nh + lane_head[None, :])
    bd = jnp.asarray(lane_head[:, None] == lane_head[None, :], bf16)
    ti = np.arange(tb)
    ltri = jnp.asarray((ti[:, None] // c == ti[None, :] // c) & (ti[None, :] <= ti[:, None]), bf16)
    rows = lambda width, cb: pl.BlockSpec((tb, width), lambda b, t: (b * nt + t, cb))
    stacked = lambda dt, width=gw: pltpu.VMEM((nc, PAIR_ROWS, width), dt)
    return pl.pallas_call(
        _gdn_kernel,
        grid=(bsz, nt),
        in_specs=[rows(3 * gw, 0), rows(gw, 3), rows(LANES, SMALL_OFF // LANES),
                  _resident((CONV_K, 3 * gw)), _resident((1, LANES)), _resident((1, LANES)),
                  _resident((1, gw)), _resident(eeven.shape), _resident(eodd.shape),
                  _resident(ebeta.shape), _resident((gw, gw)), _resident((tb, tb))],
        out_specs=pl.BlockSpec((tb, gw), lambda b, t: (b * nt + t, 0)),
        out_shape=jax.ShapeDtypeStruct((n, gw), f32),
        scratch_shapes=[
            pltpu.VMEM((tb + CONV_PAD, 3 * gw), f32),
            pltpu.VMEM((GDN_PAIRS, LANES, LANES), f32),
            stacked(bf16), stacked(bf16), stacked(bf16), stacked(bf16),
            stacked(bf16, 2 * gw),
            stacked(bf16), stacked(bf16), stacked(bf16),
            stacked(f32), stacked(f32), stacked(f32),
            pltpu.VMEM((tb, gw), f32)],
        compiler_params=_params("parallel", "arbitrary"),
        name="gdn_mixer",
    )(proj, proj, proj, conv_w, aneg, bias, gnorm, eeven, eodd, ebeta, bd, ltri)


SWA_SPAN = 128
SWA_PREP_ROWS = 512
SWA_GROUP = 16
MASKED_SCORE = -1e30


def _swa_kernel(q_ref, k_ref, v_ref, cos_ref, sin_ref, qn_ref, kn_ref, bd_ref, o_ref,
                q_s, k_s, v_s, op_s, lse_s):
    f32, bf16 = jnp.float32, jnp.bfloat16
    s_len = q_ref.shape[0]
    blk = SWA_SPAN
    lane = lax.broadcasted_iota(jnp.int32, (1, LANES), 1)
    first_half = (lane % HEAD_DIM) < (HEAD_DIM // 2)
    head0 = lane < HEAD_DIM
    bd = bd_ref[...]
    prep_rows = min(SWA_PREP_ROWS, s_len)

    def normed_rotary(x, gain, cos, sin, scale):
        ms = _group_sum_sq(x, bd) * (1.0 / HEAD_DIM)
        y = x * lax.rsqrt(ms + RMS_EPS) * gain
        partner = jnp.where(first_half, pltpu.roll(y, LANES - HEAD_DIM // 2, 1),
                            pltpu.roll(y, HEAD_DIM // 2, 1))
        return (y * cos + partner * sin) * scale

    def prologue(i, carry):
        rows = pl.ds(pl.multiple_of(i * prep_rows, prep_rows), prep_rows)
        cos, sin = cos_ref[rows, :], sin_ref[rows, :]
        q_s[rows, :] = normed_rotary(q_ref[rows, :], qn_ref[...], cos, sin, HEAD_DIM ** -0.5)
        k_s[rows, :] = normed_rotary(k_ref[rows, :], kn_ref[...], cos, sin, 1.0)
        v_s[rows, :] = v_ref[rows, :]
        return carry

    lax.fori_loop(0, s_len // prep_rows, prologue, 0)

    qi = lax.broadcasted_iota(jnp.int32, (blk, 2 * blk), 0)
    kj = lax.broadcasted_iota(jnp.int32, (blk, 2 * blk), 1)
    dist = blk + qi - kj
    band = (dist >= 0) & (dist <= SWA_SPAN)
    ones_kv = jnp.ones((2 * blk, LANES), bf16)

    for pidx, (window, dil) in enumerate(DILATED_PAIRS):
        n_blk = s_len // (dil * blk)

        def rows_of(r, i):
            start = r + i * (blk * dil)
            if dil == 1:
                return pl.ds(start, blk)
            return pl.ds(start, blk, stride=dil)

        def block_group(gi, carry):
            units = [gi * SWA_GROUP + j for j in range(SWA_GROUP)]
            ri = [(u // n_blk, u % n_blk) for u in units]
            cur = [rows_of(r, i) for r, i in ri]
            prev = [rows_of(r, jnp.maximum(i - 1, 0)) for r, i in ri]
            qb = [q_s[c, :] for c in cur]
            kw = [jnp.concatenate([k_s[pv, :], k_s[c, :]], axis=0).astype(bf16)
                  for pv, c in zip(prev, cur)]
            vw = [jnp.concatenate([jnp.concatenate([v_s[pv, :], v_s[c, :]], axis=0).astype(bf16),
                                   ones_kv], axis=1) for pv, c in zip(prev, cur)]
            valid = [band & (kj >= jnp.where(i > 0, 0, blk)) for _, i in ri]
            chains = [(j, h) for j in range(SWA_GROUP) for h in range(2)]
            qh = [jnp.where(head0 if h == 0 else ~head0, qb[j], 0.0).astype(bf16) for j, h in chains]
            s = [jnp.where(valid[j], _dot_nt(x, kw[j]), MASKED_SCORE) for (j, h), x in zip(chains, qh)]
            m = [jnp.max(x, axis=-1, keepdims=True) for x in s]
            p = [jnp.exp((x - mx).astype(bf16)) for x, mx in zip(s, m)]
            pv = [_dot(x, vw[j]) for (j, h), x in zip(chains, p)]
            o = [x[:, :LANES] / x[:, LANES:] for x in pv]
            lse = [mx + jnp.log(x[:, LANES:]) for mx, x in zip(m, pv)]
            for j in range(SWA_GROUP):
                op_s[pidx, cur[j], :] = jnp.where(head0, o[2 * j], o[2 * j + 1])
                lse_s[pidx, cur[j], :] = jnp.where(head0, lse[2 * j], lse[2 * j + 1])
            return carry

        lax.fori_loop(0, s_len // (blk * SWA_GROUP), block_group, 0)

    def epilogue(i, carry):
        rows = pl.ds(pl.multiple_of(i * prep_rows, prep_rows), prep_rows)
        l0, l1, l2 = lse_s[0, rows, :], lse_s[1, rows, :], lse_s[2, rows, :]
        m = jnp.maximum(jnp.maximum(l0, l1), l2)
        w0, w1, w2 = jnp.exp(l0 - m), jnp.exp(l1 - m), jnp.exp(l2 - m)
        mix = w0 * op_s[0, rows, :] + w1 * op_s[1, rows, :] + w2 * op_s[2, rows, :]
        o_ref[rows, :] = mix / (w0 + w1 + w2)
        return carry

    lax.fori_loop(0, s_len // prep_rows, epilogue, 0)


def swa_mixer(proj, bsz, s_len, q_norm, k_norm):
    assert all(w // d == SWA_SPAN for w, d in DILATED_PAIRS) and len(DILATED_PAIRS) == 3
    assert s_len % (SWA_SPAN * max(d for _, d in DILATED_PAIRS)) == 0
    n = bsz * s_len
    f32 = jnp.float32
    n_pairs = SWA_WIDTH // LANES
    half = HEAD_DIM // 2
    inv_freq = ROPE_THETA ** (-jnp.arange(0, HEAD_DIM, 2, dtype=f32) / HEAD_DIM)
    ang = jnp.arange(s_len, dtype=f32)[:, None] * inv_freq[None, :]
    cos = jnp.tile(jnp.cos(ang), (1, LANES // half))
    sin = jnp.tile(jnp.concatenate([-jnp.sin(ang), jnp.sin(ang)], axis=1), (1, LANES // HEAD_DIM))
    gain = lambda g: jnp.tile(g, LANES // HEAD_DIM).reshape(1, LANES)
    lane_head = np.arange(LANES) // HEAD_DIM
    bd = jnp.asarray(lane_head[:, None] == lane_head[None, :], jnp.bfloat16)
    q_col0 = 4 * GDN_WIDTH // LANES
    col = lambda which: pl.BlockSpec(
        (s_len, LANES), lambda b, p: (b, q_col0 + which * n_pairs + p))
    seq = pltpu.VMEM((s_len, LANES), f32)
    per_pattern = pltpu.VMEM((len(DILATED_PAIRS), s_len, LANES), f32)
    return pl.pallas_call(
        _swa_kernel,
        grid=(bsz, n_pairs),
        in_specs=[col(0), col(1), col(2), _resident((s_len, LANES)), _resident((s_len, LANES)),
                  _resident((1, LANES)), _resident((1, LANES)), _resident((LANES, LANES))],
        out_specs=pl.BlockSpec((s_len, LANES), lambda b, p: (b, p)),
        out_shape=jax.ShapeDtypeStruct((n, SWA_WIDTH), f32),
        scratch_shapes=[seq, seq, seq, per_pattern, per_pattern],
        compiler_params=_params("parallel", "parallel"),
        name="swa_mixer",
    )(proj, proj, proj, cos, sin, gain(q_norm), gain(k_norm), bd)


def kernel(x, norm_mix, w_in, conv_w, a_log, dt_bias, gdn_norm, q_norm, k_norm, pool_w,
           pool_scale, w_out, norm_ffn, ffn_gate, ffn_up, ffn_down, router_w, router_b,
           exp_gate, exp_up, exp_down):
    bsz, s_len, d = x.shape
    n = bsz * s_len
    depth = w_in.shape[0]
    x2 = x.reshape(n, d)
    for layer in range(depth):
        proj = norm_inproj(x2, norm_mix[layer], w_in, layer)
        ya = gdn_mixer(proj, bsz, s_len, conv_w[layer], a_log[layer], dt_bias[layer],
                       gdn_norm[layer])
        yb = swa_mixer(proj, bsz, s_len, q_norm[layer], k_norm[layer])
        x2 = pool_outproj_residual(x2, ya, yb, proj, bsz, s_len, pool_w[layer],
                                   pool_scale[layer], w_out[layer])
        i = layer // 2
        if layer % 2 == 0:
            x2 = ffn_dense(x2, norm_ffn[layer], ffn_gate[i], ffn_up[i], ffn_down[i])
        else:
            x2 = moe_block(x2, norm_ffn[layer], router_w[i], router_b[i], exp_gate[i],
                           exp_up[i], exp_down[i])
    return x2.reshape(bsz, s_len, d)
```
